```python
import jax
import jax.numpy as jnp
from jax import lax
import numpy as np

D_MODEL = 1024
BATCH = 32
SEQ = 256
DEPTH = 4
DEC_BATCH = 4
DEC_SEQ = 2048
PAST_LEN = 256

GRID_W = 64
D_MIX = D_MODEL
M_HEADS = 4
M_DK = 32
M_DV = 64
G_HEADS = 4
G_DK = 32
G_DV = 64
G_RANK = 16
G_TAU = 16.0
A_HEADS = 8
A_DNOPE = 64
A_DROPE = 32
A_DV = 64
A_DQ = 256
A_DC = 128
A_SCALE = (A_DNOPE + A_DROPE) ** -0.5
CHUNK = 64
Q_BLOCK = 128
ROPE_BASE = 10000.0
FFN_HIDDEN = ((8 * D_MODEL + 3 * 256 - 1) // (3 * 256)) * 256
ALPHA = (2 * DEPTH) ** 0.25
BETA = (8 * DEPTH) ** -0.25
IN_SPLITS = (M_HEADS * M_DK, M_HEADS * M_DK, M_HEADS * M_DV, M_HEADS * M_DV, 4 * M_HEADS,
             G_HEADS * G_DK, G_HEADS * G_DK, G_HEADS * G_DV, G_HEADS * G_DV, 2 * G_RANK,
             A_DQ, A_DC, A_DROPE)
IN_COLS = sum(IN_SPLITS)

kernel_name = 'hybrid_mlstm_gla_mla_diffusion_step'


def layer_norm(x, g, b, eps=1e-5):
    xf = x.astype(jnp.float32)
    xc = xf - xf.mean(-1, keepdims=True)
    y = xc * lax.rsqrt((xc * xc).mean(-1, keepdims=True) + eps)
    return y.astype(x.dtype) * g + b


def rms_norm(x, g, eps=1e-6):
    xf = x.astype(jnp.float32)
    y = xf * lax.rsqrt((xf * xf).mean(-1, keepdims=True) + eps)
    return y.astype(x.dtype) * g


def head_norm(x, g, n_heads, center, eps=1e-6):
    b, t, w = x.shape
    xf = x.astype(jnp.float32).reshape(b, t, n_heads, w // n_heads)
    if center:
        xf = xf - xf.mean(-1, keepdims=True)
    y = xf * lax.rsqrt((xf * xf).mean(-1, keepdims=True) + eps)
    return y.reshape(b, t, w).astype(x.dtype) * g


def to_heads(t, n):
    b, s, _ = t.shape
    return t.reshape(b, s, n, -1).transpose(0, 2, 1, 3)


def from_heads(t):
    b, h, s, d = t.shape
    return t.transpose(0, 2, 1, 3).reshape(b, s, h * d)


def flip_t(t):
    return jnp.flip(t, axis=2)


def to_chunks(t):
    b, h, s = t.shape[:3]
    return jnp.moveaxis(t.reshape((b, h, s // CHUNK, CHUNK) + t.shape[3:]), 2, 0)


def from_chunks(t):
    t = jnp.moveaxis(t, 0, 2)
    return t.reshape(t.shape[:2] + (-1,) + t.shape[4:])


def rope_axis(x, pos):
    half = x.shape[-1] // 2
    inv = ROPE_BASE ** (-jnp.arange(half, dtype=jnp.float32) / half)
    ang = pos.astype(jnp.float32)[:, None] * inv[None, :]
    cos = jnp.cos(ang).astype(x.dtype)
    sin = jnp.sin(ang).astype(x.dtype)
    x1, x2 = x[..., :half], x[..., half:]
    return jnp.concatenate([x1 * cos - x2 * sin, x2 * cos + x1 * sin], axis=-1)


def rope_2d(x, rows, cols):
    d = x.shape[-1] // 2
    return jnp.concatenate([rope_axis(x[..., :d], rows), rope_axis(x[..., d:], cols)], axis=-1)


def mlstm_scan(q, k, v, ig, lf, C0, n0, m0):
    dt = q.dtype
    f32 = jnp.float32
    mask = jnp.tril(jnp.ones((CHUNK, CHUNK), bool))
    xs = (to_chunks(q.astype(f32)), to_chunks(k.astype(f32) * (M_DK ** -0.5)), to_chunks(v.astype(f32)),
          to_chunks(ig.astype(f32)), to_chunks(lf.astype(f32)))

    def step(carry, inp):
        C, n, m = carry
        qc, kc, vc, ic, fc = inp
        b = jnp.cumsum(fc, axis=-1)
        a = b + m[..., None]
        dmat = jnp.where(mask, b[..., :, None] - b[..., None, :] + ic[..., None, :], -jnp.inf)
        mloc = jnp.maximum(a, dmat.max(-1))
        s = jnp.einsum('bhtd,bhsd->bhts', qc, kc) * jnp.exp(dmat - mloc[..., None])
        inter = jnp.exp(a - mloc)
        num = jnp.einsum('bhts,bhsv->bhtv', s, vc) + inter[..., None] * jnp.einsum('bhtd,bhdv->bhtv', qc, C)
        den = s.sum(-1) + inter * jnp.einsum('bhtd,bhd->bht', qc, n)
        h = num / jnp.maximum(jnp.abs(den), jnp.exp(-mloc))[..., None]
        bl = b[..., -1]
        g = bl[..., None] - b + ic
        m_new = jnp.maximum(bl + m, g.max(-1))
        w = jnp.exp(g - m_new[..., None])
        decay = jnp.exp(bl + m - m_new)
        C_new = decay[..., None, None] * C + jnp.einsum('bhs,bhsd,bhsv->bhdv', w, kc, vc)
        n_new = decay[..., None] * n + jnp.einsum('bhs,bhsd->bhd', w, kc)
        return (C_new, n_new, m_new), h

    (C, n, m), hs = lax.scan(step, (C0.astype(f32), n0.astype(f32), m0.astype(f32)), xs)
    return from_chunks(hs).astype(dt), (C.astype(dt), n.astype(dt), m.astype(dt))


def gla_scan(q, k, v, lg, S0):
    dt = q.dtype
    f32 = jnp.float32
    mask3 = jnp.tril(jnp.ones((CHUNK, CHUNK), bool))[:, :, None]
    xs = (to_chunks(q.astype(f32) * (G_DK ** -0.5)), to_chunks(k.astype(f32)), to_chunks(v.astype(f32)),
          to_chunks(lg.astype(f32)))

    def step(S, inp):
        qc, kc, vc, gc = inp
        bc = jnp.cumsum(gc, axis=2)
        inter = jnp.einsum('bhtd,bhdv->bhtv', qc * jnp.exp(bc), S)
        diff = bc[:, :, :, None, :] - bc[:, :, None, :, :]
        dec = jnp.exp(jnp.where(mask3, diff, -jnp.inf))
        att = jnp.einsum('bhtd,bhsd,bhtsd->bhts', qc, kc, dec)
        out = inter + jnp.einsum('bhts,bhsv->bhtv', att, vc)
        bl = bc[:, :, -1:, :]
        S_new = jnp.exp(bl[:, :, 0])[..., None] * S + jnp.einsum('bhsd,bhsv->bhdv', kc * jnp.exp(bl - bc), vc)
        return S_new, out

    S, outs = lax.scan(step, S0.astype(f32), xs)
    return from_chunks(outs).astype(dt), S.astype(dt)


def mla_attention(q_nope, q_rope, k_nope, k_rope, v):
    b, h, tq, _ = q_nope.shape
    nb = tq // Q_BLOCK
    qn = jnp.moveaxis(q_nope.reshape(b, h, nb, Q_BLOCK, -1), 2, 0)
    qr = jnp.moveaxis(q_rope.reshape(b, h, nb, Q_BLOCK, -1), 2, 0)

    def block(args):
        qnb, qrb = args
        s = jnp.einsum('bhqd,bhkd->bhqk', qnb, k_nope) + jnp.einsum('bhqr,bkr->bhqk', qrb, k_rope)
        p = jax.nn.softmax(s.astype(jnp.float32) * A_SCALE, axis=-1)
        return jnp.einsum('bhqk,bhkv->bhqv', p.astype(v.dtype), v)

    o = lax.map(block, (qn, qr))
    return jnp.moveaxis(o, 0, 2).reshape(b, h, tq, -1)


def mixer(h, lp, init_states, ctx_kv, pos):
    bsz, t, _ = h.shape
    points = [int(p) for p in np.cumsum(IN_SPLITS)[:-1]]
    (mq, mk, mv, mo, mg, gq, gk, gv, gg, ga, acq, ackv, akr) = jnp.split(h @ lp['w_in'], points, axis=-1)
    C0, n0, m0, S0 = init_states

    q, k, v = to_heads(mq, M_HEADS), to_heads(mk, M_HEADS), to_heads(mv, M_HEADS)
    gates = (mg.astype(jnp.float32) + lp['m_gate_b'].astype(jnp.float32)).reshape(bsz, t, 4, M_HEADS)
    gates = gates.transpose(2, 0, 3, 1)
    hf, (Cf, nf, mf) = mlstm_scan(q, k, v, gates[0], jax.nn.log_sigmoid(gates[1]), C0[:, 0], n0[:, 0], m0[:, 0])
    hb, (Cb, nb, mb) = mlstm_scan(flip_t(q), flip_t(k), flip_t(v), flip_t(gates[2]),
                                  flip_t(jax.nn.log_sigmoid(gates[3])), C0[:, 1], n0[:, 1], m0[:, 1])
    m_out = head_norm(from_heads(hf + flip_t(hb)), lp['m_norm_g'], M_HEADS, True) * jax.nn.sigmoid(mo)

    q, k, v = to_heads(gq, G_HEADS), to_heads(gk, G_HEADS), to_heads(gv, G_HEADS)
    lg_f = to_heads(jax.nn.log_sigmoid((ga[..., :G_RANK] @ lp['g_w2'][0] + lp['g_b2'][0]).astype(jnp.float32)) / G_TAU, G_HEADS)
    lg_b = to_heads(jax.nn.log_sigmoid((ga[..., G_RANK:] @ lp['g_w2'][1] + lp['g_b2'][1]).astype(jnp.float32)) / G_TAU, G_HEADS)
    of, Sf = gla_scan(q, k, v, lg_f, S0[:, 0])
    ob, Sb = gla_scan(flip_t(q), flip_t(k), flip_t(v), flip_t(lg_b), S0[:, 1])
    g_out = head_norm(from_heads(of + flip_t(ob)), lp['g_norm_g'], G_HEADS, False) * jax.nn.silu(gg)

    cq = rms_norm(acq, lp['a_q_norm_g'])
    qf = to_heads(cq @ lp['a_w_uq'], A_HEADS)
    q_nope, q_rope = qf[..., :A_DNOPE], qf[..., A_DNOPE:]
    ckv = rms_norm(ackv, lp['a_kv_norm_g'])
    if pos is None:
        kr_own = akr
    else:
        rows, cols = pos
        q_rope = rope_2d(q_rope, rows, cols)
        kr_own = rope_2d(akr, rows, cols)
    if ctx_kv is None:
        ckv_all, kr_all = ckv, kr_own
    else:
        ckv_all = jnp.concatenate([ctx_kv[0], ckv], axis=1)
        kr_all = jnp.concatenate([ctx_kv[1], kr_own], axis=1)
    kv = to_heads(ckv_all @ lp['a_w_ukv'], A_HEADS)
    a_out = from_heads(mla_attention(q_nope, q_rope, kv[..., :A_DNOPE], kr_all, kv[..., A_DNOPE:]))

    out = jnp.concatenate([m_out, g_out, a_out], axis=-1) @ lp['w_out']
    new = (jnp.stack([Cf, Cb], axis=1), jnp.stack([nf, nb], axis=1), jnp.stack([mf, mb], axis=1),
           jnp.stack([Sf, Sb], axis=1), ckv, akr)
    return out, new


def layer(x, cond, lp, init_states, ctx_kv, pos):
    mod = jax.nn.silu(cond) @ lp['w_ada'] + lp['b_ada']
    sh1, sc1, g1, sh2, sc2, g2 = jnp.split(mod[:, None, :], 6, axis=-1)
    mix, new = mixer(x * (1 + sc1) + sh1, lp, init_states, ctx_kv, pos)
    x = layer_norm(ALPHA * x + g1 * mix, lp['ln1_g'], lp['ln1_b'])
    hh = x * (1 + sc2) + sh2
    ffn = (jax.nn.silu(hh @ lp['w_ffn_gate']) * (hh @ lp['w_ffn_up'])) @ lp['w_ffn_down']
    x = layer_norm(ALPHA * x + g2 * ffn, lp['ln2_g'], lp['ln2_b'])
    return x, new


def setup_inputs(seed: int = 0) -> dict:
    key = jax.random.key(seed)
    ks = jax.random.split(key, 40)

    def nrm(i, shape, scale=1.0):
        return jax.random.normal(ks[i], shape, jnp.float32) * scale

    D, L = D_MODEL, DEPTH
    gate_base = jnp.repeat(jnp.array([0.0, 3.0, 0.0, 3.0], jnp.float32), M_HEADS)
    return {
        'x_prompt': nrm(0, (BATCH, SEQ, D)),
        'x_sample': nrm(1, (DEC_BATCH, DEC_SEQ, D)),
        'state_mlstm_C': nrm(2, (DEC_BATCH, L, 2, M_HEADS, M_DK, M_DV), 0.5),
        'state_mlstm_n': nrm(3, (DEC_BATCH, L, 2, M_HEADS, M_DK), 0.5),
        'state_mlstm_m': nrm(4, (DEC_BATCH, L, 2, M_HEADS)),
        'state_gla_S': nrm(5, (DEC_BATCH, L, 2, G_HEADS, G_DK, G_DV), 0.5),
        'cache_mla_ckv': nrm(6, (DEC_BATCH, L, PAST_LEN, A_DC)),
        'cache_mla_krope': nrm(7, (DEC_BATCH, L, PAST_LEN, A_DROPE)),
        'c': nrm(8, (DEC_BATCH, D)),
        'c_ctx': nrm(9, (D,)),
        'w_ada': nrm(10, (L, D, 6 * D), 0.5 * D ** -0.5),
        'b_ada': nrm(11, (L, 6 * D), 0.02),
        'w_in': nrm(12, (L, D, IN_COLS), D ** -0.5),
        'm_gate_b': gate_base + nrm(13, (L, 4 * M_HEADS), 0.1),
        'm_norm_g': 1.0 + nrm(14, (L, M_HEADS * M_DV), 0.02),
        'g_w2': nrm(15, (L, 2, G_RANK, G_HEADS * G_DK), G_RANK ** -0.5),
        'g_b2': nrm(16, (L, 2, G_HEADS * G_DK), 0.1),
        'g_norm_g': 1.0 + nrm(17, (L, G_HEADS * G_DV), 0.02),
        'a_q_norm_g': 1.0 + nrm(18, (L, A_DQ), 0.02),
        'a_kv_norm_g': 1.0 + nrm(19, (L, A_DC), 0.02),
        'a_w_uq': nrm(20, (L, A_DQ, A_HEADS * (A_DNOPE + A_DROPE)), A_DQ ** -0.5),
        'a_w_ukv': nrm(21, (L, A_DC, A_HEADS * (A_DNOPE + A_DV)), A_DC ** -0.5),
        'w_out': nrm(22, (L, D_MIX, D), BETA * D_MIX ** -0.5),
        'ln1_g': 1.0 + nrm(23, (L, D), 0.02),
        'ln1_b': nrm(24, (L, D), 0.02),
        'w_ffn_gate': nrm(25, (L, D, FFN_HIDDEN), D ** -0.5),
        'w_ffn_up': nrm(26, (L, D, FFN_HIDDEN), D ** -0.5),
        'w_ffn_down': nrm(27, (L, FFN_HIDDEN, D), BETA * FFN_HIDDEN ** -0.5),
        'ln2_g': 1.0 + nrm(28, (L, D), 0.02),
        'ln2_b': nrm(29, (L, D), 0.02),
    }


def reference(x_prompt, x_sample, state_mlstm_C, state_mlstm_n, state_mlstm_m, state_gla_S,
              cache_mla_ckv, cache_mla_krope, c, c_ctx, w_ada, b_ada, w_in, m_gate_b, m_norm_g,
              g_w2, g_b2, g_norm_g, a_q_norm_g, a_kv_norm_g, a_w_uq, a_w_ukv, w_out, ln1_g, ln1_b,
              w_ffn_gate, w_ffn_up, w_ffn_down, ln2_g, ln2_b):
    def layer_params(l):
        return {'w_ada': w_ada[l], 'b_ada': b_ada[l], 'w_in': w_in[l], 'm_gate_b': m_gate_b[l],
                'm_norm_g': m_norm_g[l], 'g_w2': g_w2[l], 'g_b2': g_b2[l], 'g_norm_g': g_norm_g[l],
                'a_q_norm_g': a_q_norm_g[l], 'a_kv_norm_g': a_kv_norm_g[l], 'a_w_uq': a_w_uq[l],
                'a_w_ukv': a_w_ukv[l], 'w_out': w_out[l], 'ln1_g': ln1_g[l], 'ln1_b': ln1_b[l],
                'w_ffn_gate': w_ffn_gate[l], 'w_ffn_up': w_ffn_up[l], 'w_ffn_down': w_ffn_down[l],
                'ln2_g': ln2_g[l], 'ln2_b': ln2_b[l]}

    bp = x_prompt.shape[0]
    dt = x_prompt.dtype
    zero_init = (jnp.zeros((bp, 2, M_HEADS, M_DK, M_DV), dt), jnp.zeros((bp, 2, M_HEADS, M_DK), dt),
                 jnp.zeros((bp, 2, M_HEADS), dt), jnp.zeros((bp, 2, G_HEADS, G_DK, G_DV), dt))
    cond_ctx = c_ctx[None, :]
    y_prompt = x_prompt
    per_layer = []
    for l in range(DEPTH):
        y_prompt, new = layer(y_prompt, cond_ctx, layer_params(l), zero_init, None, None)
        per_layer.append(new)
    new_mlstm_C = jnp.stack([s[0] for s in per_layer], axis=1)
    new_mlstm_n = jnp.stack([s[1] for s in per_layer], axis=1)
    new_mlstm_m = jnp.stack([s[2] for s in per_layer], axis=1)
    new_gla_S = jnp.stack([s[3] for s in per_layer], axis=1)
    new_mla_ckv = jnp.stack([s[4] for s in per_layer], axis=1)
    new_mla_krope = jnp.stack([s[5] for s in per_layer], axis=1)

    n_lat = x_sample.shape[1]
    grid_rows = n_lat // GRID_W
    rows = jnp.repeat(jnp.arange(grid_rows, dtype=jnp.int32), GRID_W)
    cols = jnp.tile(jnp.arange(GRID_W, dtype=jnp.int32), grid_rows)
    y_sample = x_sample
    for l in range(DEPTH):
        init = (state_mlstm_C[:, l], state_mlstm_n[:, l], state_mlstm_m[:, l], state_gla_S[:, l])
        y_sample, _ = layer(y_sample, c, layer_params(l), init,
                            (cache_mla_ckv[:, l], cache_mla_krope[:, l]), (rows, cols))

    return (y_prompt, y_sample, new_mlstm_C, new_mlstm_n, new_mlstm_m, new_gla_S, new_mla_ckv, new_mla_krope)
```

```python
import functools

import numpy as np
import jax
import jax.numpy as jnp
from jax import lax
from jax.experimental import pallas as pl
from jax.experimental.pallas import tpu as pltpu

F32 = jnp.float32
BF16 = jnp.bfloat16

D_MODEL = 1024
GRID_W = 64
M_HEADS, M_DK, M_DV = 4, 32, 64
G_HEADS, G_DK, G_DV = 4, 32, 64
G_RANK = 16
G_TAU = 16.0
A_HEADS, A_DNOPE, A_DROPE, A_DV = 8, 64, 32, 64
A_DQ, A_DC = 256, 128
A_SCALE = (A_DNOPE + A_DROPE) ** -0.5
ROPE_BASE = 10000.0
IN_SPLITS = (M_HEADS * M_DK, M_HEADS * M_DK, M_HEADS * M_DV, M_HEADS * M_DV, 4 * M_HEADS,
             G_HEADS * G_DK, G_HEADS * G_DK, G_HEADS * G_DV, G_HEADS * G_DV, 2 * G_RANK,
             A_DQ, A_DC, A_DROPE)

LANES = 128
ROW_BLOCK = 256
M_CHUNK = 128
G_CHUNK = 64
G_LEVELS = 6
A_HPAD = 128
VMEM_LIMIT = 56 * 1024 * 1024

ZM_W = 768
ZG_W = 768
ZQ_W = 384
ZS_W = 256
Z_W = ZM_W + ZG_W + ZQ_W + ZS_W
KR_LO, KR_HI = A_DNOPE, A_DNOPE + A_DROPE


def _cparams(sem):
    return pltpu.CompilerParams(dimension_semantics=sem, vmem_limit_bytes=VMEM_LIMIT)


def _dot(a, b):
    return jnp.dot(a.astype(BF16), b.astype(BF16), preferred_element_type=F32)


def _dot_nt(a, b):
    return lax.dot_general(a.astype(BF16), b.astype(BF16), (((1,), (1,)), ((), ())),
                           preferred_element_type=F32)


def _dot_tn(a, b):
    return lax.dot_general(a.astype(BF16), b.astype(BF16), (((0,), (0,)), ((), ())),
                           preferred_element_type=F32)


def _split(x):
    hi = x.astype(BF16)
    lo = (x - hi.astype(F32)).astype(BF16)
    return hi, lo


def _dot_c_x(c, x):
    hi, lo = _split(x)
    return jnp.dot(c, hi, preferred_element_type=F32) + jnp.dot(c, lo, preferred_element_type=F32)


def _dot_x_c(x, c):
    hi, lo = _split(x)
    return jnp.dot(hi, c, preferred_element_type=F32) + jnp.dot(lo, c, preferred_element_type=F32)


def _dot_x_ct(x, c):
    hi, lo = _split(x)
    dn = (((1,), (1,)), ((), ()))
    return (lax.dot_general(hi, c, dn, preferred_element_type=F32)
            + lax.dot_general(lo, c, dn, preferred_element_type=F32))


def _dot_xt_c(x, c):
    hi, lo = _split(x)
    dn = (((0,), (0,)), ((), ()))
    return (lax.dot_general(hi, c, dn, preferred_element_type=F32)
            + lax.dot_general(lo, c, dn, preferred_element_type=F32))


def _log_sigmoid(x):
    return jnp.minimum(x, 0.0) - jnp.log1p(jnp.exp(-jnp.abs(x)))


def _sigmoid(x):
    return 1.0 / (1.0 + jnp.exp(-x))


def _ada_kernel(c_ref, w_ref, b_ref, o_ref):
    c = c_ref[...]
    s = c * _sigmoid(c)
    o_ref[0] = _dot(s, w_ref[0]) + b_ref[0]


def _ada_call(cond8, w_ada, b_ada):
    depth, d, n = w_ada.shape
    tn = 1536
    return pl.pallas_call(
        _ada_kernel,
        grid=(depth, n // tn),
        in_specs=[pl.BlockSpec((8, d), lambda l, j: (0, 0)),
                  pl.BlockSpec((1, d, tn), lambda l, j: (l, 0, j)),
                  pl.BlockSpec((1, 1, tn), lambda l, j: (l, 0, j))],
        out_specs=pl.BlockSpec((1, 8, tn), lambda l, j: (l, 0, j)),
        out_shape=jax.ShapeDtypeStruct((depth, 8, n), F32),
        compiler_params=_cparams(("arbitrary", "arbitrary")),
        name="ada_mod",
    )(cond8, w_ada, b_ada.reshape(depth, 1, n))


def _inproj_kernel(x_ref, mod_ref, w_ref, wgt_ref, zm_ref, zg_ref, zq_ref, zs_ref, zt_ref):
    d = D_MODEL
    mod = mod_ref[0]
    h = (x_ref[...] * (1.0 + mod[:, d:2 * d]) + mod[:, 0:d]).astype(BF16)
    z = jnp.dot(h, w_ref[...], preferred_element_type=F32)
    zm_ref[...] = z[:, 0:ZM_W]
    zg_ref[...] = z[:, ZM_W:ZM_W + ZG_W]
    zq_ref[...] = z[:, ZM_W + ZG_W:ZM_W + ZG_W + ZQ_W]
    zs_ref[...] = z[:, ZM_W + ZG_W + ZQ_W:Z_W]
    zt_ref[...] = lax.dot_general(wgt_ref[...], h, (((1,), (1,)), ((), ())), preferred_element_type=F32)


def _inproj_call(x, mods, w_in_p, w_gt, mod_map):
    nt, d = x.shape
    tm = ROW_BLOCK
    row = lambda i: (i, 0)
    const = lambda i: (0, 0)
    return pl.pallas_call(
        _inproj_kernel,
        grid=(nt // tm,),
        in_specs=[pl.BlockSpec((tm, d), row),
                  pl.BlockSpec((1, 1, 6 * d), lambda i: (mod_map(i), 0, 0)),
                  pl.BlockSpec((d, Z_W), const),
                  pl.BlockSpec((16, d), const)],
        out_specs=[pl.BlockSpec((tm, ZM_W), row), pl.BlockSpec((tm, ZG_W), row),
                   pl.BlockSpec((tm, ZQ_W), row), pl.BlockSpec((tm, ZS_W), row),
                   pl.BlockSpec((16, tm), lambda i: (0, i))],
        out_shape=[jax.ShapeDtypeStruct((nt, ZM_W), F32), jax.ShapeDtypeStruct((nt, ZG_W), F32),
                   jax.ShapeDtypeStruct((nt, ZQ_W), F32), jax.ShapeDtypeStruct((nt, ZS_W), F32),
                   jax.ShapeDtypeStruct((16, nt), F32)],
        compiler_params=_cparams(("parallel",)),
        name="in_proj",
    )(x, mods, w_in_p, w_gt)


def _mlstm_chunk(d, q, k, v, graw, graw_t, tri, c_st, n_st, m_st):
    lm = q.shape[0]
    io, fo = 8 * d, 8 * d + 4
    last = lm - 1 if d == 0 else 0
    lf_c = _log_sigmoid(graw)
    lf_t = _log_sigmoid(graw_t)
    bcol = _dot_c_x(tri, lf_c)
    brow = _dot_x_ct(lf_t, tri)
    ri = lax.broadcasted_iota(jnp.int32, (lm, lm), 0)
    ci = lax.broadcasted_iota(jnp.int32, (lm, lm), 1)
    causal = (ci <= ri) if d == 0 else (ci >= ri)
    head32 = lax.broadcasted_iota(jnp.int32, (lm, LANES), 1) // M_DK
    head64 = lax.broadcasted_iota(jnp.int32, (lm, M_HEADS * M_DV), 1) // M_DV
    rowhead = lax.broadcasted_iota(jnp.int32, (M_HEADS * M_DK, 1), 0) // M_DK
    lane1 = lax.broadcasted_iota(jnp.int32, (1, LANES), 1)

    qb = q.astype(BF16)
    vb = v.astype(BF16)
    q_c = jnp.dot(qb, c_st.astype(BF16), preferred_element_type=F32)
    q_n = jnp.dot(qb, n_st.astype(BF16), preferred_element_type=F32)

    num = jnp.zeros((lm, M_HEADS * M_DV), F32)
    i_exp = jnp.zeros((lm, M_HEADS * M_DV), F32)
    r_exp = jnp.zeros((lm, M_HEADS * M_DV), F32)
    w_exp = jnp.zeros((lm, LANES), F32)
    d_row = jnp.zeros((M_HEADS * M_DK, 1), F32)
    m_row = jnp.zeros((1, LANES), F32)
    for h in range(M_HEADS):
        col_b = bcol[:, fo + h:fo + h + 1]
        row_g = graw_t[io + h:io + h + 1, :] - brow[fo + h:fo + h + 1, :]
        m_h = m_st[:, h:h + 1]
        dmat = jnp.where(causal, col_b + row_g, -jnp.inf)
        a = col_b + m_h
        mloc = jnp.maximum(a, jnp.max(dmat, axis=1, keepdims=True))
        qk = lax.dot_general(qb, jnp.where(head32 == h, k, 0.0).astype(BF16),
                             (((1,), (1,)), ((), ())), preferred_element_type=F32)
        s = qk * jnp.exp(dmat - mloc)
        inter = jnp.exp(a - mloc)
        den = jnp.sum(s, axis=1, keepdims=True) + inter * q_n[:, h:h + 1]
        rden = 1.0 / jnp.maximum(jnp.abs(den), jnp.exp(-mloc))
        num = num + jnp.dot(s.astype(BF16), jnp.where(head64 == h, v, 0.0).astype(BF16),
                            preferred_element_type=F32)
        i_exp = jnp.where(head64 == h, inter, i_exp)
        r_exp = jnp.where(head64 == h, rden, r_exp)
        bl = col_b[last:last + 1, :]
        g = bl - col_b + graw[:, io + h:io + h + 1]
        m_new = jnp.maximum(bl + m_h, jnp.max(g, axis=0, keepdims=True))
        w_exp = jnp.where(head32 == h, jnp.exp(g - m_new), w_exp)
        d_row = jnp.where(rowhead == h, jnp.exp(bl + m_h - m_new), d_row)
        m_row = jnp.where(lane1 == h, m_new, m_row)
    h_out = (num + i_exp * q_c) * r_exp

    kw = (k * w_exp).astype(BF16)
    dn = (((0,), (0,)), ((), ()))
    upd_c = lax.dot_general(kw, vb, dn, preferred_element_type=F32)
    upd_n = lax.dot_general(kw, jnp.ones((lm, LANES), BF16), dn, preferred_element_type=F32)
    rc = lax.broadcasted_iota(jnp.int32, (M_HEADS * M_DK, M_HEADS * M_DV), 0) // M_DK
    cc = lax.broadcasted_iota(jnp.int32, (M_HEADS * M_DK, M_HEADS * M_DV), 1) // M_DV
    rn = lax.broadcasted_iota(jnp.int32, (M_HEADS * M_DK, LANES), 0) // M_DK
    cn = lax.broadcasted_iota(jnp.int32, (M_HEADS * M_DK, LANES), 1)
    c_new = d_row * c_st + jnp.where(rc == cc, upd_c, 0.0)
    n_new = d_row * n_st + jnp.where(rn == cn, upd_n, 0.0)
    return h_out, c_new, n_new, m_row


def _mlstm_kernel(first_ref, seq_ref, bwd_ref,
                  zmf_ref, zsf_ref, ztf_ref, zmb_ref, zsb_ref, ztb_ref,
                  c0_ref, n0_ref, m0_ref, gb_ref, gbt_ref, trif_ref, trib_ref,
                  hf_ref, hb_ref, cout_ref, nout_ref, mout_ref,
                  cs_ref, ns_ref, ms_ref):
    r = pl.program_id(0)

    @pl.when(first_ref[r] == 1)
    def _():
        cs_ref[...] = c0_ref[0]
        ns_ref[...] = n0_ref[0]
        ms_ref[...] = m0_ref[0]

    nch = ROW_BLOCK // M_CHUNK
    qk_w = M_HEADS * M_DK
    for d in (0, 1):
        zm_ref, zs_ref, zt_ref, h_ref = ((zmf_ref, zsf_ref, ztf_ref, hf_ref),
                                         (zmb_ref, zsb_ref, ztb_ref, hb_ref))[d]
        tri = (trif_ref, trib_ref)[d][...]
        order = range(nch) if d == 0 else range(nch - 1, -1, -1)
        for c in order:
            rows = slice(c * M_CHUNK, (c + 1) * M_CHUNK)
            q = zm_ref[rows, 0:qk_w]
            k = zm_ref[rows, qk_w:2 * qk_w] * (M_DK ** -0.5)
            v = zm_ref[rows, 2 * qk_w:2 * qk_w + M_HEADS * M_DV]
            graw = zs_ref[rows, :] + gb_ref[...]
            graw_t = zt_ref[:, rows] + gbt_ref[...]
            h_out, c_new, n_new, m_new = _mlstm_chunk(d, q, k, v, graw, graw_t, tri,
                                                      cs_ref[d], ns_ref[d], ms_ref[d])
            h_ref[rows, :] = h_out
            cs_ref[d] = c_new
            ns_ref[d] = n_new
            ms_ref[d] = m_new
    cout_ref[0] = cs_ref[...]
    nout_ref[0] = ns_ref[...]
    mout_ref[0] = ms_ref[...]


def _mlstm_call(plan, zm, zs, zt, c0, n0, m0, gb, gbt, trif, trib):
    nt = zm.shape[0]
    nb = nt // ROW_BLOCK
    nseq = c0.shape[0]
    hw = M_HEADS * M_DV
    kd = M_HEADS * M_DK
    fwd = lambda r, first, seq, bwd: (r, 0)
    bwd_ = lambda r, first, seq, bwd: (bwd[r], 0)
    fwd_t = lambda r, first, seq, bwd: (0, r)
    bwd_t = lambda r, first, seq, bwd: (0, bwd[r])
    seq4 = lambda r, first, seq, bwd: (seq[r], 0, 0, 0)
    const = lambda r, first, seq, bwd: (0, 0)
    grid_spec = pltpu.PrefetchScalarGridSpec(
        num_scalar_prefetch=3,
        grid=(nb,),
        in_specs=[pl.BlockSpec((ROW_BLOCK, 512), fwd), pl.BlockSpec((ROW_BLOCK, LANES), fwd),
                  pl.BlockSpec((16, ROW_BLOCK), fwd_t),
                  pl.BlockSpec((ROW_BLOCK, 512), bwd_), pl.BlockSpec((ROW_BLOCK, LANES), bwd_),
                  pl.BlockSpec((16, ROW_BLOCK), bwd_t),
                  pl.BlockSpec((1, 2, kd, hw), seq4), pl.BlockSpec((1, 2, kd, LANES), seq4),
                  pl.BlockSpec((1, 2, 1, LANES), seq4),
                  pl.BlockSpec((1, LANES), const), pl.BlockSpec((16, 1), const),
                  pl.BlockSpec((M_CHUNK, M_CHUNK), const), pl.BlockSpec((M_CHUNK, M_CHUNK), const)],
        out_specs=[pl.BlockSpec((ROW_BLOCK, hw), fwd), pl.BlockSpec((ROW_BLOCK, hw), bwd_),
                   pl.BlockSpec((1, 2, kd, hw), seq4), pl.BlockSpec((1, 2, kd, LANES), seq4),
                   pl.BlockSpec((1, 2, 1, LANES), seq4)],
        scratch_shapes=[pltpu.VMEM((2, kd, hw), F32), pltpu.VMEM((2, kd, LANES), F32),
                        pltpu.VMEM((2, 1, LANES), F32)],
    )
    return pl.pallas_call(
        _mlstm_kernel,
        grid_spec=grid_spec,
        out_shape=[jax.ShapeDtypeStruct((nt, hw), F32), jax.ShapeDtypeStruct((nt, hw), F32),
                   jax.ShapeDtypeStruct((nseq, 2, kd, hw), F32),
                   jax.ShapeDtypeStruct((nseq, 2, kd, LANES), F32),
                   jax.ShapeDtypeStruct((nseq, 2, 1, LANES), F32)],
        compiler_params=_cparams(("arbitrary",)),
        name="mlstm_scan",
    )(plan["first"], plan["seq"], plan["bwd"], zm, zs, zt, zm, zs, zt, c0, n0, m0, gb, gbt, trif, trib)


def _gla_consts():
    n = G_CHUNK
    t = np.arange(n)[:, None]
    u = np.arange(n)[None, :]
    a_blocks, b_blocks, masks = [], [], [np.eye(n)]
    for lev in range(G_LEVELS):
        b = 1 << lev
        mid = (t // (2 * b)) * (2 * b) + b
        right = (t % (2 * b)) >= b
        a_blocks.append((right & (u >= mid) & (u <= t)).astype(np.float32))
        b_blocks.append((~right & (u > t) & (u <= mid - 1)).astype(np.float32))
        same_parent = (t // (2 * b)) == (u // (2 * b))
        masks.append((same_parent & right & ((u % (2 * b)) < b)).astype(np.float32))
    incl = (u <= t).astype(np.float32)
    strict_suffix = (u > t).astype(np.float32)
    acat_f = np.concatenate(a_blocks + b_blocks + [incl, strict_suffix], axis=0)
    mcat_f = np.stack([np.tile(m, (1, G_HEADS)) for m in masks])
    flip = lambda m: m[::-1, ::-1]
    acat_b = np.concatenate([flip(m) for m in a_blocks + b_blocks + [incl, strict_suffix]], axis=0)
    mcat_b = np.stack([np.tile(flip(m), (1, G_HEADS)) for m in masks])
    return (np.stack([acat_f, acat_b]), np.stack([mcat_f, mcat_b]))


def _gla_kernel(first_ref, seq_ref, bwd_ref,
                zgf_ref, zsf_ref, zgb_ref, zsb_ref, s0_ref, w2_ref, b2_ref, acat_ref, mcat_ref,
                of_ref, ob_ref, sout_ref, ss_ref):
    r = pl.program_id(0)

    @pl.when(first_ref[r] == 1)
    def _():
        ss_ref[...] = s0_ref[0]

    n = G_CHUNK
    nch = ROW_BLOCK // n
    kd = G_HEADS * G_DK
    vd = G_HEADS * G_DV
    head32 = lax.broadcasted_iota(jnp.int32, (n, kd), 1) // G_DK
    head64 = lax.broadcasted_iota(jnp.int32, (n, vd), 1) // G_DV
    rc = lax.broadcasted_iota(jnp.int32, (kd, vd), 0) // G_DK
    cc = lax.broadcasted_iota(jnp.int32, (kd, vd), 1) // G_DV
    ones_v = jnp.ones((n, vd), BF16)

    for d in (0, 1):
        zg_ref, zs_ref, o_ref = ((zgf_ref, zsf_ref, of_ref), (zgb_ref, zsb_ref, ob_ref))[d]

        def body(ci, carry, d=d, zg_ref=zg_ref, zs_ref=zs_ref, o_ref=o_ref):
            c = ci if d == 0 else nch - 1 - ci
            r0 = pl.multiple_of(c * n, n)
            q = zg_ref[pl.ds(r0, n), 0:kd] * (G_DK ** -0.5)
            k = zg_ref[pl.ds(r0, n), kd:2 * kd]
            v = zg_ref[pl.ds(r0, n), 2 * kd:2 * kd + vd]
            x = jnp.dot(zs_ref[pl.ds(r0, n), :].astype(BF16), w2_ref[d], preferred_element_type=F32) + b2_ref[d]
            lg = _log_sigmoid(x) * (1.0 / G_TAU)
            e = jnp.exp(_dot_c_x(acat_ref[d], lg))
            s_st = ss_ref[d]

            def bd_k(kt):
                return jnp.concatenate([jnp.where(head32 == h, kt, 0.0).astype(BF16)
                                        for h in range(G_HEADS)], axis=0)

            att = mcat_ref[d, 0] * _dot_nt(q, bd_k(k))
            for lev in range(G_LEVELS):
                qt = q * e[lev * n:(lev + 1) * n]
                kt = k * e[(G_LEVELS + lev) * n:(G_LEVELS + lev + 1) * n]
                att = att + mcat_ref[d, lev + 1] * _dot_nt(qt, bd_k(kt))
            v_bd = jnp.concatenate([jnp.where(head64 == h, v, 0.0).astype(BF16)
                                    for h in range(G_HEADS)], axis=0)
            intra = jnp.dot(att.astype(BF16), v_bd, preferred_element_type=F32)
            inter = _dot(q * e[2 * G_LEVELS * n:(2 * G_LEVELS + 1) * n], s_st)
            o_ref[pl.ds(r0, n), :] = inter + intra
            decay = jnp.exp(_dot_xt_c(lg, ones_v))
            upd = _dot_tn(k * e[(2 * G_LEVELS + 1) * n:(2 * G_LEVELS + 2) * n], v)
            ss_ref[d] = decay * s_st + jnp.where(rc == cc, upd, 0.0)
            return carry

        lax.fori_loop(0, nch, body, 0)
    sout_ref[0] = ss_ref[...]


def _gla_call(plan, zg, zs, s0, w2p, b2p, acat, mcat):
    nt = zg.shape[0]
    nb = nt // ROW_BLOCK
    nseq = s0.shape[0]
    kd = G_HEADS * G_DK
    vd = G_HEADS * G_DV
    n = G_CHUNK
    fwd = lambda r, first, seq, bwd: (r, 0)
    bwd_ = lambda r, first, seq, bwd: (bwd[r], 0)
    seq4 = lambda r, first, seq, bwd: (seq[r], 0, 0, 0)
    const3 = lambda r, first, seq, bwd: (0, 0, 0)
    const4 = lambda r, first, seq, bwd: (0, 0, 0, 0)
    grid_spec = pltpu.PrefetchScalarGridSpec(
        num_scalar_prefetch=3,
        grid=(nb,),
        in_specs=[pl.BlockSpec((ROW_BLOCK, 512), fwd), pl.BlockSpec((ROW_BLOCK, LANES), fwd),
                  pl.BlockSpec((ROW_BLOCK, 512), bwd_), pl.BlockSpec((ROW_BLOCK, LANES), bwd_),
                  pl.BlockSpec((1, 2, kd, vd), seq4),
                  pl.BlockSpec((2, LANES, kd), const3), pl.BlockSpec((2, 1, kd), const3),
                  pl.BlockSpec((2, (2 * G_LEVELS + 2) * n, n), const3),
                  pl.BlockSpec((2, G_LEVELS + 1, n, G_HEADS * n), const4)],
        out_specs=[pl.BlockSpec((ROW_BLOCK, vd), fwd), pl.BlockSpec((ROW_BLOCK, vd), bwd_),
                   pl.BlockSpec((1, 2, kd, vd), seq4)],
        scratch_shapes=[pltpu.VMEM((2, kd, vd), F32)],
    )
    return pl.pallas_call(
        _gla_kernel,
        grid_spec=grid_spec,
        out_shape=[jax.ShapeDtypeStruct((nt, vd), F32), jax.ShapeDtypeStruct((nt, vd), F32),
                   jax.ShapeDtypeStruct((nseq, 2, kd, vd), F32)],
        compiler_params=_cparams(("arbitrary",)),
        name="gla_scan",
    )(plan["first"], plan["seq"], plan["bwd"], zg, zs, zg, zs, s0, w2p, b2p, acat, mcat)


def _rms(x, g, eps=1e-6):
    return x * lax.rsqrt(jnp.mean(x * x, axis=-1, keepdims=True) + eps) * g


def _mla_prep_kernel(zq_ref, zs_ref, cos_ref, sin_ref, kcos_ref, gq_ref, gkv_ref,
                     wuq_ref, wuqs_ref, wk_ref, wv_ref, q_ref, k_ref, v_ref, ckv_ref):
    cq = _rms(zq_ref[:, 0:A_DQ], gq_ref[...]).astype(BF16)
    qn = jnp.dot(cq, wuq_ref[...], preferred_element_type=F32)
    qs = jnp.dot(cq, wuqs_ref[...], preferred_element_type=F32)
    cos = cos_ref[...]
    sin = sin_ref[...]
    for h in range(A_HEADS):
        sl = slice(h * A_HPAD, (h + 1) * A_HPAD)
        q_ref[:, sl] = ((qn[:, sl] * cos + qs[:, sl] * sin) * A_SCALE).astype(BF16)
    ckv = _rms(zq_ref[:, A_DQ:A_DQ + A_DC], gkv_ref[...])
    ckv_ref[...] = ckv
    ckvb = ckv.astype(BF16)
    kn = jnp.dot(ckvb, wk_ref[...], preferred_element_type=F32)
    v_ref[...] = jnp.dot(ckvb, wv_ref[...], preferred_element_type=F32).astype(BF16)
    kr = zs_ref[:, 0:LANES] * kcos_ref[...] + zs_ref[:, LANES:2 * LANES] * sin
    for h in range(A_HEADS):
        sl = slice(h * A_HPAD, (h + 1) * A_HPAD)
        k_ref[:, sl] = (kn[:, sl] + kr).astype(BF16)


def _mla_prep_call(zq, zs, cos, sin, kcos, gq, gkv, wuq, wuqs, wk, wv):
    nt = zq.shape[0]
    tm = ROW_BLOCK
    hw = A_HEADS * A_HPAD
    row = lambda i: (i, 0)
    const = lambda i: (0, 0)
    return pl.pallas_call(
        _mla_prep_kernel,
        grid=(nt // tm,),
        in_specs=[pl.BlockSpec((tm, ZQ_W), row), pl.BlockSpec((tm, ZS_W), row),
                  pl.BlockSpec((tm, LANES), row), pl.BlockSpec((tm, LANES), row), pl.BlockSpec((tm, LANES), row),
                  pl.BlockSpec((1, A_DQ), const), pl.BlockSpec((1, A_DC), const),
                  pl.BlockSpec((A_DQ, hw), const), pl.BlockSpec((A_DQ, hw), const),
                  pl.BlockSpec((A_DC, hw), const), pl.BlockSpec((A_DC, hw), const)],
        out_specs=[pl.BlockSpec((tm, hw), row), pl.BlockSpec((tm, hw), row), pl.BlockSpec((tm, hw), row),
                   pl.BlockSpec((tm, A_DC), row)],
        out_shape=[jax.ShapeDtypeStruct((nt, hw), BF16), jax.ShapeDtypeStruct((nt, hw), BF16),
                   jax.ShapeDtypeStruct((nt, hw), BF16), jax.ShapeDtypeStruct((nt, A_DC), F32)],
        compiler_params=_cparams(("parallel",)),
        name="mla_prep",
    )(zq, zs, cos, sin, kcos, gq, gkv, wuq, wuqs, wk, wv)


def _cache_kv_kernel(ckv_ref, kr_ref, wk_ref, wv_ref, k_ref, v_ref):
    ckvb = ckv_ref[...].astype(BF16)
    kn = jnp.dot(ckvb, wk_ref[...], preferred_element_type=F32)
    v_ref[...] = jnp.dot(ckvb, wv_ref[...], preferred_element_type=F32).astype(BF16)
    kr = kr_ref[...]
    for h in range(A_HEADS):
        sl = slice(h * A_HPAD, (h + 1) * A_HPAD)
        k_ref[:, sl] = (kn[:, sl] + kr).astype(BF16)


def _cache_kv_call(ckv, krp, wk, wv):
    nt = ckv.shape[0]
    tm = ROW_BLOCK
    hw = A_HEADS * A_HPAD
    row = lambda i: (i, 0)
    const = lambda i: (0, 0)
    return pl.pallas_call(
        _cache_kv_kernel,
        grid=(nt // tm,),
        in_specs=[pl.BlockSpec((tm, A_DC), row), pl.BlockSpec((tm, LANES), row),
                  pl.BlockSpec((A_DC, hw), const), pl.BlockSpec((A_DC, hw), const)],
        out_specs=[pl.BlockSpec((tm, hw), row), pl.BlockSpec((tm, hw), row)],
        out_shape=[jax.ShapeDtypeStruct((nt, hw), BF16), jax.ShapeDtypeStruct((nt, hw), BF16)],
        compiler_params=_cparams(("parallel",)),
        name="mla_cache_kv",
    )(ckv, krp, wk, wv)


def _attn_kernel(*refs, n_seg):
    q_ref = refs[0]
    kv_refs = refs[1:1 + 2 * n_seg]
    o_ref = refs[1 + 2 * n_seg]
    for h in range(A_HEADS):
        sl = slice(h * A_HPAD, (h + 1) * A_HPAD)
        q = q_ref[:, sl]
        dn = (((1,), (1,)), ((), ()))
        s = [lax.dot_general(q, kv_refs[2 * j][:, sl], dn, preferred_element_type=F32) for j in range(n_seg)]
        m = s[0].max(axis=1, keepdims=True)
        for sj in s[1:]:
            m = jnp.maximum(m, sj.max(axis=1, keepdims=True))
        p = [jnp.exp(sj - m) for sj in s]
        l = p[0].sum(axis=1, keepdims=True)
        for pj in p[1:]:
            l = l + pj.sum(axis=1, keepdims=True)
        o = jnp.dot(p[0].astype(BF16), kv_refs[1][:, sl], preferred_element_type=F32)
        for j in range(1, n_seg):
            o = o + jnp.dot(p[j].astype(BF16), kv_refs[2 * j + 1][:, sl], preferred_element_type=F32)
        o_ref[:, sl] = (o / l).astype(BF16)


def _attn_ctx_call(q, k, v, n_rows):
    hw = A_HEADS * A_HPAD
    tm = ROW_BLOCK
    row = lambda i: (i, 0)
    return pl.pallas_call(
        functools.partial(_attn_kernel, n_seg=1),
        grid=(n_rows // tm,),
        in_specs=[pl.BlockSpec((tm, hw), row)] * 3,
        out_specs=pl.BlockSpec((tm, hw), row),
        out_shape=jax.ShapeDtypeStruct((n_rows, hw), BF16),
        compiler_params=_cparams(("parallel",)),
        name="mla_attn_ctx",
    )(q, k, v)


def _attn_dec_call(q, k, v, kc, vc, n_ctx_rows, n_dec, dec_len, past_len):
    hw = A_HEADS * A_HPAD
    tm = ROW_BLOCK
    nq = dec_len // tm
    assert n_ctx_rows % dec_len == 0
    qmap = lambda b, i: (n_ctx_rows // tm + b * nq + i, 0)
    own = lambda b, i: (n_ctx_rows // dec_len + b, 0)
    return pl.pallas_call(
        functools.partial(_attn_kernel, n_seg=2),
        grid=(n_dec, nq),
        in_specs=[pl.BlockSpec((tm, hw), qmap),
                  pl.BlockSpec((past_len, hw), lambda b, i: (b, 0)),
                  pl.BlockSpec((past_len, hw), lambda b, i: (b, 0)),
                  pl.BlockSpec((dec_len, hw), own),
                  pl.BlockSpec((dec_len, hw), own)],
        out_specs=pl.BlockSpec((tm, hw), lambda b, i: (b * nq + i, 0)),
        out_shape=jax.ShapeDtypeStruct((n_dec * dec_len, hw), BF16),
        compiler_params=_cparams(("parallel", "arbitrary")),
        name="mla_attn_dec",
    )(q, kc, vc, k, v)


def _layer_norm(y, g, b, eps=1e-5):
    yc = y - jnp.mean(y, axis=-1, keepdims=True)
    return yc * lax.rsqrt(jnp.mean(yc * yc, axis=-1, keepdims=True) + eps) * g + b


def _outproj_kernel(hf_ref, hb_ref, of_ref, ob_ref, a_ref, mo_ref, gg_ref, x_ref, mod_ref,
                    mng_ref, gng_ref, avg_ref, wm_ref, wg_ref, wa_ref, lg_ref, lb_ref, o_ref, *, alpha):
    d = D_MODEL
    avg = avg_ref[...]
    hm = hf_ref[...] + hb_ref[...]
    hc = hm - _dot_x_c(hm, avg)
    m_out = hc * lax.rsqrt(_dot_x_c(hc * hc, avg) + 1e-6) * mng_ref[...] * _sigmoid(mo_ref[...])
    og = of_ref[...] + ob_ref[...]
    gg = gg_ref[...]
    g_out = og * lax.rsqrt(_dot_x_c(og * og, avg) + 1e-6) * gng_ref[...] * (gg * _sigmoid(gg))
    mix = (_dot(m_out, wm_ref[...]) + _dot(g_out, wg_ref[...])
           + jnp.dot(a_ref[...], wa_ref[...], preferred_element_type=F32))
    mod = mod_ref[0]
    y = alpha * x_ref[...] + mod[:, 2 * d:3 * d] * mix
    o_ref[...] = _layer_norm(y, lg_ref[...], lb_ref[...])


def _outproj_call(hf, hb, of, ob, a, zm, zg, x, mods, mng, gng, avg, wm, wg, wa, lg, lb, mod_map, alpha):
    nt, d = x.shape
    tm = ROW_BLOCK
    hw = M_HEADS * M_DV
    row = lambda i: (i, 0)
    const = lambda i: (0, 0)
    gate_col = lambda i: (i, 2)
    return pl.pallas_call(
        functools.partial(_outproj_kernel, alpha=alpha),
        grid=(nt // tm,),
        in_specs=[pl.BlockSpec((tm, hw), row)] * 4
                 + [pl.BlockSpec((tm, A_HEADS * A_HPAD), row),
                    pl.BlockSpec((tm, hw), gate_col), pl.BlockSpec((tm, hw), gate_col),
                    pl.BlockSpec((tm, d), row),
                    pl.BlockSpec((1, 1, 6 * d), lambda i: (mod_map(i), 0, 0)),
                    pl.BlockSpec((1, hw), const), pl.BlockSpec((1, hw), const),
                    pl.BlockSpec((hw, hw), const),
                    pl.BlockSpec((hw, d), const), pl.BlockSpec((hw, d), const),
                    pl.BlockSpec((A_HEADS * A_HPAD, d), const),
                    pl.BlockSpec((1, d), const), pl.BlockSpec((1, d), const)],
        out_specs=pl.BlockSpec((tm, d), row),
        out_shape=jax.ShapeDtypeStruct((nt, d), F32),
        compiler_params=_cparams(("parallel",)),
        name="out_proj_ln",
    )(hf, hb, of, ob, a, zm, zg, x, mods, mng, gng, avg, wm, wg, wa, lg, lb)


def _ffn_kernel(x_ref, mod_ref, wg_ref, wu_ref, wd_ref, lg_ref, lb_ref, o_ref, *, n_chunks, alpha):
    d = D_MODEL
    mod = mod_ref[0]
    x = x_ref[...]
    hh = (x * (1.0 + mod[:, 4 * d:5 * d]) + mod[:, 3 * d:4 * d]).astype(BF16)
    f = wg_ref.shape[1]
    cw = f // n_chunks
    acc = jnp.zeros(x.shape, F32)
    for c in range(n_chunks):
        sl = slice(c * cw, (c + 1) * cw)
        g = jnp.dot(hh, wg_ref[:, sl], preferred_element_type=F32)
        u = jnp.dot(hh, wu_ref[:, sl], preferred_element_type=F32)
        act = (g * _sigmoid(g) * u).astype(BF16)
        acc = acc + jnp.dot(act, wd_ref[sl, :], preferred_element_type=F32)
    y = alpha * x + mod[:, 5 * d:6 * d] * acc
    o_ref[...] = _layer_norm(y, lg_ref[...], lb_ref[...])


def _ffn_call(x, mods, wg, wu, wd, lg, lb, mod_map, alpha):
    nt, d = x.shape
    f = wg.shape[1]
    tm = ROW_BLOCK
    row = lambda i: (i, 0)
    const = lambda i: (0, 0)
    resident = dict(pipeline_mode=pl.Buffered(1))
    return pl.pallas_call(
        functools.partial(_ffn_kernel, n_chunks=2, alpha=alpha),
        grid=(nt // tm,),
        in_specs=[pl.BlockSpec((tm, d), row),
                  pl.BlockSpec((1, 1, 6 * d), lambda i: (mod_map(i), 0, 0)),
                  pl.BlockSpec((d, f), const, **resident), pl.BlockSpec((d, f), const, **resident),
                  pl.BlockSpec((f, d), const, **resident),
                  pl.BlockSpec((1, d), const), pl.BlockSpec((1, d), const)],
        out_specs=pl.BlockSpec((tm, d), row),
        out_shape=jax.ShapeDtypeStruct((nt, d), F32),
        compiler_params=_cparams(("parallel",)),
        name="ffn_ln",
    )(x, mods, wg, wu, wd, lg, lb)


def _block_diag_heads(t):
    h = t.shape[-3]
    eye = jnp.eye(h, dtype=t.dtype)
    out = jnp.einsum('...hkv,hg->...hkgv', t, eye)
    return out.reshape(t.shape[:-3] + (h * t.shape[-2], h * t.shape[-1]))


def _diag_blocks(t, h, dk, dv):
    t = t.reshape(t.shape[:-2] + (h, dk, h, dv))
    return jnp.stack([t[..., i, :, i, :] for i in range(h)], axis=-3)


def _rope_tables(n_ctx_rows, n_dec, dec_len):
    half = A_DROPE // 4
    inv = ROPE_BASE ** (-jnp.arange(half, dtype=F32) / half)
    pos = jnp.arange(dec_len, dtype=jnp.int32)
    rows = (pos // GRID_W).astype(F32)[:, None] * inv[None, :]
    cols = (pos % GRID_W).astype(F32)[:, None] * inv[None, :]
    cr, sr, cc, sc = jnp.cos(rows), jnp.sin(rows), jnp.cos(cols), jnp.sin(cols)
    cos32 = jnp.concatenate([cr, cr, cc, cc], axis=1)
    sin32 = jnp.concatenate([-sr, sr, -sc, sc], axis=1)
    ones_lo = jnp.ones((dec_len, KR_LO), F32)
    ones_hi = jnp.ones((dec_len, A_HPAD - KR_HI), F32)
    cos = jnp.concatenate([ones_lo, cos32, ones_hi], axis=1)
    sin = jnp.concatenate([0 * ones_lo, sin32, 0 * ones_hi], axis=1)
    kcos = jnp.concatenate([0 * ones_lo, cos32, 0 * ones_hi], axis=1)
    lane = jnp.arange(A_HPAD)
    rope_lane = ((lane >= KR_LO) & (lane < KR_HI)).astype(F32)[None, :]
    ctx_cos = jnp.ones((n_ctx_rows, A_HPAD), F32)
    ctx_sin = jnp.zeros((n_ctx_rows, A_HPAD), F32)
    ctx_kcos = jnp.broadcast_to(rope_lane, (n_ctx_rows, A_HPAD))
    tile = lambda t: jnp.tile(t, (n_dec, 1))
    return (jnp.concatenate([ctx_cos, tile(cos)]), jnp.concatenate([ctx_sin, tile(sin)]),
            jnp.concatenate([ctx_kcos, tile(kcos)]))


def _rope_partner(w):
    e = A_DROPE // 4
    return jnp.concatenate([w[..., e:2 * e], w[..., 0:e], w[..., 3 * e:4 * e], w[..., 2 * e:3 * e]], axis=-1)


def kernel(x_prompt, x_sample, state_mlstm_C, state_mlstm_n, state_mlstm_m, state_gla_S, cache_mla_ckv, cache_mla_krope, c, c_ctx, w_ada, b_ada, w_in, m_gate_b, m_norm_g, g_w2, g_b2, g_norm_g, a_q_norm_g, a_kv_norm_g, a_w_uq, a_w_ukv, w_out, ln1_g, ln1_b, w_ffn_gate, w_ffn_up, w_ffn_down, ln2_g, ln2_b):
    n_ctx, ctx_len, d = x_prompt.shape
    n_dec, dec_len, _ = x_sample.shape
    depth = w_in.shape[0]
    past_len = cache_mla_ckv.shape[2]
    assert d == D_MODEL and ctx_len == ROW_BLOCK and dec_len % ROW_BLOCK == 0 and past_len % 8 == 0
    assert n_dec + 1 <= 8
    n_ctx_rows = n_ctx * ctx_len
    n_dec_rows = n_dec * dec_len
    nt = n_ctx_rows + n_dec_rows
    nb_ctx = n_ctx_rows // ROW_BLOCK
    nb_seq = dec_len // ROW_BLOCK
    nseq = n_ctx + n_dec
    alpha = (2.0 * depth) ** 0.25

    blocks = np.arange(nt // ROW_BLOCK)
    dec_b = np.maximum(blocks - nb_ctx, 0)
    is_ctx = blocks < nb_ctx
    plan = {
        "first": jnp.asarray(np.where(is_ctx, 1, (dec_b % nb_seq) == 0).astype(np.int32)),
        "seq": jnp.asarray(np.where(is_ctx, blocks, n_ctx + dec_b // nb_seq).astype(np.int32)),
        "bwd": jnp.asarray(np.where(is_ctx, blocks,
                                    nb_ctx + (dec_b // nb_seq) * nb_seq + nb_seq - 1 - dec_b % nb_seq).astype(np.int32)),
    }

    def mod_map_for(layer):
        def mod_map(i):
            return layer * 8 + jnp.where(i < nb_ctx, 0, 1 + (i - nb_ctx) // nb_seq)
        return mod_map

    cond8 = jnp.zeros((8, d), F32).at[0].set(c_ctx).at[1:1 + n_dec].set(c)
    mods = _ada_call(cond8, w_ada, b_ada).reshape(depth * 8, 1, 6 * d)

    pts = [int(p) for p in np.cumsum(IN_SPLITS)[:-1]]
    (wmq, wmk, wmv, wmo, wmg, wgq, wgk, wgv, wgg, wga, wacq, wackv, wakr) = jnp.split(w_in, pts, axis=-1)
    zpad = lambda n: jnp.zeros((depth, d, n), F32)
    s1 = jnp.concatenate([wmg, wga, zpad(KR_LO - 48), wakr, zpad(LANES - KR_HI)], axis=-1)
    s2 = jnp.concatenate([zpad(KR_LO), _rope_partner(wakr), zpad(LANES - KR_HI)], axis=-1)
    w_in_p = jnp.concatenate([wmq, wmk, wmv, wmo, wgq, wgk, wgv, wgg, wacq, wackv, s1, s2], axis=-1).astype(BF16)
    w_gt = jnp.swapaxes(wmg, 1, 2).astype(BF16)
    gate_b = jnp.zeros((depth, 1, LANES), F32).at[:, 0, 0:16].set(m_gate_b)
    gate_bt = m_gate_b.reshape(depth, 16, 1)
    w2p = jnp.zeros((depth, 2, LANES, G_HEADS * G_DK), F32)
    w2p = w2p.at[:, 0, 16:16 + G_RANK].set(g_w2[:, 0]).at[:, 1, 16 + G_RANK:16 + 2 * G_RANK].set(g_w2[:, 1]).astype(BF16)
    b2p = g_b2.reshape(depth, 2, 1, G_HEADS * G_DK)
    uq = a_w_uq.reshape(depth, A_DQ, A_HEADS, A_DNOPE + A_DROPE)
    hp = lambda n: jnp.zeros((depth, uq.shape[1], A_HEADS, n), F32)
    wuq = jnp.concatenate([uq, hp(A_HPAD - KR_HI)], axis=-1).reshape(depth, A_DQ, -1).astype(BF16)
    wuqs = jnp.concatenate([hp(KR_LO), _rope_partner(uq[..., A_DNOPE:]), hp(A_HPAD - KR_HI)],
                           axis=-1).reshape(depth, A_DQ, -1).astype(BF16)
    ukv = a_w_ukv.reshape(depth, A_DC, A_HEADS, A_DNOPE + A_DV)
    kp = jnp.zeros((depth, A_DC, A_HEADS, A_HPAD - A_DNOPE), F32)
    wk = jnp.concatenate([ukv[..., :A_DNOPE], kp], axis=-1).reshape(depth, A_DC, -1).astype(BF16)
    wv = jnp.concatenate([ukv[..., A_DNOPE:], kp], axis=-1).reshape(depth, A_DC, -1).astype(BF16)
    mw, gw = M_HEADS * M_DV, G_HEADS * G_DV
    wo_m = w_out[:, 0:mw].astype(BF16)
    wo_g = w_out[:, mw:mw + gw].astype(BF16)
    wo_a = w_out[:, mw + gw:].reshape(depth, A_HEADS, A_DV, d)
    wo_a = jnp.concatenate([wo_a, jnp.zeros((depth, A_HEADS, A_HPAD - A_DV, d), F32)], axis=2)
    wo_a = wo_a.reshape(depth, A_HEADS * A_HPAD, d).astype(BF16)
    wfg, wfu, wfd = w_ffn_gate.astype(BF16), w_ffn_up.astype(BF16), w_ffn_down.astype(BF16)

    ti = np.arange(M_CHUNK)
    trif = jnp.asarray((ti[None, :] <= ti[:, None]).astype(np.float32), BF16)
    trib = jnp.asarray((ti[None, :] >= ti[:, None]).astype(np.float32), BF16)
    acat_np, mcat_np = _gla_consts()
    acat = jnp.asarray(acat_np, BF16)
    mcat = jnp.asarray(mcat_np, F32)
    hi = np.arange(mw) // M_DV
    avg = jnp.asarray((hi[:, None] == hi[None, :]).astype(np.float32) / M_DV, BF16)
    cos, sin, kcos = _rope_tables(n_ctx_rows, n_dec, dec_len)

    def init_state(block, n_tail):
        z = jnp.zeros((n_ctx,) + block.shape[1:], F32)
        return jnp.concatenate([z, block], axis=0)

    krp_all = jnp.zeros(cache_mla_krope.shape[:-1] + (LANES,), F32).at[..., KR_LO:KR_HI].set(cache_mla_krope)

    x = jnp.concatenate([x_prompt.reshape(n_ctx_rows, d), x_sample.reshape(n_dec_rows, d)], axis=0)
    new_c, new_n, new_m, new_s, new_ckv, new_kr = [], [], [], [], [], []
    for l in range(depth):
        mod_map = mod_map_for(l)
        zm, zg, zq, zs, zt = _inproj_call(x, mods, w_in_p[l], w_gt[l], mod_map)

        c0 = init_state(_block_diag_heads(state_mlstm_C[:, l]), 0)
        n_bd = jnp.einsum('bdhk,hg->bdhkg', state_mlstm_n[:, l], jnp.eye(M_HEADS, LANES, dtype=F32))
        n0 = init_state(n_bd.reshape(n_dec, 2, M_HEADS * M_DK, LANES), 0)
        m0 = init_state(jnp.zeros((n_dec, 2, 1, LANES), F32).at[:, :, 0, 0:M_HEADS].set(state_mlstm_m[:, l]), 0)
        hf, hb, c_fin, n_fin, m_fin = _mlstm_call(plan, zm, zs, zt, c0, n0, m0, gate_b[l], gate_bt[l], trif, trib)

        s0 = init_state(_block_diag_heads(state_gla_S[:, l]), 0)
        of, ob, s_fin = _gla_call(plan, zg, zs, s0, w2p[l], b2p[l], acat, mcat)

        q, k, v, ckv = _mla_prep_call(zq, zs, cos, sin, kcos, a_q_norm_g[l][None, :], a_kv_norm_g[l][None, :],
                                      wuq[l], wuqs[l], wk[l], wv[l])
        kc, vc = _cache_kv_call(cache_mla_ckv[:, l].reshape(n_dec * past_len, A_DC),
                                krp_all[:, l].reshape(n_dec * past_len, LANES), wk[l], wv[l])
        a_ctx = _attn_ctx_call(q, k, v, n_ctx_rows)
        a_dec = _attn_dec_call(q, k, v, kc, vc, n_ctx_rows, n_dec, dec_len, past_len)
        a = jnp.concatenate([a_ctx, a_dec], axis=0)

        x = _outproj_call(hf, hb, of, ob, a, zm, zg, x, mods, m_norm_g[l][None, :], g_norm_g[l][None, :], avg,
                          wo_m[l], wo_g[l], wo_a[l], ln1_g[l][None, :], ln1_b[l][None, :], mod_map, alpha)
        x = _ffn_call(x, mods, wfg[l], wfu[l], wfd[l], ln2_g[l][None, :], ln2_b[l][None, :], mod_map, alpha)

        new_c.append(_diag_blocks(c_fin[:n_ctx], M_HEADS, M_DK, M_DV))
        nf = n_fin[:n_ctx].reshape(n_ctx, 2, M_HEADS, M_DK, LANES)
        new_n.append(jnp.stack([nf[:, :, h, :, h] for h in range(M_HEADS)], axis=2))
        new_m.append(m_fin[:n_ctx, :, 0, 0:M_HEADS])
        new_s.append(_diag_blocks(s_fin[:n_ctx], G_HEADS, G_DK, G_DV))
        new_ckv.append(ckv[:n_ctx_rows].reshape(n_ctx, ctx_len, A_DC))
        new_kr.append(zs[:n_ctx_rows, KR_LO:KR_HI].reshape(n_ctx, ctx_len, A_DROPE))

    y_prompt = x[:n_ctx_rows].reshape(n_ctx, ctx_len, d)
    y_sample = x[n_ctx_rows:].reshape(n_dec, dec_len, d)
    st = lambda xs: jnp.stack(xs, axis=1)
    return (y_prompt, y_sample, st(new_c), st(new_n), st(new_m), st(new_s), st(new_ckv), st(new_kr))
```

```python
import functools

import numpy as np
import jax
import jax.numpy as jnp
from jax import lax
from jax.experimental import pallas as pl
from jax.experimental.pallas import tpu as pltpu

F32 = jnp.float32
BF16 = jnp.bfloat16

D_MODEL = 1024
GRID_W = 64
M_HEADS, M_DK, M_DV = 4, 32, 64
G_HEADS, G_DK, G_DV = 4, 32, 64
G_RANK = 16
G_TAU = 16.0
A_HEADS, A_DNOPE, A_DROPE, A_DV = 8, 64, 32, 64
A_DQ, A_DC = 256, 128
A_SCALE = (A_DNOPE + A_DROPE) ** -0.5
ROPE_BASE = 10000.0
IN_SPLITS = (M_HEADS * M_DK, M_HEADS * M_DK, M_HEADS * M_DV, M_HEADS * M_DV, 4 * M_HEADS,
             G_HEADS * G_DK, G_HEADS * G_DK, G_HEADS * G_DV, G_HEADS * G_DV, 2 * G_RANK,
             A_DQ, A_DC, A_DROPE)

LANES = 128
ROW_BLOCK = 256
M_CHUNK = 128
G_CHUNK = 64
G_LEVELS = 6
A_HPAD = 128
VMEM_LIMIT = 56 * 1024 * 1024

ZM_W = 768
ZG_W = 768
ZQ_W = 384
ZS_W = 256
Z_W = ZM_W + ZG_W + ZQ_W + ZS_W
KR_LO, KR_HI = A_DNOPE, A_DNOPE + A_DROPE


def _cparams(sem):
    return pltpu.CompilerParams(dimension_semantics=sem, vmem_limit_bytes=VMEM_LIMIT)


def _dot(a, b):
    return jnp.dot(a.astype(BF16), b.astype(BF16), preferred_element_type=F32)


def _dot_nt(a, b):
    return lax.dot_general(a.astype(BF16), b.astype(BF16), (((1,), (1,)), ((), ())),
                           preferred_element_type=F32)


def _dot_tn(a, b):
    return lax.dot_general(a.astype(BF16), b.astype(BF16), (((0,), (0,)), ((), ())),
                           preferred_element_type=F32)


def _split(x):
    hi = x.astype(BF16)
    lo = (x - hi.astype(F32)).astype(BF16)
    return hi, lo


def _dot_c_x(c, x):
    hi, lo = _split(x)
    return jnp.dot(c, hi, preferred_element_type=F32) + jnp.dot(c, lo, preferred_element_type=F32)


def _dot_x_c(x, c):
    hi, lo = _split(x)
    return jnp.dot(hi, c, preferred_element_type=F32) + jnp.dot(lo, c, preferred_element_type=F32)


def _dot_x_ct(x, c):
    hi, lo = _split(x)
    dn = (((1,), (1,)), ((), ()))
    return (lax.dot_general(hi, c, dn, preferred_element_type=F32)
            + lax.dot_general(lo, c, dn, preferred_element_type=F32))


def _dot_xt_c(x, c):
    hi, lo = _split(x)
    dn = (((0,), (0,)), ((), ()))
    return (lax.dot_general(hi, c, dn, preferred_element_type=F32)
            + lax.dot_general(lo, c, dn, preferred_element_type=F32))


def _log_sigmoid(x):
    return jnp.minimum(x, 0.0) - jnp.log1p(jnp.exp(-jnp.abs(x)))


def _sigmoid(x):
    return 1.0 / (1.0 + jnp.exp(-x))


def _ada_kernel(c_ref, w_ref, b_ref, o_ref):
    c = c_ref[...]
    s = c * _sigmoid(c)
    o_ref[0] = _dot(s, w_ref[0]) + b_ref[0]


def _ada_call(cond8, w_ada, b_ada):
    depth, d, n = w_ada.shape
    tn = 1536
    return pl.pallas_call(
        _ada_kernel,
        grid=(depth, n // tn),
        in_specs=[pl.BlockSpec((8, d), lambda l, j: (0, 0)),
                  pl.BlockSpec((1, d, tn), lambda l, j: (l, 0, j)),
                  pl.BlockSpec((1, 1, tn), lambda l, j: (l, 0, j))],
        out_specs=pl.BlockSpec((1, 8, tn), lambda l, j: (l, 0, j)),
        out_shape=jax.ShapeDtypeStruct((depth, 8, n), F32),
        compiler_params=_cparams(("arbitrary", "arbitrary")),
        name="ada_mod",
    )(cond8, w_ada, b_ada.reshape(depth, 1, n))


def _inproj_kernel(x_ref, mod_ref, w_ref, wgt_ref, zm_ref, zg_ref, zq_ref, zs_ref, zt_ref):
    d = D_MODEL
    mod = mod_ref[0]
    h = (x_ref[...] * (1.0 + mod[:, d:2 * d]) + mod[:, 0:d]).astype(BF16)
    z = jnp.dot(h, w_ref[...], preferred_element_type=F32)
    zm_ref[...] = z[:, 0:ZM_W]
    zg_ref[...] = z[:, ZM_W:ZM_W + ZG_W]
    zq_ref[...] = z[:, ZM_W + ZG_W:ZM_W + ZG_W + ZQ_W]
    zs_ref[...] = z[:, ZM_W + ZG_W + ZQ_W:Z_W]
    zt_ref[...] = lax.dot_general(wgt_ref[...], h, (((1,), (1,)), ((), ())), preferred_element_type=F32)


def _inproj_call(x, mods, w_in_p, w_gt, mod_map):
    nt, d = x.shape
    tm = ROW_BLOCK
    row = lambda i: (i, 0)
    const = lambda i: (0, 0)
    return pl.pallas_call(
        _inproj_kernel,
        grid=(nt // tm,),
        in_specs=[pl.BlockSpec((tm, d), row),
                  pl.BlockSpec((1, 1, 6 * d), lambda i: (mod_map(i), 0, 0)),
                  pl.BlockSpec((d, Z_W), const),
                  pl.BlockSpec((16, d), const)],
        out_specs=[pl.BlockSpec((tm, ZM_W), row), pl.BlockSpec((tm, ZG_W), row),
                   pl.BlockSpec((tm, ZQ_W), row), pl.BlockSpec((tm, ZS_W), row),
                   pl.BlockSpec((16, tm), lambda i: (0, i))],
        out_shape=[jax.ShapeDtypeStruct((nt, ZM_W), F32), jax.ShapeDtypeStruct((nt, ZG_W), F32),
                   jax.ShapeDtypeStruct((nt, ZQ_W), F32), jax.ShapeDtypeStruct((nt, ZS_W), F32),
                   jax.ShapeDtypeStruct((16, nt), F32)],
        compiler_params=_cparams(("parallel",)),
        name="in_proj",
    )(x, mods, w_in_p, w_gt)


def _mlstm_chunk(d, q, k, v, graw, graw_t, tri):
    lm = q.shape[0]
    io, fo = 8 * d, 8 * d + 4
    last = lm - 1 if d == 0 else 0
    lf_c = _log_sigmoid(graw)
    lf_t = _log_sigmoid(graw_t)
    bcol = _dot_c_x(tri, lf_c)
    brow = _dot_x_ct(lf_t, tri)
    ri = lax.broadcasted_iota(jnp.int32, (lm, lm), 0)
    ci = lax.broadcasted_iota(jnp.int32, (lm, lm), 1)
    causal = (ci <= ri) if d == 0 else (ci >= ri)
    head32 = lax.broadcasted_iota(jnp.int32, (lm, LANES), 1) // M_DK
    head64 = lax.broadcasted_iota(jnp.int32, (lm, M_HEADS * M_DV), 1) // M_DV
    rowhead = lax.broadcasted_iota(jnp.int32, (M_HEADS * M_DK, 1), 0) // M_DK
    lane1 = lax.broadcasted_iota(jnp.int32, (1, LANES), 1)

    qb = q.astype(BF16)
    vb = v.astype(BF16)

    num = jnp.zeros((lm, M_HEADS * M_DV), F32)
    w_exp = jnp.zeros((lm, LANES), F32)
    per_head = []
    for h in range(M_HEADS):
        col_b = bcol[:, fo + h:fo + h + 1]
        row_g = graw_t[io + h:io + h + 1, :] - brow[fo + h:fo + h + 1, :]
        dmat = jnp.where(causal, col_b + row_g, -jnp.inf)
        mintra = jnp.max(dmat, axis=1, keepdims=True)
        qk = lax.dot_general(qb, jnp.where(head32 == h, k, 0.0).astype(BF16),
                             (((1,), (1,)), ((), ())), preferred_element_type=F32)
        p = qk * jnp.exp(dmat - mintra)
        den_a = jnp.sum(p, axis=1, keepdims=True)
        num = num + jnp.dot(p.astype(BF16), jnp.where(head64 == h, v, 0.0).astype(BF16),
                            preferred_element_type=F32)
        bl = col_b[last:last + 1, :]
        g = bl - col_b + graw[:, io + h:io + h + 1]
        gmax = jnp.max(g, axis=0, keepdims=True)
        w_exp = jnp.where(head32 == h, jnp.exp(g - gmax), w_exp)
        per_head.append((col_b, mintra, den_a, bl, gmax))
    kw = (k * w_exp).astype(BF16)
    dn = (((0,), (0,)), ((), ()))
    upd_c = lax.dot_general(kw, vb, dn, preferred_element_type=F32)
    upd_n = lax.dot_general(kw, jnp.ones((lm, LANES), BF16), dn, preferred_element_type=F32)
    rc = lax.broadcasted_iota(jnp.int32, (M_HEADS * M_DK, M_HEADS * M_DV), 0) // M_DK
    cc = lax.broadcasted_iota(jnp.int32, (M_HEADS * M_DK, M_HEADS * M_DV), 1) // M_DV
    rn = lax.broadcasted_iota(jnp.int32, (M_HEADS * M_DK, LANES), 0) // M_DK
    cn = lax.broadcasted_iota(jnp.int32, (M_HEADS * M_DK, LANES), 1)
    upd_c = jnp.where(rc == cc, upd_c, 0.0)
    upd_n = jnp.where(rn == cn, upd_n, 0.0)

    def finish(c_st, n_st, m_st):
        q_c = jnp.dot(qb, c_st.astype(BF16), preferred_element_type=F32)
        q_n = jnp.dot(qb, n_st.astype(BF16), preferred_element_type=F32)
        f1_exp = jnp.zeros((lm, M_HEADS * M_DV), F32)
        f2_exp = jnp.zeros((lm, M_HEADS * M_DV), F32)
        r_exp = jnp.zeros((lm, M_HEADS * M_DV), F32)
        a_row = jnp.zeros((M_HEADS * M_DK, 1), F32)
        b_row = jnp.zeros((M_HEADS * M_DK, 1), F32)
        m_row = jnp.zeros((1, LANES), F32)
        for h in range(M_HEADS):
            col_b, mintra, den_a, bl, gmax = per_head[h]
            m_h = m_st[:, h:h + 1]
            a = col_b + m_h
            mloc = jnp.maximum(a, mintra)
            f1 = jnp.exp(mintra - mloc)
            f2 = jnp.exp(a - mloc)
            den = f1 * den_a + f2 * q_n[:, h:h + 1]
            rden = 1.0 / jnp.maximum(jnp.abs(den), jnp.exp(-mloc))
            f1_exp = jnp.where(head64 == h, f1, f1_exp)
            f2_exp = jnp.where(head64 == h, f2, f2_exp)
            r_exp = jnp.where(head64 == h, rden, r_exp)
            m_new = jnp.maximum(bl + m_h, gmax)
            a_row = jnp.where(rowhead == h, jnp.exp(bl + m_h - m_new), a_row)
            b_row = jnp.where(rowhead == h, jnp.exp(gmax - m_new), b_row)
            m_row = jnp.where(lane1 == h, m_new, m_row)
        h_out = (f1_exp * num + f2_exp * q_c) * r_exp
        return h_out, a_row * c_st + b_row * upd_c, a_row * n_st + b_row * upd_n, m_row

    return finish


def _mlstm_kernel(first_ref, seq_ref, bwd_ref,
                  zmf_ref, zsf_ref, ztf_ref, zmb_ref, zsb_ref, ztb_ref,
                  c0_ref, n0_ref, m0_ref, gb_ref, gbt_ref, trif_ref, trib_ref,
                  hf_ref, hb_ref, cout_ref, nout_ref, mout_ref,
                  cs_ref, ns_ref, ms_ref):
    r = pl.program_id(0)

    @pl.when(first_ref[r] == 1)
    def _():
        cs_ref[...] = c0_ref[0]
        ns_ref[...] = n0_ref[0]
        ms_ref[...] = m0_ref[0]

    nch = ROW_BLOCK // M_CHUNK
    qk_w = M_HEADS * M_DK
    for d in (0, 1):
        zm_ref, zs_ref, zt_ref, h_ref = ((zmf_ref, zsf_ref, ztf_ref, hf_ref),
                                         (zmb_ref, zsb_ref, ztb_ref, hb_ref))[d]
        tri = (trif_ref, trib_ref)[d][...]
        order = range(nch) if d == 0 else range(nch - 1, -1, -1)
        finishers = []
        for c in order:
            rows = slice(c * M_CHUNK, (c + 1) * M_CHUNK)
            q = zm_ref[rows, 0:qk_w]
            k = zm_ref[rows, qk_w:2 * qk_w] * (M_DK ** -0.5)
            v = zm_ref[rows, 2 * qk_w:2 * qk_w + M_HEADS * M_DV]
            graw = zs_ref[rows, :] + gb_ref[...]
            graw_t = zt_ref[:, rows] + gbt_ref[...]
            finishers.append((rows, _mlstm_chunk(d, q, k, v, graw, graw_t, tri)))
        state = (cs_ref[d], ns_ref[d], ms_ref[d])
        for rows, finish in finishers:
            h_out, *state = finish(*state)
            h_ref[rows, :] = h_out
        cs_ref[d], ns_ref[d], ms_ref[d] = state
        cout_ref[0, d], nout_ref[0, d], mout_ref[0, d] = state


def _mlstm_call(plan, zm, zs, zt, c0, n0, m0, gb, gbt, trif, trib):
    nt = zm.shape[0]
    nb = nt // ROW_BLOCK
    nseq = c0.shape[0]
    hw = M_HEADS * M_DV
    kd = M_HEADS * M_DK
    fwd = lambda r, first, seq, bwd: (r, 0)
    bwd_ = lambda r, first, seq, bwd: (bwd[r], 0)
    fwd_t = lambda r, first, seq, bwd: (0, r)
    bwd_t = lambda r, first, seq, bwd: (0, bwd[r])
    seq4 = lambda r, first, seq, bwd: (seq[r], 0, 0, 0)
    const = lambda r, first, seq, bwd: (0, 0)
    grid_spec = pltpu.PrefetchScalarGridSpec(
        num_scalar_prefetch=3,
        grid=(nb,),
        in_specs=[pl.BlockSpec((ROW_BLOCK, 512), fwd), pl.BlockSpec((ROW_BLOCK, LANES), fwd),
                  pl.BlockSpec((16, ROW_BLOCK), fwd_t),
                  pl.BlockSpec((ROW_BLOCK, 512), bwd_), pl.BlockSpec((ROW_BLOCK, LANES), bwd_),
                  pl.BlockSpec((16, ROW_BLOCK), bwd_t),
                  pl.BlockSpec((1, 2, kd, hw), seq4), pl.BlockSpec((1, 2, kd, LANES), seq4),
                  pl.BlockSpec((1, 2, 1, LANES), seq4),
                  pl.BlockSpec((1, LANES), const), pl.BlockSpec((16, 1), const),
                  pl.BlockSpec((M_CHUNK, M_CHUNK), const), pl.BlockSpec((M_CHUNK, M_CHUNK), const)],
        out_specs=[pl.BlockSpec((ROW_BLOCK, hw), fwd), pl.BlockSpec((ROW_BLOCK, hw), bwd_),
                   pl.BlockSpec((1, 2, kd, hw), seq4), pl.BlockSpec((1, 2, kd, LANES), seq4),
                   pl.BlockSpec((1, 2, 1, LANES), seq4)],
        scratch_shapes=[pltpu.VMEM((2, kd, hw), F32), pltpu.VMEM((2, kd, LANES), F32),
                        pltpu.VMEM((2, 1, LANES), F32)],
    )
    return pl.pallas_call(
        _mlstm_kernel,
        grid_spec=grid_spec,
        out_shape=[jax.ShapeDtypeStruct((nt, hw), F32), jax.ShapeDtypeStruct((nt, hw), F32),
                   jax.ShapeDtypeStruct((nseq, 2, kd, hw), F32),
                   jax.ShapeDtypeStruct((nseq, 2, kd, LANES), F32),
                   jax.ShapeDtypeStruct((nseq, 2, 1, LANES), F32)],
        compiler_params=_cparams(("arbitrary",)),
        name="mlstm_scan",
    )(plan["first"], plan["seq"], plan["bwd"], zm, zs, zt, zm, zs, zt, c0, n0, m0, gb, gbt, trif, trib)


def _gla_consts():
    n = G_CHUNK
    t = np.arange(n)[:, None]
    u = np.arange(n)[None, :]
    a_blocks, b_blocks, masks = [], [], [np.eye(n)]
    for lev in range(G_LEVELS):
        b = 1 << lev
        mid = (t // (2 * b)) * (2 * b) + b
        right = (t % (2 * b)) >= b
        a_blocks.append((right & (u >= mid) & (u <= t)).astype(np.float32))
        b_blocks.append((~right & (u > t) & (u <= mid - 1)).astype(np.float32))
        same_parent = (t // (2 * b)) == (u // (2 * b))
        masks.append((same_parent & right & ((u % (2 * b)) < b)).astype(np.float32))
    incl = (u <= t).astype(np.float32)
    strict_suffix = (u > t).astype(np.float32)
    acat_f = np.concatenate(a_blocks + b_blocks + [incl, strict_suffix], axis=0)
    mcat_f = np.stack([np.tile(m, (1, G_HEADS)) for m in masks])
    flip = lambda m: m[::-1, ::-1]
    acat_b = np.concatenate([flip(m) for m in a_blocks + b_blocks + [incl, strict_suffix]], axis=0)
    mcat_b = np.stack([np.tile(flip(m), (1, G_HEADS)) for m in masks])
    return (np.stack([acat_f, acat_b]), np.stack([mcat_f, mcat_b]))


def _gla_kernel(first_ref, seq_ref, bwd_ref,
                zgf_ref, zsf_ref, zgb_ref, zsb_ref, s0_ref, w2_ref, b2_ref, acat_ref, mcat_ref,
                of_ref, ob_ref, sout_ref, ss_ref):
    r = pl.program_id(0)

    @pl.when(first_ref[r] == 1)
    def _():
        ss_ref[...] = s0_ref[0]

    n = G_CHUNK
    nch = ROW_BLOCK // n
    kd = G_HEADS * G_DK
    vd = G_HEADS * G_DV
    head32 = lax.broadcasted_iota(jnp.int32, (n, kd), 1) // G_DK
    head64 = lax.broadcasted_iota(jnp.int32, (n, vd), 1) // G_DV
    rc = lax.broadcasted_iota(jnp.int32, (kd, vd), 0) // G_DK
    cc = lax.broadcasted_iota(jnp.int32, (kd, vd), 1) // G_DV
    ones_v = jnp.ones((n, vd), BF16)

    def bd_k(kt):
        return jnp.concatenate([jnp.where(head32 == h, kt, 0.0).astype(BF16)
                                for h in range(G_HEADS)], axis=0)

    states = [ss_ref[0], ss_ref[1]]
    for ci in range(nch):
        for d in (0, 1):
            zg_ref, zs_ref, o_ref = ((zgf_ref, zsf_ref, of_ref), (zgb_ref, zsb_ref, ob_ref))[d]
            c = ci if d == 0 else nch - 1 - ci
            rows = slice(c * n, (c + 1) * n)
            q = zg_ref[rows, 0:kd] * (G_DK ** -0.5)
            k = zg_ref[rows, kd:2 * kd]
            v = zg_ref[rows, 2 * kd:2 * kd + vd]
            x = jnp.dot(zs_ref[rows, :].astype(BF16), w2_ref[d], preferred_element_type=F32) + b2_ref[d]
            lg = _log_sigmoid(x) * (1.0 / G_TAU)
            e = jnp.exp(_dot_c_x(acat_ref[d], lg))
            att = mcat_ref[d, 0] * _dot_nt(q, bd_k(k))
            for lev in range(G_LEVELS):
                qt = q * e[lev * n:(lev + 1) * n]
                kt = k * e[(G_LEVELS + lev) * n:(G_LEVELS + lev + 1) * n]
                att = att + mcat_ref[d, lev + 1] * _dot_nt(qt, bd_k(kt))
            v_bd = jnp.concatenate([jnp.where(head64 == h, v, 0.0).astype(BF16)
                                    for h in range(G_HEADS)], axis=0)
            intra = jnp.dot(att.astype(BF16), v_bd, preferred_element_type=F32)
            inter = _dot(q * e[2 * G_LEVELS * n:(2 * G_LEVELS + 1) * n], states[d])
            o_ref[rows, :] = inter + intra
            decay = jnp.exp(_dot_xt_c(lg, ones_v))
            upd = _dot_tn(k * e[(2 * G_LEVELS + 1) * n:(2 * G_LEVELS + 2) * n], v)
            states[d] = decay * states[d] + jnp.where(rc == cc, upd, 0.0)
    for d in (0, 1):
        ss_ref[d] = states[d]
        sout_ref[0, d] = states[d]


def _gla_call(plan, zg, zs, s0, w2p, b2p, acat, mcat):
    nt = zg.shape[0]
    nb = nt // ROW_BLOCK
    nseq = s0.shape[0]
    kd = G_HEADS * G_DK
    vd = G_HEADS * G_DV
    n = G_CHUNK
    fwd = lambda r, first, seq, bwd: (r, 0)
    bwd_ = lambda r, first, seq, bwd: (bwd[r], 0)
    seq4 = lambda r, first, seq, bwd: (seq[r], 0, 0, 0)
    const3 = lambda r, first, seq, bwd: (0, 0, 0)
    const4 = lambda r, first, seq, bwd: (0, 0, 0, 0)
    grid_spec = pltpu.PrefetchScalarGridSpec(
        num_scalar_prefetch=3,
        grid=(nb,),
        in_specs=[pl.BlockSpec((ROW_BLOCK, 512), fwd), pl.BlockSpec((ROW_BLOCK, LANES), fwd),
                  pl.BlockSpec((ROW_BLOCK, 512), bwd_), pl.BlockSpec((ROW_BLOCK, LANES), bwd_),
                  pl.BlockSpec((1, 2, kd, vd), seq4),
                  pl.BlockSpec((2, LANES, kd), const3), pl.BlockSpec((2, 1, kd), const3),
                  pl.BlockSpec((2, (2 * G_LEVELS + 2) * n, n), const3),
                  pl.BlockSpec((2, G_LEVELS + 1, n, G_HEADS * n), const4)],
        out_specs=[pl.BlockSpec((ROW_BLOCK, vd), fwd), pl.BlockSpec((ROW_BLOCK, vd), bwd_),
                   pl.BlockSpec((1, 2, kd, vd), seq4)],
        scratch_shapes=[pltpu.VMEM((2, kd, vd), F32)],
    )
    return pl.pallas_call(
        _gla_kernel,
        grid_spec=grid_spec,
        out_shape=[jax.ShapeDtypeStruct((nt, vd), F32), jax.ShapeDtypeStruct((nt, vd), F32),
                   jax.ShapeDtypeStruct((nseq, 2, kd, vd), F32)],
        compiler_params=_cparams(("arbitrary",)),
        name="gla_scan",
    )(plan["first"], plan["seq"], plan["bwd"], zg, zs, zg, zs, s0, w2p, b2p, acat, mcat)


def _rms(x, g, eps=1e-6):
    return x * lax.rsqrt(jnp.mean(x * x, axis=-1, keepdims=True) + eps) * g


def _mla_prep_kernel(zq_ref, zs_ref, cos_ref, sin_ref, kcos_ref, gq_ref, gkv_ref,
                     wuq_ref, wuqs_ref, wk_ref, wv_ref, q_ref, k_ref, v_ref, ckv_ref):
    cq = _rms(zq_ref[:, 0:A_DQ], gq_ref[...]).astype(BF16)
    qn = jnp.dot(cq, wuq_ref[...], preferred_element_type=F32)
    qs = jnp.dot(cq, wuqs_ref[...], preferred_element_type=F32)
    cos = cos_ref[...]
    sin = sin_ref[...]
    for h in range(A_HEADS):
        sl = slice(h * A_HPAD, (h + 1) * A_HPAD)
        q_ref[:, sl] = ((qn[:, sl] * cos + qs[:, sl] * sin) * A_SCALE).astype(BF16)
    ckv = _rms(zq_ref[:, A_DQ:A_DQ + A_DC], gkv_ref[...])
    ckv_ref[...] = ckv
    ckvb = ckv.astype(BF16)
    kn = jnp.dot(ckvb, wk_ref[...], preferred_element_type=F32)
    v_ref[...] = jnp.dot(ckvb, wv_ref[...], preferred_element_type=F32).astype(BF16)
    kr = zs_ref[:, 0:LANES] * kcos_ref[...] + zs_ref[:, LANES:2 * LANES] * sin
    for h in range(A_HEADS):
        sl = slice(h * A_HPAD, (h + 1) * A_HPAD)
        k_ref[:, sl] = (kn[:, sl] + kr).astype(BF16)


def _mla_prep_call(zq, zs, cos, sin, kcos, gq, gkv, wuq, wuqs, wk, wv):
    nt = zq.shape[0]
    tm = ROW_BLOCK
    hw = A_HEADS * A_HPAD
    row = lambda i: (i, 0)
    const = lambda i: (0, 0)
    return pl.pallas_call(
        _mla_prep_kernel,
        grid=(nt // tm,),
        in_specs=[pl.BlockSpec((tm, ZQ_W), row), pl.BlockSpec((tm, ZS_W), row),
                  pl.BlockSpec((tm, LANES), row), pl.BlockSpec((tm, LANES), row), pl.BlockSpec((tm, LANES), row),
                  pl.BlockSpec((1, A_DQ), const), pl.BlockSpec((1, A_DC), const),
                  pl.BlockSpec((A_DQ, hw), const), pl.BlockSpec((A_DQ, hw), const),
                  pl.BlockSpec((A_DC, hw), const), pl.BlockSpec((A_DC, hw), const)],
        out_specs=[pl.BlockSpec((tm, hw), row), pl.BlockSpec((tm, hw), row), pl.BlockSpec((tm, hw), row),
                   pl.BlockSpec((tm, A_DC), row)],
        out_shape=[jax.ShapeDtypeStruct((nt, hw), BF16), jax.ShapeDtypeStruct((nt, hw), BF16),
                   jax.ShapeDtypeStruct((nt, hw), BF16), jax.ShapeDtypeStruct((nt, A_DC), F32)],
        compiler_params=_cparams(("parallel",)),
        name="mla_prep",
    )(zq, zs, cos, sin, kcos, gq, gkv, wuq, wuqs, wk, wv)


def _cache_kv_kernel(ckv_ref, kr_ref, wk_ref, wv_ref, k_ref, v_ref):
    ckvb = ckv_ref[...].astype(BF16)
    kn = jnp.dot(ckvb, wk_ref[...], preferred_element_type=F32)
    v_ref[...] = jnp.dot(ckvb, wv_ref[...], preferred_element_type=F32).astype(BF16)
    kr = kr_ref[...]
    for h in range(A_HEADS):
        sl = slice(h * A_HPAD, (h + 1) * A_HPAD)
        k_ref[:, sl] = (kn[:, sl] + kr).astype(BF16)


def _cache_kv_call(ckv, krp, wk, wv):
    nt = ckv.shape[0]
    tm = ROW_BLOCK
    hw = A_HEADS * A_HPAD
    row = lambda i: (i, 0)
    const = lambda i: (0, 0)
    return pl.pallas_call(
        _cache_kv_kernel,
        grid=(nt // tm,),
        in_specs=[pl.BlockSpec((tm, A_DC), row), pl.BlockSpec((tm, LANES), row),
                  pl.BlockSpec((A_DC, hw), const), pl.BlockSpec((A_DC, hw), const)],
        out_specs=[pl.BlockSpec((tm, hw), row), pl.BlockSpec((tm, hw), row)],
        out_shape=[jax.ShapeDtypeStruct((nt, hw), BF16), jax.ShapeDtypeStruct((nt, hw), BF16)],
        compiler_params=_cparams(("parallel",)),
        name="mla_cache_kv",
    )(ckv, krp, wk, wv)


def _attn_kernel(*refs, n_seg):
    q_ref = refs[0]
    kv_refs = refs[1:1 + 2 * n_seg]
    o_ref = refs[1 + 2 * n_seg]
    for h in range(A_HEADS):
        sl = slice(h * A_HPAD, (h + 1) * A_HPAD)
        q = q_ref[:, sl]
        dn = (((1,), (1,)), ((), ()))
        s = [lax.dot_general(q, kv_refs[2 * j][:, sl], dn, preferred_element_type=F32) for j in range(n_seg)]
        m = s[0].max(axis=1, keepdims=True)
        for sj in s[1:]:
            m = jnp.maximum(m, sj.max(axis=1, keepdims=True))
        p = [jnp.exp(sj - m) for sj in s]
        l = p[0].sum(axis=1, keepdims=True)
        for pj in p[1:]:
            l = l + pj.sum(axis=1, keepdims=True)
        o = jnp.dot(p[0].astype(BF16), kv_refs[1][:, sl], preferred_element_type=F32)
        for j in range(1, n_seg):
            o = o + jnp.dot(p[j].astype(BF16), kv_refs[2 * j + 1][:, sl], preferred_element_type=F32)
        o_ref[:, sl] = (o / l).astype(BF16)


def _attn_ctx_call(q, k, v, n_rows):
    hw = A_HEADS * A_HPAD
    tm = ROW_BLOCK
    row = lambda i: (i, 0)
    return pl.pallas_call(
        functools.partial(_attn_kernel, n_seg=1),
        grid=(n_rows // tm,),
        in_specs=[pl.BlockSpec((tm, hw), row)] * 3,
        out_specs=pl.BlockSpec((tm, hw), row),
        out_shape=jax.ShapeDtypeStruct((n_rows, hw), BF16),
        compiler_params=_cparams(("parallel",)),
        name="mla_attn_ctx",
    )(q, k, v)


def _attn_dec_call(q, k, v, kc, vc, n_ctx_rows, n_dec, dec_len, past_len):
    hw = A_HEADS * A_HPAD
    tm = ROW_BLOCK
    nq = dec_len // tm
    assert n_ctx_rows % dec_len == 0
    qmap = lambda b, i: (n_ctx_rows // tm + b * nq + i, 0)
    own = lambda b, i: (n_ctx_rows // dec_len + b, 0)
    return pl.pallas_call(
        functools.partial(_attn_kernel, n_seg=2),
        grid=(n_dec, nq),
        in_specs=[pl.BlockSpec((tm, hw), qmap),
                  pl.BlockSpec((past_len, hw), lambda b, i: (b, 0)),
                  pl.BlockSpec((past_len, hw), lambda b, i: (b, 0)),
                  pl.BlockSpec((dec_len, hw), own),
                  pl.BlockSpec((dec_len, hw), own)],
        out_specs=pl.BlockSpec((tm, hw), lambda b, i: (b * nq + i, 0)),
        out_shape=jax.ShapeDtypeStruct((n_dec * dec_len, hw), BF16),
        compiler_params=_cparams(("parallel", "arbitrary")),
        name="mla_attn_dec",
    )(q, kc, vc, k, v)


def _layer_norm(y, g, b, eps=1e-5):
    yc = y - jnp.mean(y, axis=-1, keepdims=True)
    return yc * lax.rsqrt(jnp.mean(yc * yc, axis=-1, keepdims=True) + eps) * g + b


def _outproj_kernel(hf_ref, hb_ref, of_ref, ob_ref, a_ref, mo_ref, gg_ref, x_ref, mod_ref,
                    mng_ref, gng_ref, avg_ref, wm_ref, wg_ref, wa_ref, lg_ref, lb_ref, o_ref, *, alpha):
    d = D_MODEL
    avg = avg_ref[...]
    hm = hf_ref[...] + hb_ref[...]
    hc = hm - _dot_x_c(hm, avg)
    m_out = hc * lax.rsqrt(_dot_x_c(hc * hc, avg) + 1e-6) * mng_ref[...] * _sigmoid(mo_ref[...])
    og = of_ref[...] + ob_ref[...]
    gg = gg_ref[...]
    g_out = og * lax.rsqrt(_dot_x_c(og * og, avg) + 1e-6) * gng_ref[...] * (gg * _sigmoid(gg))
    mix = (_dot(m_out, wm_ref[...]) + _dot(g_out, wg_ref[...])
           + jnp.dot(a_ref[...], wa_ref[...], preferred_element_type=F32))
    mod = mod_ref[0]
    y = alpha * x_ref[...] + mod[:, 2 * d:3 * d] * mix
    o_ref[...] = _layer_norm(y, lg_ref[...], lb_ref[...])


def _outproj_call(hf, hb, of, ob, a, zm, zg, x, mods, mng, gng, avg, wm, wg, wa, lg, lb, mod_map, alpha):
    nt, d = x.shape
    tm = ROW_BLOCK
    hw = M_HEADS * M_DV
    row = lambda i: (i, 0)
    const = lambda i: (0, 0)
    gate_col = lambda i: (i, 2)
    return pl.pallas_call(
        functools.partial(_outproj_kernel, alpha=alpha),
        grid=(nt // tm,),
        in_specs=[pl.BlockSpec((tm, hw), row)] * 4
                 + [pl.BlockSpec((tm, A_HEADS * A_HPAD), row),
                    pl.BlockSpec((tm, hw), gate_col), pl.BlockSpec((tm, hw), gate_col),
                    pl.BlockSpec((tm, d), row),
                    pl.BlockSpec((1, 1, 6 * d), lambda i: (mod_map(i), 0, 0)),
                    pl.BlockSpec((1, hw), const), pl.BlockSpec((1, hw), const),
                    pl.BlockSpec((hw, hw), const),
                    pl.BlockSpec((hw, d), const), pl.BlockSpec((hw, d), const),
                    pl.BlockSpec((A_HEADS * A_HPAD, d), const),
                    pl.BlockSpec((1, d), const), pl.BlockSpec((1, d), const)],
        out_specs=pl.BlockSpec((tm, d), row),
        out_shape=jax.ShapeDtypeStruct((nt, d), F32),
        compiler_params=_cparams(("parallel",)),
        name="out_proj_ln",
    )(hf, hb, of, ob, a, zm, zg, x, mods, mng, gng, avg, wm, wg, wa, lg, lb)


def _ffn_kernel(x_ref, mod_ref, wg_ref, wu_ref, wd_ref, lg_ref, lb_ref, o_ref, *, n_chunks, alpha):
    d = D_MODEL
    mod = mod_ref[0]
    x = x_ref[...]
    hh = (x * (1.0 + mod[:, 4 * d:5 * d]) + mod[:, 3 * d:4 * d]).astype(BF16)
    f = wg_ref.shape[1]
    cw = f // n_chunks
    acc = jnp.zeros(x.shape, F32)
    for c in range(n_chunks):
        sl = slice(c * cw, (c + 1) * cw)
        g = jnp.dot(hh, wg_ref[:, sl], preferred_element_type=F32)
        u = jnp.dot(hh, wu_ref[:, sl], preferred_element_type=F32)
        act = (g * _sigmoid(g) * u).astype(BF16)
        acc = acc + jnp.dot(act, wd_ref[sl, :], preferred_element_type=F32)
    y = alpha * x + mod[:, 5 * d:6 * d] * acc
    o_ref[...] = _layer_norm(y, lg_ref[...], lb_ref[...])


def _ffn_call(x, mods, wg, wu, wd, lg, lb, mod_map, alpha):
    nt, d = x.shape
    f = wg.shape[1]
    tm = ROW_BLOCK
    row = lambda i: (i, 0)
    const = lambda i: (0, 0)
    resident = dict(pipeline_mode=pl.Buffered(1))
    return pl.pallas_call(
        functools.partial(_ffn_kernel, n_chunks=2, alpha=alpha),
        grid=(nt // tm,),
        in_specs=[pl.BlockSpec((tm, d), row),
                  pl.BlockSpec((1, 1, 6 * d), lambda i: (mod_map(i), 0, 0)),
                  pl.BlockSpec((d, f), const, **resident), pl.BlockSpec((d, f), const, **resident),
                  pl.BlockSpec((f, d), const, **resident),
                  pl.BlockSpec((1, d), const), pl.BlockSpec((1, d), const)],
        out_specs=pl.BlockSpec((tm, d), row),
        out_shape=jax.ShapeDtypeStruct((nt, d), F32),
        compiler_params=_cparams(("parallel",)),
        name="ffn_ln",
    )(x, mods, wg, wu, wd, lg, lb)


def _block_diag_heads(t):
    h = t.shape[-3]
    eye = jnp.eye(h, dtype=t.dtype)
    out = jnp.einsum('...hkv,hg->...hkgv', t, eye)
    return out.reshape(t.shape[:-3] + (h * t.shape[-2], h * t.shape[-1]))


def _diag_blocks(t, h, dk, dv):
    t = t.reshape(t.shape[:-2] + (h, dk, h, dv))
    return jnp.stack([t[..., i, :, i, :] for i in range(h)], axis=-3)


def _rope_tables(n_ctx_rows, n_dec, dec_len):
    half = A_DROPE // 4
    inv = ROPE_BASE ** (-jnp.arange(half, dtype=F32) / half)
    pos = jnp.arange(dec_len, dtype=jnp.int32)
    rows = (pos // GRID_W).astype(F32)[:, None] * inv[None, :]
    cols = (pos % GRID_W).astype(F32)[:, None] * inv[None, :]
    cr, sr, cc, sc = jnp.cos(rows), jnp.sin(rows), jnp.cos(cols), jnp.sin(cols)
    cos32 = jnp.concatenate([cr, cr, cc, cc], axis=1)
    sin32 = jnp.concatenate([-sr, sr, -sc, sc], axis=1)
    ones_lo = jnp.ones((dec_len, KR_LO), F32)
    ones_hi = jnp.ones((dec_len, A_HPAD - KR_HI), F32)
    cos = jnp.concatenate([ones_lo, cos32, ones_hi], axis=1)
    sin = jnp.concatenate([0 * ones_lo, sin32, 0 * ones_hi], axis=1)
    kcos = jnp.concatenate([0 * ones_lo, cos32, 0 * ones_hi], axis=1)
    lane = jnp.arange(A_HPAD)
    rope_lane = ((lane >= KR_LO) & (lane < KR_HI)).astype(F32)[None, :]
    ctx_cos = jnp.ones((n_ctx_rows, A_HPAD), F32)
    ctx_sin = jnp.zeros((n_ctx_rows, A_HPAD), F32)
    ctx_kcos = jnp.broadcast_to(rope_lane, (n_ctx_rows, A_HPAD))
    tile = lambda t: jnp.tile(t, (n_dec, 1))
    return (jnp.concatenate([ctx_cos, tile(cos)]), jnp.concatenate([ctx_sin, tile(sin)]),
            jnp.concatenate([ctx_kcos, tile(kcos)]))


def _rope_partner(w):
    e = A_DROPE // 4
    return jnp.concatenate([w[..., e:2 * e], w[..., 0:e], w[..., 3 * e:4 * e], w[..., 2 * e:3 * e]], axis=-1)


def kernel(x_prompt, x_sample, state_mlstm_C, state_mlstm_n, state_mlstm_m, state_gla_S, cache_mla_ckv, cache_mla_krope, c, c_ctx, w_ada, b_ada, w_in, m_gate_b, m_norm_g, g_w2, g_b2, g_norm_g, a_q_norm_g, a_kv_norm_g, a_w_uq, a_w_ukv, w_out, ln1_g, ln1_b, w_ffn_gate, w_ffn_up, w_ffn_down, ln2_g, ln2_b):
    n_ctx, ctx_len, d = x_prompt.shape
    n_dec, dec_len, _ = x_sample.shape
    depth = w_in.shape[0]
    past_len = cache_mla_ckv.shape[2]
    assert d == D_MODEL and ctx_len == ROW_BLOCK and dec_len % ROW_BLOCK == 0 and past_len % 8 == 0
    assert n_dec + 1 <= 8
    n_ctx_rows = n_ctx * ctx_len
    n_dec_rows = n_dec * dec_len
    nt = n_ctx_rows + n_dec_rows
    nb_ctx = n_ctx_rows // ROW_BLOCK
    nb_seq = dec_len // ROW_BLOCK
    nseq = n_ctx + n_dec
    alpha = (2.0 * depth) ** 0.25

    blocks = np.arange(nt // ROW_BLOCK)
    dec_b = np.maximum(blocks - nb_ctx, 0)
    is_ctx = blocks < nb_ctx
    plan = {
        "first": jnp.asarray(np.where(is_ctx, 1, (dec_b % nb_seq) == 0).astype(np.int32)),
        "seq": jnp.asarray(np.where(is_ctx, blocks, n_ctx + dec_b // nb_seq).astype(np.int32)),
        "bwd": jnp.asarray(np.where(is_ctx, blocks,
                                    nb_ctx + (dec_b // nb_seq) * nb_seq + nb_seq - 1 - dec_b % nb_seq).astype(np.int32)),
    }

    def mod_map_for(layer):
        def mod_map(i):
            return layer * 8 + jnp.where(i < nb_ctx, 0, 1 + (i - nb_ctx) // nb_seq)
        return mod_map

    cond8 = jnp.zeros((8, d), F32).at[0].set(c_ctx).at[1:1 + n_dec].set(c)
    mods = _ada_call(cond8, w_ada, b_ada).reshape(depth * 8, 1, 6 * d)

    pts = [int(p) for p in np.cumsum(IN_SPLITS)[:-1]]
    (wmq, wmk, wmv, wmo, wmg, wgq, wgk, wgv, wgg, wga, wacq, wackv, wakr) = jnp.split(w_in, pts, axis=-1)
    zpad = lambda n: jnp.zeros((depth, d, n), F32)
    s1 = jnp.concatenate([wmg, wga, zpad(KR_LO - 48), wakr, zpad(LANES - KR_HI)], axis=-1)
    s2 = jnp.concatenate([zpad(KR_LO), _rope_partner(wakr), zpad(LANES - KR_HI)], axis=-1)
    w_in_p = jnp.concatenate([wmq, wmk, wmv, wmo, wgq, wgk, wgv, wgg, wacq, wackv, s1, s2], axis=-1).astype(BF16)
    w_gt = jnp.swapaxes(wmg, 1, 2).astype(BF16)
    gate_b = jnp.zeros((depth, 1, LANES), F32).at[:, 0, 0:16].set(m_gate_b)
    gate_bt = m_gate_b.reshape(depth, 16, 1)
    w2p = jnp.zeros((depth, 2, LANES, G_HEADS * G_DK), F32)
    w2p = w2p.at[:, 0, 16:16 + G_RANK].set(g_w2[:, 0]).at[:, 1, 16 + G_RANK:16 + 2 * G_RANK].set(g_w2[:, 1]).astype(BF16)
    b2p = g_b2.reshape(depth, 2, 1, G_HEADS * G_DK)
    uq = a_w_uq.reshape(depth, A_DQ, A_HEADS, A_DNOPE + A_DROPE)
    hp = lambda n: jnp.zeros((depth, uq.shape[1], A_HEADS, n), F32)
    wuq = jnp.concatenate([uq, hp(A_HPAD - KR_HI)], axis=-1).reshape(depth, A_DQ, -1).astype(BF16)
    wuqs = jnp.concatenate([hp(KR_LO), _rope_partner(uq[..., A_DNOPE:]), hp(A_HPAD - KR_HI)],
                           axis=-1).reshape(depth, A_DQ, -1).astype(BF16)
    ukv = a_w_ukv.reshape(depth, A_DC, A_HEADS, A_DNOPE + A_DV)
    kp = jnp.zeros((depth, A_DC, A_HEADS, A_HPAD - A_DNOPE), F32)
    wk = jnp.concatenate([ukv[..., :A_DNOPE], kp], axis=-1).reshape(depth, A_DC, -1).astype(BF16)
    wv = jnp.concatenate([ukv[..., A_DNOPE:], kp], axis=-1).reshape(depth, A_DC, -1).astype(BF16)
    mw, gw = M_HEADS * M_DV, G_HEADS * G_DV
    wo_m = w_out[:, 0:mw].astype(BF16)
    wo_g = w_out[:, mw:mw + gw].astype(BF16)
    wo_a = w_out[:, mw + gw:].reshape(depth, A_HEADS, A_DV, d)
    wo_a = jnp.concatenate([wo_a, jnp.zeros((depth, A_HEADS, A_HPAD - A_DV, d), F32)], axis=2)
    wo_a = wo_a.reshape(depth, A_HEADS * A_HPAD, d).astype(BF16)
    wfg, wfu, wfd = w_ffn_gate.astype(BF16), w_ffn_up.astype(BF16), w_ffn_down.astype(BF16)

    ti = np.arange(M_CHUNK)
    trif = jnp.asarray((ti[None, :] <= ti[:, None]).astype(np.float32), BF16)
    trib = jnp.asarray((ti[None, :] >= ti[:, None]).astype(np.float32), BF16)
    acat_np, mcat_np = _gla_consts()
    acat = jnp.asarray(acat_np, BF16)
    mcat = jnp.asarray(mcat_np, F32)
    hi = np.arange(mw) // M_DV
    avg = jnp.asarray((hi[:, None] == hi[None, :]).astype(np.float32) / M_DV, BF16)
    cos, sin, kcos = _rope_tables(n_ctx_rows, n_dec, dec_len)

    def init_state(block, n_tail):
        z = jnp.zeros((n_ctx,) + block.shape[1:], F32)
        return jnp.concatenate([z, block], axis=0)

    krp_all = jnp.zeros(cache_mla_krope.shape[:-1] + (LANES,), F32).at[..., KR_LO:KR_HI].set(cache_mla_krope)

    x = jnp.concatenate([x_prompt.reshape(n_ctx_rows, d), x_sample.reshape(n_dec_rows, d)], axis=0)
    new_c, new_n, new_m, new_s, new_ckv, new_kr = [], [], [], [], [], []
    for l in range(depth):
        mod_map = mod_map_for(l)
        zm, zg, zq, zs, zt = _inproj_call(x, mods, w_in_p[l], w_gt[l], mod_map)

        c0 = init_state(_block_diag_heads(state_mlstm_C[:, l]), 0)
        n_bd = jnp.einsum('bdhk,hg->bdhkg', state_mlstm_n[:, l], jnp.eye(M_HEADS, LANES, dtype=F32))
        n0 = init_state(n_bd.reshape(n_dec, 2, M_HEADS * M_DK, LANES), 0)
        m0 = init_state(jnp.zeros((n_dec, 2, 1, LANES), F32).at[:, :, 0, 0:M_HEADS].set(state_mlstm_m[:, l]), 0)
        hf, hb, c_fin, n_fin, m_fin = _mlstm_call(plan, zm, zs, zt, c0, n0, m0, gate_b[l], gate_bt[l], trif, trib)

        s0 = init_state(_block_diag_heads(state_gla_S[:, l]), 0)
        of, ob, s_fin = _gla_call(plan, zg, zs, s0, w2p[l], b2p[l], acat, mcat)

        q, k, v, ckv = _mla_prep_call(zq, zs, cos, sin, kcos, a_q_norm_g[l][None, :], a_kv_norm_g[l][None, :],
                                      wuq[l], wuqs[l], wk[l], wv[l])
        kc, vc = _cache_kv_call(cache_mla_ckv[:, l].reshape(n_dec * past_len, A_DC),
                                krp_all[:, l].reshape(n_dec * past_len, LANES), wk[l], wv[l])
        a_ctx = _attn_ctx_call(q, k, v, n_ctx_rows)
        a_dec = _attn_dec_call(q, k, v, kc, vc, n_ctx_rows, n_dec, dec_len, past_len)
        a = jnp.concatenate([a_ctx, a_dec], axis=0)

        x = _outproj_call(hf, hb, of, ob, a, zm, zg, x, mods, m_norm_g[l][None, :], g_norm_g[l][None, :], avg,
                          wo_m[l], wo_g[l], wo_a[l], ln1_g[l][None, :], ln1_b[l][None, :], mod_map, alpha)
        x = _ffn_call(x, mods, wfg[l], wfu[l], wfd[l], ln2_g[l][None, :], ln2_b[l][None, :], mod_map, alpha)

        new_c.append(_diag_blocks(c_fin[:n_ctx], M_HEADS, M_DK, M_DV))
        nf = n_fin[:n_ctx].reshape(n_ctx, 2, M_HEADS, M_DK, LANES)
        new_n.append(jnp.stack([nf[:, :, h, :, h] for h in range(M_HEADS)], axis=2))
        new_m.append(m_fin[:n_ctx, :, 0, 0:M_HEADS])
        new_s.append(_diag_blocks(s_fin[:n_ctx], G_HEADS, G_DK, G_DV))
        new_ckv.append(ckv[:n_ctx_rows].reshape(n_ctx, ctx_len, A_DC))
        new_kr.append(zs[:n_ctx_rows, KR_LO:KR_HI].reshape(n_ctx, ctx_len, A_DROPE))

    y_prompt = x[:n_ctx_rows].reshape(n_ctx, ctx_len, d)
    y_sample = x[n_ctx_rows:].reshape(n_dec, dec_len, d)
    st = lambda xs: jnp.stack(xs, axis=1)
    return (y_prompt, y_sample, st(new_c), st(new_n), st(new_m), st(new_s), st(new_ckv), st(new_kr))
```

```python
import functools

import numpy as np
import jax
import jax.numpy as jnp
from jax import lax
from jax.experimental import pallas as pl
from jax.experimental.pallas import tpu as pltpu

F32 = jnp.float32
BF16 = jnp.bfloat16

D_MODEL = 1024
GRID_W = 64
M_HEADS, M_DK, M_DV = 4, 32, 64
G_HEADS, G_DK, G_DV = 4, 32, 64
G_RANK = 16
G_TAU = 16.0
A_HEADS, A_DNOPE, A_DROPE, A_DV = 8, 64, 32, 64
A_DQ, A_DC = 256, 128
A_SCALE = (A_DNOPE + A_DROPE) ** -0.5
ROPE_BASE = 10000.0
IN_SPLITS = (M_HEADS * M_DK, M_HEADS * M_DK, M_HEADS * M_DV, M_HEADS * M_DV, 4 * M_HEADS,
             G_HEADS * G_DK, G_HEADS * G_DK, G_HEADS * G_DV, G_HEADS * G_DV, 2 * G_RANK,
             A_DQ, A_DC, A_DROPE)

LANES = 128
ROW_BLOCK = 256
M_CHUNK = 128
G_CHUNK = 64
G_LEVELS = 6
A_HPAD = 128
VMEM_LIMIT = 56 * 1024 * 1024

ZM_W = 768
ZG_W = 768
ZQ_W = 384
ZS_W = 256
Z_W = ZM_W + ZG_W + ZQ_W + ZS_W
KR_LO, KR_HI = A_DNOPE, A_DNOPE + A_DROPE


def _cparams(sem):
    return pltpu.CompilerParams(dimension_semantics=sem, vmem_limit_bytes=VMEM_LIMIT)


def _dot(a, b):
    return jnp.dot(a.astype(BF16), b.astype(BF16), preferred_element_type=F32)


def _dot_nt(a, b):
    return lax.dot_general(a.astype(BF16), b.astype(BF16), (((1,), (1,)), ((), ())),
                           preferred_element_type=F32)


def _dot_tn(a, b):
    return lax.dot_general(a.astype(BF16), b.astype(BF16), (((0,), (0,)), ((), ())),
                           preferred_element_type=F32)


def _split(x):
    hi = x.astype(BF16)
    lo = (x - hi.astype(F32)).astype(BF16)
    return hi, lo


def _dot_c_x(c, x):
    hi, lo = _split(x)
    return jnp.dot(c, hi, preferred_element_type=F32) + jnp.dot(c, lo, preferred_element_type=F32)


def _dot_x_c(x, c):
    hi, lo = _split(x)
    return jnp.dot(hi, c, preferred_element_type=F32) + jnp.dot(lo, c, preferred_element_type=F32)


def _dot_x_ct(x, c):
    hi, lo = _split(x)
    dn = (((1,), (1,)), ((), ()))
    return (lax.dot_general(hi, c, dn, preferred_element_type=F32)
            + lax.dot_general(lo, c, dn, preferred_element_type=F32))


def _dot_xt_c(x, c):
    hi, lo = _split(x)
    dn = (((0,), (0,)), ((), ()))
    return (lax.dot_general(hi, c, dn, preferred_element_type=F32)
            + lax.dot_general(lo, c, dn, preferred_element_type=F32))


def _log_sigmoid(x):
    return jnp.minimum(x, 0.0) - jnp.log1p(jnp.exp(-jnp.abs(x)))


def _sigmoid(x):
    return 1.0 / (1.0 + jnp.exp(-x))


def _ada_kernel(c_ref, w_ref, b_ref, o_ref):
    c = c_ref[...]
    s = c * _sigmoid(c)
    o_ref[0] = _dot(s, w_ref[0]) + b_ref[0]


def _ada_call(cond8, w_ada, b_ada):
    depth, d, n = w_ada.shape
    tn = 1536
    return pl.pallas_call(
        _ada_kernel,
        grid=(depth, n // tn),
        in_specs=[pl.BlockSpec((8, d), lambda l, j: (0, 0)),
                  pl.BlockSpec((1, d, tn), lambda l, j: (l, 0, j)),
                  pl.BlockSpec((1, 1, tn), lambda l, j: (l, 0, j))],
        out_specs=pl.BlockSpec((1, 8, tn), lambda l, j: (l, 0, j)),
        out_shape=jax.ShapeDtypeStruct((depth, 8, n), F32),
        compiler_params=_cparams(("arbitrary", "arbitrary")),
        name="ada_mod",
    )(cond8, w_ada, b_ada.reshape(depth, 1, n))


def _inproj_kernel(x_ref, mod_ref, w_ref, wgt_ref, zm_ref, zg_ref, zq_ref, zs_ref, zt_ref):
    d = D_MODEL
    mod = mod_ref[0]
    h = (x_ref[...] * (1.0 + mod[:, d:2 * d]) + mod[:, 0:d]).astype(BF16)
    z = jnp.dot(h, w_ref[...], preferred_element_type=F32)
    zm_ref[...] = z[:, 0:ZM_W]
    zg_ref[...] = z[:, ZM_W:ZM_W + ZG_W]
    zq_ref[...] = z[:, ZM_W + ZG_W:ZM_W + ZG_W + ZQ_W]
    zs_ref[...] = z[:, ZM_W + ZG_W + ZQ_W:Z_W]
    zt_ref[...] = lax.dot_general(wgt_ref[...], h, (((1,), (1,)), ((), ())), preferred_element_type=F32)


def _inproj_call(x, mods, w_in_p, w_gt, mod_map):
    nt, d = x.shape
    tm = ROW_BLOCK
    row = lambda i: (i, 0)
    const = lambda i: (0, 0)
    return pl.pallas_call(
        _inproj_kernel,
        grid=(nt // tm,),
        in_specs=[pl.BlockSpec((tm, d), row),
                  pl.BlockSpec((1, 1, 6 * d), lambda i: (mod_map(i), 0, 0)),
                  pl.BlockSpec((d, Z_W), const),
                  pl.BlockSpec((16, d), const)],
        out_specs=[pl.BlockSpec((tm, ZM_W), row), pl.BlockSpec((tm, ZG_W), row),
                   pl.BlockSpec((tm, ZQ_W), row), pl.BlockSpec((tm, ZS_W), row),
                   pl.BlockSpec((16, tm), lambda i: (0, i))],
        out_shape=[jax.ShapeDtypeStruct((nt, ZM_W), F32), jax.ShapeDtypeStruct((nt, ZG_W), F32),
                   jax.ShapeDtypeStruct((nt, ZQ_W), F32), jax.ShapeDtypeStruct((nt, ZS_W), F32),
                   jax.ShapeDtypeStruct((16, nt), F32)],
        compiler_params=_cparams(("parallel",)),
        name="in_proj",
    )(x, mods, w_in_p, w_gt)


def _mlstm_chunk(d, q, k, v, graw, graw_t, tri):
    lm = q.shape[0]
    io, fo = 8 * d, 8 * d + 4
    last = lm - 1 if d == 0 else 0
    lf_c = _log_sigmoid(graw)
    lf_t = _log_sigmoid(graw_t)
    bcol = _dot_c_x(tri, lf_c)
    brow = _dot_x_ct(lf_t, tri)
    ri = lax.broadcasted_iota(jnp.int32, (lm, lm), 0)
    ci = lax.broadcasted_iota(jnp.int32, (lm, lm), 1)
    causal = (ci <= ri) if d == 0 else (ci >= ri)
    head32 = lax.broadcasted_iota(jnp.int32, (lm, LANES), 1) // M_DK
    head64 = lax.broadcasted_iota(jnp.int32, (lm, M_HEADS * M_DV), 1) // M_DV
    rowhead = lax.broadcasted_iota(jnp.int32, (M_HEADS * M_DK, 1), 0) // M_DK
    lane1 = lax.broadcasted_iota(jnp.int32, (1, LANES), 1)

    qb = q.astype(BF16)
    vb = v.astype(BF16)

    num = jnp.zeros((lm, M_HEADS * M_DV), F32)
    w_exp = jnp.zeros((lm, LANES), F32)
    per_head = []
    for h in range(M_HEADS):
        col_b = bcol[:, fo + h:fo + h + 1]
        row_g = graw_t[io + h:io + h + 1, :] - brow[fo + h:fo + h + 1, :]
        dmat = jnp.where(causal, col_b + row_g, -jnp.inf)
        mintra = jnp.max(dmat, axis=1, keepdims=True)
        qk = lax.dot_general(qb, jnp.where(head32 == h, k, 0.0).astype(BF16),
                             (((1,), (1,)), ((), ())), preferred_element_type=F32)
        p = qk * jnp.exp(dmat - mintra)
        den_a = jnp.sum(p, axis=1, keepdims=True)
        num = num + jnp.dot(p.astype(BF16), jnp.where(head64 == h, v, 0.0).astype(BF16),
                            preferred_element_type=F32)
        bl = col_b[last:last + 1, :]
        g = bl - col_b + graw[:, io + h:io + h + 1]
        gmax = jnp.max(g, axis=0, keepdims=True)
        w_exp = jnp.where(head32 == h, jnp.exp(g - gmax), w_exp)
        per_head.append((col_b, mintra, den_a, bl, gmax))
    kw = (k * w_exp).astype(BF16)
    dn = (((0,), (0,)), ((), ()))
    upd_c = lax.dot_general(kw, vb, dn, preferred_element_type=F32)
    upd_n = lax.dot_general(kw, jnp.ones((lm, LANES), BF16), dn, preferred_element_type=F32)
    rc = lax.broadcasted_iota(jnp.int32, (M_HEADS * M_DK, M_HEADS * M_DV), 0) // M_DK
    cc = lax.broadcasted_iota(jnp.int32, (M_HEADS * M_DK, M_HEADS * M_DV), 1) // M_DV
    rn = lax.broadcasted_iota(jnp.int32, (M_HEADS * M_DK, LANES), 0) // M_DK
    cn = lax.broadcasted_iota(jnp.int32, (M_HEADS * M_DK, LANES), 1)
    upd_c = jnp.where(rc == cc, upd_c, 0.0)
    upd_n = jnp.where(rn == cn, upd_n, 0.0)

    def finish(c_st, n_st, m_st):
        q_c = jnp.dot(qb, c_st.astype(BF16), preferred_element_type=F32)
        q_n = jnp.dot(qb, n_st.astype(BF16), preferred_element_type=F32)
        f1_exp = jnp.zeros((lm, M_HEADS * M_DV), F32)
        f2_exp = jnp.zeros((lm, M_HEADS * M_DV), F32)
        r_exp = jnp.zeros((lm, M_HEADS * M_DV), F32)
        a_row = jnp.zeros((M_HEADS * M_DK, 1), F32)
        b_row = jnp.zeros((M_HEADS * M_DK, 1), F32)
        m_row = jnp.zeros((1, LANES), F32)
        for h in range(M_HEADS):
            col_b, mintra, den_a, bl, gmax = per_head[h]
            m_h = m_st[:, h:h + 1]
            a = col_b + m_h
            mloc = jnp.maximum(a, mintra)
            f1 = jnp.exp(mintra - mloc)
            f2 = jnp.exp(a - mloc)
            den = f1 * den_a + f2 * q_n[:, h:h + 1]
            rden = 1.0 / jnp.maximum(jnp.abs(den), jnp.exp(-mloc))
            f1_exp = jnp.where(head64 == h, f1, f1_exp)
            f2_exp = jnp.where(head64 == h, f2, f2_exp)
            r_exp = jnp.where(head64 == h, rden, r_exp)
            m_new = jnp.maximum(bl + m_h, gmax)
            a_row = jnp.where(rowhead == h, jnp.exp(bl + m_h - m_new), a_row)
            b_row = jnp.where(rowhead == h, jnp.exp(gmax - m_new), b_row)
            m_row = jnp.where(lane1 == h, m_new, m_row)
        h_out = (f1_exp * num + f2_exp * q_c) * r_exp
        return h_out, a_row * c_st + b_row * upd_c, a_row * n_st + b_row * upd_n, m_row

    return finish


def _mlstm_kernel(first_ref, seq_ref, bwd_ref,
                  zmf_ref, zsf_ref, ztf_ref, zmb_ref, zsb_ref, ztb_ref,
                  c0_ref, n0_ref, m0_ref, gb_ref, gbt_ref, trif_ref, trib_ref,
                  hf_ref, hb_ref, cout_ref, nout_ref, mout_ref,
                  cs_ref, ns_ref, ms_ref):
    r = pl.program_id(0)

    @pl.when(first_ref[r] == 1)
    def _():
        cs_ref[...] = c0_ref[0]
        ns_ref[...] = n0_ref[0]
        ms_ref[...] = m0_ref[0]

    nch = ROW_BLOCK // M_CHUNK
    qk_w = M_HEADS * M_DK
    for d in (0, 1):
        zm_ref, zs_ref, zt_ref, h_ref = ((zmf_ref, zsf_ref, ztf_ref, hf_ref),
                                         (zmb_ref, zsb_ref, ztb_ref, hb_ref))[d]
        tri = (trif_ref, trib_ref)[d][...]
        order = range(nch) if d == 0 else range(nch - 1, -1, -1)
        finishers = []
        for c in order:
            rows = slice(c * M_CHUNK, (c + 1) * M_CHUNK)
            q = zm_ref[rows, 0:qk_w]
            k = zm_ref[rows, qk_w:2 * qk_w] * (M_DK ** -0.5)
            v = zm_ref[rows, 2 * qk_w:2 * qk_w + M_HEADS * M_DV]
            graw = zs_ref[rows, :] + gb_ref[...]
            graw_t = zt_ref[:, rows] + gbt_ref[...]
            finishers.append((rows, _mlstm_chunk(d, q, k, v, graw, graw_t, tri)))
        state = (cs_ref[d], ns_ref[d], ms_ref[d])
        for rows, finish in finishers:
            h_out, *state = finish(*state)
            h_ref[rows, :] = h_out
        cs_ref[d], ns_ref[d], ms_ref[d] = state
        cout_ref[0, d], nout_ref[0, d], mout_ref[0, d] = state


def _mlstm_call(plan, zm, zs, zt, c0, n0, m0, gb, gbt, trif, trib):
    nt = zm.shape[0]
    nb = nt // ROW_BLOCK
    nseq = c0.shape[0]
    hw = M_HEADS * M_DV
    kd = M_HEADS * M_DK
    fwd = lambda r, first, seq, bwd: (r, 0)
    bwd_ = lambda r, first, seq, bwd: (bwd[r], 0)
    fwd_t = lambda r, first, seq, bwd: (0, r)
    bwd_t = lambda r, first, seq, bwd: (0, bwd[r])
    seq4 = lambda r, first, seq, bwd: (seq[r], 0, 0, 0)
    const = lambda r, first, seq, bwd: (0, 0)
    grid_spec = pltpu.PrefetchScalarGridSpec(
        num_scalar_prefetch=3,
        grid=(nb,),
        in_specs=[pl.BlockSpec((ROW_BLOCK, 512), fwd), pl.BlockSpec((ROW_BLOCK, LANES), fwd),
                  pl.BlockSpec((16, ROW_BLOCK), fwd_t),
                  pl.BlockSpec((ROW_BLOCK, 512), bwd_), pl.BlockSpec((ROW_BLOCK, LANES), bwd_),
                  pl.BlockSpec((16, ROW_BLOCK), bwd_t),
                  pl.BlockSpec((1, 2, kd, hw), seq4), pl.BlockSpec((1, 2, kd, LANES), seq4),
                  pl.BlockSpec((1, 2, 1, LANES), seq4),
                  pl.BlockSpec((1, LANES), const), pl.BlockSpec((16, 1), const),
                  pl.BlockSpec((M_CHUNK, M_CHUNK), const), pl.BlockSpec((M_CHUNK, M_CHUNK), const)],
        out_specs=[pl.BlockSpec((ROW_BLOCK, hw), fwd), pl.BlockSpec((ROW_BLOCK, hw), bwd_),
                   pl.BlockSpec((1, 2, kd, hw), seq4), pl.BlockSpec((1, 2, kd, LANES), seq4),
                   pl.BlockSpec((1, 2, 1, LANES), seq4)],
        scratch_shapes=[pltpu.VMEM((2, kd, hw), F32), pltpu.VMEM((2, kd, LANES), F32),
                        pltpu.VMEM((2, 1, LANES), F32)],
    )
    return pl.pallas_call(
        _mlstm_kernel,
        grid_spec=grid_spec,
        out_shape=[jax.ShapeDtypeStruct((nt, hw), F32), jax.ShapeDtypeStruct((nt, hw), F32),
                   jax.ShapeDtypeStruct((nseq, 2, kd, hw), F32),
                   jax.ShapeDtypeStruct((nseq, 2, kd, LANES), F32),
                   jax.ShapeDtypeStruct((nseq, 2, 1, LANES), F32)],
        compiler_params=_cparams(("arbitrary",)),
        name="mlstm_scan",
    )(plan["first"], plan["seq"], plan["bwd"], zm, zs, zt, zm, zs, zt, c0, n0, m0, gb, gbt, trif, trib)


MS_CHUNK = 64
MS_W = M_HEADS * M_DV


def _mlstm_consts():
    n = MS_CHUNK
    t = np.arange(n)[:, None]
    s = np.arange(n)[None, :]
    tri = [(s <= t), (s >= t)]
    head_of = np.arange(MS_W) // M_DV
    kmask = (head_of[:, None] == (np.arange(M_HEADS * M_DK) // M_DK)[None, :]).astype(np.float32)
    vmask = (head_of[:, None] == head_of[None, :]).astype(np.float32)
    bd = ((np.arange(M_HEADS * M_DK) // M_DK)[:, None] == head_of[None, :]).astype(np.float32)
    sel, tritile, neg = [], [], []
    for d in (0, 1):
        io, fo = 8 * d, 8 * d + 4
        m = np.zeros((LANES, 3 * MS_W), np.float32)
        for h in range(M_HEADS):
            m[fo + h, h * M_DV:(h + 1) * M_DV] = 1.0
            m[io + h, MS_W + h * M_DV:MS_W + (h + 1) * M_DV] = 1.0
            m[fo + h, 2 * MS_W + h * M_DK:2 * MS_W + (h + 1) * M_DK] = 1.0
            m[io + h, 2 * MS_W + LANES + h * M_DK:2 * MS_W + LANES + (h + 1) * M_DK] = 1.0
        sel.append(m)
        tritile.append(np.tile(tri[d].T.astype(np.float32), (1, M_HEADS)))
        neg.append(np.tile(np.where(tri[d], 0.0, -np.inf).astype(np.float32), (1, M_HEADS)))
    return dict(tri=np.stack(tri).astype(np.float32), sel=np.stack(sel), tritile=np.stack(tritile),
                neg=np.stack(neg), kmask=kmask, vmask=vmask, bd2=np.concatenate([bd, bd], axis=1))


def _mlstm_unit(d, q, k, v, graw, graw_t, tri, sel, tritile, neg, kmask, vmask):
    n = q.shape[0]
    io, fo = 8 * d, 8 * d + 4
    last = n - 1 if d == 0 else 0
    lane = lax.broadcasted_iota(jnp.int32, (n, LANES), 1)
    lf = _log_sigmoid(graw)
    bcol = _dot_c_x(tri, lf)
    x = jnp.where((lane >= fo) & (lane < fo + M_HEADS), bcol, graw)
    ex = _dot_x_c(x, sel)
    b256, i256 = ex[:, 0:MS_W], ex[:, MS_W:2 * MS_W]
    b128, i128 = ex[:, 2 * MS_W:2 * MS_W + LANES], ex[:, 2 * MS_W + LANES:]
    rows_t = _dot_x_c(_log_sigmoid(graw_t), tritile)
    head_row = lax.broadcasted_iota(jnp.int32, (1, MS_W), 1) // M_DV
    b_row = jnp.zeros((1, MS_W), F32)
    for h in range(M_HEADS):
        b_row = jnp.where(head_row == h, rows_t[fo + h:fo + h + 1, :], b_row)
    decay = jnp.exp(b256 - b_row + neg)
    qb = q.astype(BF16)
    kb = k.astype(BF16)
    k_bd = jnp.concatenate([kb] * M_HEADS, axis=0) * kmask
    qk = lax.dot_general(qb, k_bd, (((1,), (1,)), ((), ())), preferred_element_type=F32)
    p = (qk * decay).astype(BF16)
    i_max = jnp.max(i256, axis=0, keepdims=True)
    scale = jnp.exp(i256 - i_max)
    vs = (v * scale).astype(BF16)
    sb = scale.astype(BF16)
    ve = jnp.concatenate([jnp.concatenate([vs] * M_HEADS, axis=0) * vmask,
                          jnp.concatenate([sb] * M_HEADS, axis=0) * vmask], axis=1)
    nd = jnp.dot(p, ve, preferred_element_type=F32)
    bl256 = b256[last:last + 1, :]
    g_max = jnp.max(bl256 - b256 + i256, axis=0, keepdims=True)
    g128 = b128[last:last + 1, :] - b128 + i128
    w = jnp.exp(g128 - jnp.max(g128, axis=0, keepdims=True))
    kw = (k * w).astype(BF16)
    v1 = jnp.concatenate([v.astype(BF16), jnp.ones((n, MS_W), BF16)], axis=1)
    upd = lax.dot_general(kw, v1, (((0,), (0,)), ((), ())), preferred_element_type=F32)

    def finish(cn, m, bd2):
        qcn = jnp.dot(qb, cn.astype(BF16), preferred_element_type=F32)
        a = b256 + m
        sig = jnp.maximum(i_max, a)
        f1 = jnp.exp(i_max - sig)
        f2 = jnp.exp(a - sig)
        num = f1 * nd[:, 0:MS_W] + f2 * qcn[:, 0:MS_W]
        den = f1 * nd[:, MS_W:] + f2 * qcn[:, MS_W:]
        h_out = num / jnp.maximum(jnp.abs(den), jnp.exp(-sig))
        m_new = jnp.maximum(bl256 + m, g_max)
        alpha = jnp.exp(bl256 + m - m_new)
        beta = jnp.exp(g_max - m_new)
        alpha2 = jnp.concatenate([alpha, alpha], axis=1)
        beta2 = jnp.concatenate([beta, beta], axis=1)
        return h_out, alpha2 * cn + (beta2 * bd2) * upd, m_new

    return finish


def _mlstm2_kernel(first_ref, seq_ref, bwd_ref,
                   zmf_ref, zsf_ref, ztf_ref, zmb_ref, zsb_ref, ztb_ref,
                   cn0_ref, m0_ref, gb_ref, gbt_ref,
                   tri_ref, sel_ref, tritile_ref, neg_ref, kmask_ref, vmask_ref, bd2_ref,
                   hf_ref, hb_ref, cnout_ref, mout_ref,
                   cns_ref, ms_ref):
    r = pl.program_id(0)

    fresh = first_ref[r] == 1
    nch = ROW_BLOCK // MS_CHUNK
    qk_w = M_HEADS * M_DK
    kmask = kmask_ref[...]
    vmask = vmask_ref[...]
    bd2 = bd2_ref[...]
    units = []
    for ci in range(nch):
        for d in (0, 1):
            zm_ref, zs_ref, zt_ref = ((zmf_ref, zsf_ref, ztf_ref), (zmb_ref, zsb_ref, ztb_ref))[d]
            c = ci if d == 0 else nch - 1 - ci
            rows = slice(c * MS_CHUNK, (c + 1) * MS_CHUNK)
            q = zm_ref[rows, 0:qk_w]
            k = zm_ref[rows, qk_w:2 * qk_w] * (M_DK ** -0.5)
            v = zm_ref[rows, 2 * qk_w:2 * qk_w + MS_W]
            graw = zs_ref[rows, :] + gb_ref[...]
            graw_t = zt_ref[:, rows] + gbt_ref[...]
            units.append((d, rows, _mlstm_unit(d, q, k, v, graw, graw_t, tri_ref[d], sel_ref[d],
                                               tritile_ref[d], neg_ref[d], kmask, vmask)))
    state = [(jnp.where(fresh, cn0_ref[0, d], cns_ref[d]), jnp.where(fresh, m0_ref[0, d], ms_ref[d]))
             for d in (0, 1)]
    for d, rows, finish in units:
        h_out, cn_new, m_new = finish(*state[d], bd2)
        (hf_ref, hb_ref)[d][rows, :] = h_out
        state[d] = (cn_new, m_new)
    for d in (0, 1):
        cns_ref[d], ms_ref[d] = state[d]
        cnout_ref[0, d], mout_ref[0, d] = state[d]


def _mlstm2_kernel_sm(first_ref, seq_ref, bwd_ref,
                      zmf_ref, zsf_ref, ztf_ref, zmb_ref, zsb_ref, ztb_ref,
                      cn0_ref, m0_ref, gb_ref, gbt_ref,
                      tri_ref, sel_ref, tritile_ref, neg_ref, kmask_ref, vmask_ref, bd2_ref,
                      hf_ref, hb_ref, cnout_ref, mout_ref,
                      cns_ref, ms_ref):
    r = pl.program_id(0)
    fresh = first_ref[r] == 1
    n = MS_CHUNK
    nch = ROW_BLOCK // n
    qk_w = M_HEADS * M_DK
    kmask = kmask_ref[...]
    vmask = vmask_ref[...]
    bd2 = bd2_ref[...]
    nt_dims = (((1,), (1,)), ((), ()))
    tn_dims = (((0,), (0,)), ((), ()))
    units = [(d, ci if d == 0 else nch - 1 - ci) for ci in range(nch) for d in (0, 1)]
    rows = lambda c: slice(c * n, (c + 1) * n)
    zm_refs, zs_refs, zt_refs, h_refs = (zmf_ref, zmb_ref), (zsf_ref, zsb_ref), (ztf_ref, ztb_ref), (hf_ref, hb_ref)
    fo = lambda d: 8 * d + 4
    last = lambda d: n - 1 if d == 0 else 0
    lane = lax.broadcasted_iota(jnp.int32, (n, LANES), 1)
    head_row = lax.broadcasted_iota(jnp.int32, (1, MS_W), 1) // M_DV
    f32dot = lambda a, b: jnp.dot(a, b, preferred_element_type=F32)

    def per_unit(f):
        return [f(d, c) for d, c in units]

    def dot_split(c_left, x_split, c_right):
        hi, lo = x_split
        if c_left is not None:
            return f32dot(c_left, hi) + f32dot(c_left, lo)
        return f32dot(hi, c_right) + f32dot(lo, c_right)

    q = per_unit(lambda d, c: zm_refs[d][rows(c), 0:qk_w].astype(BF16))
    k = per_unit(lambda d, c: zm_refs[d][rows(c), qk_w:2 * qk_w] * (M_DK ** -0.5))
    v = per_unit(lambda d, c: zm_refs[d][rows(c), 2 * qk_w:2 * qk_w + MS_W])
    graw = per_unit(lambda d, c: zs_refs[d][rows(c), :] + gb_ref[...])
    graw_t = per_unit(lambda d, c: zt_refs[d][:, rows(c)] + gbt_ref[...])
    lf = [_split(_log_sigmoid(g)) for g in graw]
    lf_t = [_split(_log_sigmoid(g)) for g in graw_t]
    bcol = [dot_split(tri_ref[d], s, None) for (d, _), s in zip(units, lf)]
    xs = [_split(jnp.where((lane >= fo(d)) & (lane < fo(d) + M_HEADS), b, g))
          for (d, _), b, g in zip(units, bcol, graw)]
    ex = [dot_split(None, s, sel_ref[d]) for (d, _), s in zip(units, xs)]
    b256 = [e[:, 0:MS_W] for e in ex]
    i256 = [e[:, MS_W:2 * MS_W] for e in ex]
    b128 = [e[:, 2 * MS_W:2 * MS_W + LANES] for e in ex]
    i128 = [e[:, 2 * MS_W + LANES:] for e in ex]
    rows_t = [dot_split(None, s, tritile_ref[d]) for (d, _), s in zip(units, lf_t)]
    b_row = []
    for (d, _), rt in zip(units, rows_t):
        br = jnp.zeros((1, MS_W), F32)
        for h in range(M_HEADS):
            br = jnp.where(head_row == h, rt[fo(d) + h:fo(d) + h + 1, :], br)
        b_row.append(br)
    decay = [jnp.exp(b - br + neg_ref[d]) for (d, _), b, br in zip(units, b256, b_row)]
    k_bd = [jnp.concatenate([ki.astype(BF16)] * M_HEADS, axis=0) * kmask for ki in k]
    qk = [lax.dot_general(qi, ki, nt_dims, preferred_element_type=F32) for qi, ki in zip(q, k_bd)]
    p = [(a * b).astype(BF16) for a, b in zip(qk, decay)]
    i_max = [jnp.max(i, axis=0, keepdims=True) for i in i256]
    scale = [jnp.exp(i - im) for i, im in zip(i256, i_max)]
    ve = [jnp.concatenate([jnp.concatenate([(vi * sc).astype(BF16)] * M_HEADS, axis=0) * vmask,
                           jnp.concatenate([sc.astype(BF16)] * M_HEADS, axis=0) * vmask], axis=1)
          for vi, sc in zip(v, scale)]
    nd = [f32dot(pi, vei) for pi, vei in zip(p, ve)]
    bl256 = [b[last(d):last(d) + 1, :] for (d, _), b in zip(units, b256)]
    g_max = [jnp.max(bl - b + i, axis=0, keepdims=True) for bl, b, i in zip(bl256, b256, i256)]
    g128 = [b[last(d):last(d) + 1, :] - b + i for (d, _), b, i in zip(units, b128, i128)]
    w = [jnp.exp(g - jnp.max(g, axis=0, keepdims=True)) for g in g128]
    kw = [(ki * wi).astype(BF16) for ki, wi in zip(k, w)]
    ones = jnp.ones((n, MS_W), BF16)
    upd = [lax.dot_general(kwi, jnp.concatenate([vi.astype(BF16), ones], axis=1), tn_dims,
                           preferred_element_type=F32) for kwi, vi in zip(kw, v)]
    state = [(jnp.where(fresh, cn0_ref[0, d], cns_ref[d]), jnp.where(fresh, m0_ref[0, d], ms_ref[d]))
             for d in (0, 1)]
    outs = []
    for i, (d, c) in enumerate(units):
        cn, m = state[d]
        qcn = f32dot(q[i], cn.astype(BF16))
        a = b256[i] + m
        sig = jnp.maximum(i_max[i], a)
        f1 = jnp.exp(i_max[i] - sig)
        f2 = jnp.exp(a - sig)
        num = f1 * nd[i][:, 0:MS_W] + f2 * qcn[:, 0:MS_W]
        den = f1 * nd[i][:, MS_W:] + f2 * qcn[:, MS_W:]
        outs.append(num / jnp.maximum(jnp.abs(den), jnp.exp(-sig)))
        m_new = jnp.maximum(bl256[i] + m, g_max[i])
        alpha = jnp.exp(bl256[i] + m - m_new)
        beta = jnp.exp(g_max[i] - m_new)
        state[d] = (jnp.concatenate([alpha, alpha], axis=1) * cn
                    + (jnp.concatenate([beta, beta], axis=1) * bd2) * upd[i], m_new)
    for (d, c), o in zip(units, outs):
        h_refs[d][rows(c), :] = o
    for d in (0, 1):
        cns_ref[d], ms_ref[d] = state[d]
        cnout_ref[0, d], mout_ref[0, d] = state[d]


def _mlstm2_call(plan, zm, zs, zt, cn0, m0, gb, gbt, mc):
    nt = zm.shape[0]
    nb = nt // ROW_BLOCK
    nseq = cn0.shape[0]
    kd = M_HEADS * M_DK
    n = MS_CHUNK
    fwd = lambda r, first, seq, bwd: (r, 0)
    bwd_ = lambda r, first, seq, bwd: (bwd[r], 0)
    fwd_t = lambda r, first, seq, bwd: (0, r)
    bwd_t = lambda r, first, seq, bwd: (0, bwd[r])
    seq4 = lambda r, first, seq, bwd: (seq[r], 0, 0, 0)
    const = lambda r, first, seq, bwd: (0, 0)
    const3 = lambda r, first, seq, bwd: (0, 0, 0)
    return dict(
        body=_mlstm2_kernel_sm,
        in_specs=[pl.BlockSpec((ROW_BLOCK, 512), fwd), pl.BlockSpec((ROW_BLOCK, LANES), fwd),
                  pl.BlockSpec((16, ROW_BLOCK), fwd_t),
                  pl.BlockSpec((ROW_BLOCK, 512), bwd_), pl.BlockSpec((ROW_BLOCK, LANES), bwd_),
                  pl.BlockSpec((16, ROW_BLOCK), bwd_t),
                  pl.BlockSpec((1, 2, kd, 2 * MS_W), seq4), pl.BlockSpec((1, 2, 1, MS_W), seq4),
                  pl.BlockSpec((1, LANES), const), pl.BlockSpec((16, 1), const),
                  pl.BlockSpec((2, n, n), const3), pl.BlockSpec((2, LANES, 3 * MS_W), const3),
                  pl.BlockSpec((2, n, MS_W), const3), pl.BlockSpec((2, n, MS_W), const3),
                  pl.BlockSpec((MS_W, kd), const), pl.BlockSpec((MS_W, MS_W), const),
                  pl.BlockSpec((kd, 2 * MS_W), const)],
        out_specs=[pl.BlockSpec((ROW_BLOCK, MS_W), fwd), pl.BlockSpec((ROW_BLOCK, MS_W), bwd_),
                   pl.BlockSpec((1, 2, kd, 2 * MS_W), seq4), pl.BlockSpec((1, 2, 1, MS_W), seq4)],
        scratch_shapes=[pltpu.VMEM((2, kd, 2 * MS_W), F32), pltpu.VMEM((2, 1, MS_W), F32)],
        out_shape=[jax.ShapeDtypeStruct((nt, MS_W), F32), jax.ShapeDtypeStruct((nt, MS_W), F32),
                   jax.ShapeDtypeStruct((nseq, 2, kd, 2 * MS_W), F32),
                   jax.ShapeDtypeStruct((nseq, 2, 1, MS_W), F32)],
        operands=[zm, zs, zt, zm, zs, zt, cn0, m0, gb, gbt,
                  mc["tri"], mc["sel"], mc["tritile"], mc["neg"], mc["kmask"], mc["vmask"], mc["bd2"]])


def _scan_call(plan, parts, name):
    n_in = [len(p["operands"]) for p in parts]
    n_out = [len(p["out_specs"]) for p in parts]
    n_scr = [len(p["scratch_shapes"]) for p in parts]

    def kern(first_ref, seq_ref, bwd_ref, *refs):
        ins, outs, scrs = refs[:sum(n_in)], refs[sum(n_in):sum(n_in) + sum(n_out)], refs[sum(n_in) + sum(n_out):]
        oi = oo = os_ = 0
        for p, a, b, c in zip(parts, n_in, n_out, n_scr):
            p["body"](first_ref, seq_ref, bwd_ref, *ins[oi:oi + a], *outs[oo:oo + b], *scrs[os_:os_ + c])
            oi, oo, os_ = oi + a, oo + b, os_ + c

    nb = plan["first"].shape[0]
    grid_spec = pltpu.PrefetchScalarGridSpec(
        num_scalar_prefetch=3,
        grid=(nb,),
        in_specs=[s for p in parts for s in p["in_specs"]],
        out_specs=[s for p in parts for s in p["out_specs"]],
        scratch_shapes=[s for p in parts for s in p["scratch_shapes"]],
    )
    outs = pl.pallas_call(
        kern,
        grid_spec=grid_spec,
        out_shape=[s for p in parts for s in p["out_shape"]],
        compiler_params=_cparams(("arbitrary",)),
        name=name,
    )(plan["first"], plan["seq"], plan["bwd"], *[o for p in parts for o in p["operands"]])
    res, oo = [], 0
    for b in n_out:
        res.append(outs[oo:oo + b])
        oo += b
    return res


def _gla_consts():
    n = G_CHUNK
    t = np.arange(n)[:, None]
    u = np.arange(n)[None, :]
    a_blocks, b_blocks, masks = [], [], [np.eye(n)]
    for lev in range(G_LEVELS):
        b = 1 << lev
        mid = (t // (2 * b)) * (2 * b) + b
        right = (t % (2 * b)) >= b
        a_blocks.append((right & (u >= mid) & (u <= t)).astype(np.float32))
        b_blocks.append((~right & (u > t) & (u <= mid - 1)).astype(np.float32))
        same_parent = (t // (2 * b)) == (u // (2 * b))
        masks.append((same_parent & right & ((u % (2 * b)) < b)).astype(np.float32))
    mcat_f = np.stack([np.tile(m, (1, G_HEADS)) for m in masks])
    flip = lambda m: m[::-1, ::-1]
    mcat_b = np.stack([np.tile(flip(m), (1, G_HEADS)) for m in masks])
    gath = []
    for d in (0, 1):
        g = np.zeros((G_SMALL_LEVELS * n, n), np.float32)
        for lev in range(G_SMALL_LEVELS):
            b = 1 << lev
            for row in range(n):
                g[lev * n + row, _gla_boundary(row, b, d)] = 1.0
        gath.append(g)
    tri = np.stack([(u <= t), (u >= t)]).astype(np.float32)
    return np.stack(gath), tri, np.stack([mcat_f, mcat_b])


G_SMALL_LEVELS = 3


def _gla_boundary(row, b, d):
    pstart = (row // (2 * b)) * (2 * b)
    return pstart + b - 1 if d == 0 else pstart + b


def _gla_kernel(first_ref, seq_ref, bwd_ref,
                zgf_ref, zsf_ref, zgb_ref, zsb_ref, s0_ref, w2_ref, b2_ref,
                tri_ref, gath_ref, mcat_ref, kmask_ref, vmask_ref, bd_ref,
                of_ref, ob_ref, sout_ref, ss_ref):
    r = pl.program_id(0)

    fresh = first_ref[r] == 1
    n = G_CHUNK
    nch = ROW_BLOCK // n
    kd = G_HEADS * G_DK
    vd = G_HEADS * G_DV
    ones_v = jnp.ones((n, vd), BF16)
    kmask = kmask_ref[...]
    vmask = vmask_ref[...]
    bd = bd_ref[...]
    nt_dims = (((1,), (1,)), ((), ()))

    def bd_k(kt):
        return jnp.concatenate([kt.astype(BF16)] * G_HEADS, axis=0) * kmask

    states = [jnp.where(fresh, s0_ref[0, d], ss_ref[d]) for d in (0, 1)]
    pending = []
    for ci in range(nch):
        for d in (0, 1):
            zg_ref, zs_ref, o_ref = ((zgf_ref, zsf_ref, of_ref), (zgb_ref, zsb_ref, ob_ref))[d]
            c = ci if d == 0 else nch - 1 - ci
            rows = slice(c * n, (c + 1) * n)
            q = zg_ref[rows, 0:kd] * (G_DK ** -0.5)
            k = zg_ref[rows, kd:2 * kd]
            v = zg_ref[rows, 2 * kd:2 * kd + vd].astype(BF16)
            x = jnp.dot(zs_ref[rows, :].astype(BF16), w2_ref[d], preferred_element_type=F32) + b2_ref[d]
            lg = _log_sigmoid(x) * (1.0 / G_TAU)
            bc = _dot_c_x(tri_ref[d], lg)
            r_small = _dot_c_x(gath_ref[d], bc)
            qb = q.astype(BF16)
            att = mcat_ref[d, 0] * lax.dot_general(qb, bd_k(k), nt_dims, preferred_element_type=F32)
            for lev in range(G_LEVELS):
                b = 1 << lev
                if lev < G_SMALL_LEVELS:
                    r_lev = r_small[lev * n:(lev + 1) * n]
                else:
                    r_lev = jnp.concatenate(
                        [jnp.broadcast_to(bc[_gla_boundary(p0, b, d):_gla_boundary(p0, b, d) + 1, :], (2 * b, kd))
                         for p0 in range(0, n, 2 * b)], axis=0)
                e_lev = jnp.exp(-jnp.abs(bc - r_lev))
                att = att + mcat_ref[d, lev + 1] * lax.dot_general(
                    (q * e_lev).astype(BF16), bd_k(k * e_lev), nt_dims, preferred_element_type=F32)
            v_bd = jnp.concatenate([v] * G_HEADS, axis=0) * vmask
            intra = jnp.dot(att.astype(BF16), v_bd, preferred_element_type=F32)
            inter = _dot(q * jnp.exp(bc), states[d])
            pending.append((o_ref, rows, inter + intra))
            last = n - 1 if d == 0 else 0
            decay = jnp.exp(_dot_xt_c(lg, ones_v))
            upd = lax.dot_general((k * jnp.exp(bc[last:last + 1, :] - bc)).astype(BF16), v,
                                  (((0,), (0,)), ((), ())), preferred_element_type=F32)
            states[d] = decay * states[d] + bd * upd
    for o_ref, rows, val in pending:
        o_ref[rows, :] = val
    for d in (0, 1):
        ss_ref[d] = states[d]
        sout_ref[0, d] = states[d]


def _gla_kernel_sm(first_ref, seq_ref, bwd_ref,
                   zgf_ref, zsf_ref, zgb_ref, zsb_ref, s0_ref, w2_ref, b2_ref,
                   tri_ref, gath_ref, mcat_ref, kmask_ref, vmask_ref, bd_ref,
                   of_ref, ob_ref, sout_ref, ss_ref):
    r = pl.program_id(0)
    fresh = first_ref[r] == 1
    n = G_CHUNK
    nch = ROW_BLOCK // n
    kd = G_HEADS * G_DK
    vd = G_HEADS * G_DV
    ones_v = jnp.ones((n, vd), BF16)
    kmask = kmask_ref[...]
    vmask = vmask_ref[...]
    bd = bd_ref[...]
    nt_dims = (((1,), (1,)), ((), ()))
    tn_dims = (((0,), (0,)), ((), ()))
    units = [(d, ci if d == 0 else nch - 1 - ci) for ci in range(nch) for d in (0, 1)]
    rows = lambda c: slice(c * n, (c + 1) * n)
    zg_refs, zs_refs, o_refs = (zgf_ref, zgb_ref), (zsf_ref, zsb_ref), (of_ref, ob_ref)

    def per_unit(f):
        return [f(d, c) for d, c in units]

    def bd_k(kt):
        return jnp.concatenate([kt] * G_HEADS, axis=0) * kmask

    q = per_unit(lambda d, c: zg_refs[d][rows(c), 0:kd] * (G_DK ** -0.5))
    k = per_unit(lambda d, c: zg_refs[d][rows(c), kd:2 * kd])
    v = per_unit(lambda d, c: zg_refs[d][rows(c), 2 * kd:2 * kd + vd].astype(BF16))
    x = per_unit(lambda d, c: jnp.dot(zs_refs[d][rows(c), :].astype(BF16), w2_ref[d],
                                      preferred_element_type=F32) + b2_ref[d])
    lg = [_log_sigmoid(xi) * (1.0 / G_TAU) for xi in x]
    lg_split = [_split(l) for l in lg]
    bc = [jnp.dot(tri_ref[d], hi, preferred_element_type=F32) + jnp.dot(tri_ref[d], lo, preferred_element_type=F32)
          for (d, _), (hi, lo) in zip(units, lg_split)]
    bc_split = [_split(b) for b in bc]
    r_small = [jnp.dot(gath_ref[d], hi, preferred_element_type=F32) + jnp.dot(gath_ref[d], lo, preferred_element_type=F32)
               for (d, _), (hi, lo) in zip(units, bc_split)]
    qb = [qi.astype(BF16) for qi in q]
    kbd = [bd_k(ki.astype(BF16)) for ki in k]
    att = [mcat_ref[d, 0] * lax.dot_general(qi, ki, nt_dims, preferred_element_type=F32)
           for (d, _), qi, ki in zip(units, qb, kbd)]
    for lev in range(G_LEVELS):
        b = 1 << lev
        if lev < G_SMALL_LEVELS:
            r_lev = [rs[lev * n:(lev + 1) * n] for rs in r_small]
        else:
            r_lev = [jnp.concatenate(
                [jnp.broadcast_to(bci[_gla_boundary(p0, b, d):_gla_boundary(p0, b, d) + 1, :], (2 * b, kd))
                 for p0 in range(0, n, 2 * b)], axis=0) for (d, _), bci in zip(units, bc)]
        e_lev = [jnp.exp(-jnp.abs(bci - ri)) for bci, ri in zip(bc, r_lev)]
        qt = [(qi * ei).astype(BF16) for qi, ei in zip(q, e_lev)]
        kt = [bd_k((ki * ei).astype(BF16)) for ki, ei in zip(k, e_lev)]
        p = [lax.dot_general(qi, ki, nt_dims, preferred_element_type=F32) for qi, ki in zip(qt, kt)]
        att = [ai + mcat_ref[d, lev + 1] * pi for (d, _), ai, pi in zip(units, att, p)]
    v_bd = [jnp.concatenate([vi] * G_HEADS, axis=0) * vmask for vi in v]
    intra = [jnp.dot(ai.astype(BF16), vi, preferred_element_type=F32) for ai, vi in zip(att, v_bd)]
    q_inc = [(qi * jnp.exp(bci)).astype(BF16) for qi, bci in zip(q, bc)]
    k_suf = [(ki * jnp.exp(bci[(n - 1 if d == 0 else 0):(n if d == 0 else 1), :] - bci)).astype(BF16)
             for (d, _), ki, bci in zip(units, k, bc)]
    decay = [jnp.exp(lax.dot_general(hi, ones_v, tn_dims, preferred_element_type=F32)
                     + lax.dot_general(lo, ones_v, tn_dims, preferred_element_type=F32))
             for hi, lo in lg_split]
    upd = [bd * lax.dot_general(ki, vi, tn_dims, preferred_element_type=F32) for ki, vi in zip(k_suf, v)]
    states = [jnp.where(fresh, s0_ref[0, d], ss_ref[d]) for d in (0, 1)]
    outs = []
    for i, (d, c) in enumerate(units):
        inter = jnp.dot(q_inc[i], states[d].astype(BF16), preferred_element_type=F32)
        outs.append(inter + intra[i])
        states[d] = decay[i] * states[d] + upd[i]
    for (d, c), o in zip(units, outs):
        o_refs[d][rows(c), :] = o
    for d in (0, 1):
        ss_ref[d] = states[d]
        sout_ref[0, d] = states[d]


def _gla_call(plan, zg, zs, s0, w2p, b2p, gath, tri, mcat, kmask, vmask, bd):
    nt = zg.shape[0]
    nb = nt // ROW_BLOCK
    nseq = s0.shape[0]
    kd = G_HEADS * G_DK
    vd = G_HEADS * G_DV
    n = G_CHUNK
    fwd = lambda r, first, seq, bwd: (r, 0)
    bwd_ = lambda r, first, seq, bwd: (bwd[r], 0)
    seq4 = lambda r, first, seq, bwd: (seq[r], 0, 0, 0)
    const3 = lambda r, first, seq, bwd: (0, 0, 0)
    const4 = lambda r, first, seq, bwd: (0, 0, 0, 0)
    return dict(
        body=_gla_kernel_sm,
        in_specs=[pl.BlockSpec((ROW_BLOCK, 512), fwd), pl.BlockSpec((ROW_BLOCK, LANES), fwd),
                  pl.BlockSpec((ROW_BLOCK, 512), bwd_), pl.BlockSpec((ROW_BLOCK, LANES), bwd_),
                  pl.BlockSpec((1, 2, kd, vd), seq4),
                  pl.BlockSpec((2, LANES, kd), const3), pl.BlockSpec((2, 1, kd), const3),
                  pl.BlockSpec((2, n, n), const3),
                  pl.BlockSpec((2, G_SMALL_LEVELS * n, n), const3),
                  pl.BlockSpec((2, G_LEVELS + 1, n, G_HEADS * n), const4),
                  pl.BlockSpec((G_HEADS * n, kd), lambda r, first, seq, bwd: (0, 0)),
                  pl.BlockSpec((G_HEADS * n, vd), lambda r, first, seq, bwd: (0, 0)),
                  pl.BlockSpec((kd, vd), lambda r, first, seq, bwd: (0, 0))],
        out_specs=[pl.BlockSpec((ROW_BLOCK, vd), fwd), pl.BlockSpec((ROW_BLOCK, vd), bwd_),
                   pl.BlockSpec((1, 2, kd, vd), seq4)],
        scratch_shapes=[pltpu.VMEM((2, kd, vd), F32)],
        out_shape=[jax.ShapeDtypeStruct((nt, vd), F32), jax.ShapeDtypeStruct((nt, vd), F32),
                   jax.ShapeDtypeStruct((nseq, 2, kd, vd), F32)],
        operands=[zg, zs, zg, zs, s0, w2p, b2p, tri, gath, mcat, kmask, vmask, bd])


def _rms(x, g, eps=1e-6):
    return x * lax.rsqrt(jnp.mean(x * x, axis=-1, keepdims=True) + eps) * g


def _mla_prep_kernel(zq_ref, zs_ref, cos_ref, sin_ref, kcos_ref, gq_ref, gkv_ref,
                     wuq_ref, wuqs_ref, wk_ref, wv_ref, q_ref, k_ref, v_ref, ckv_ref):
    cq = _rms(zq_ref[:, 0:A_DQ], gq_ref[...]).astype(BF16)
    qn = jnp.dot(cq, wuq_ref[...], preferred_element_type=F32)
    qs = jnp.dot(cq, wuqs_ref[...], preferred_element_type=F32)
    cos = cos_ref[...]
    sin = sin_ref[...]
    for h in range(A_HEADS):
        sl = slice(h * A_HPAD, (h + 1) * A_HPAD)
        q_ref[:, sl] = ((qn[:, sl] * cos + qs[:, sl] * sin) * A_SCALE).astype(BF16)
    ckv = _rms(zq_ref[:, A_DQ:A_DQ + A_DC], gkv_ref[...])
    ckv_ref[...] = ckv
    ckvb = ckv.astype(BF16)
    kn = jnp.dot(ckvb, wk_ref[...], preferred_element_type=F32)
    v_ref[...] = jnp.dot(ckvb, wv_ref[...], preferred_element_type=F32).astype(BF16)
    kr = zs_ref[:, 0:LANES] * kcos_ref[...] + zs_ref[:, LANES:2 * LANES] * sin
    for h in range(A_HEADS):
        sl = slice(h * A_HPAD, (h + 1) * A_HPAD)
        k_ref[:, sl] = (kn[:, sl] + kr).astype(BF16)


def _mla_prep_call(zq, zs, cos, sin, kcos, gq, gkv, wuq, wuqs, wk, wv):
    nt = zq.shape[0]
    tm = ROW_BLOCK
    hw = A_HEADS * A_HPAD
    row = lambda i: (i, 0)
    const = lambda i: (0, 0)
    return pl.pallas_call(
        _mla_prep_kernel,
        grid=(nt // tm,),
        in_specs=[pl.BlockSpec((tm, ZQ_W), row), pl.BlockSpec((tm, ZS_W), row),
                  pl.BlockSpec((tm, LANES), row), pl.BlockSpec((tm, LANES), row), pl.BlockSpec((tm, LANES), row),
                  pl.BlockSpec((1, A_DQ), const), pl.BlockSpec((1, A_DC), const),
                  pl.BlockSpec((A_DQ, hw), const), pl.BlockSpec((A_DQ, hw), const),
                  pl.BlockSpec((A_DC, hw), const), pl.BlockSpec((A_DC, hw), const)],
        out_specs=[pl.BlockSpec((tm, hw), row), pl.BlockSpec((tm, hw), row), pl.BlockSpec((tm, hw), row),
                   pl.BlockSpec((tm, A_DC), row)],
        out_shape=[jax.ShapeDtypeStruct((nt, hw), BF16), jax.ShapeDtypeStruct((nt, hw), BF16),
                   jax.ShapeDtypeStruct((nt, hw), BF16), jax.ShapeDtypeStruct((nt, A_DC), F32)],
        compiler_params=_cparams(("parallel",)),
        name="mla_prep",
    )(zq, zs, cos, sin, kcos, gq, gkv, wuq, wuqs, wk, wv)


def _cache_kv_kernel(ckv_ref, kr_ref, wk_ref, wv_ref, k_ref, v_ref):
    ckvb = ckv_ref[...].astype(BF16)
    kn = jnp.dot(ckvb, wk_ref[...], preferred_element_type=F32)
    v_ref[...] = jnp.dot(ckvb, wv_ref[...], preferred_element_type=F32).astype(BF16)
    kr = kr_ref[...]
    for h in range(A_HEADS):
        sl = slice(h * A_HPAD, (h + 1) * A_HPAD)
        k_ref[:, sl] = (kn[:, sl] + kr).astype(BF16)


def _cache_kv_call(ckv, krp, wk, wv):
    nt = ckv.shape[0]
    tm = ROW_BLOCK
    hw = A_HEADS * A_HPAD
    row = lambda i: (i, 0)
    const = lambda i: (0, 0)
    return pl.pallas_call(
        _cache_kv_kernel,
        grid=(nt // tm,),
        in_specs=[pl.BlockSpec((tm, A_DC), row), pl.BlockSpec((tm, LANES), row),
                  pl.BlockSpec((A_DC, hw), const), pl.BlockSpec((A_DC, hw), const)],
        out_specs=[pl.BlockSpec((tm, hw), row), pl.BlockSpec((tm, hw), row)],
        out_shape=[jax.ShapeDtypeStruct((nt, hw), BF16), jax.ShapeDtypeStruct((nt, hw), BF16)],
        compiler_params=_cparams(("parallel",)),
        name="mla_cache_kv",
    )(ckv, krp, wk, wv)


def _attn_kernel(*refs, n_seg):
    q_ref = refs[0]
    kv_refs = refs[1:1 + 2 * n_seg]
    o_ref = refs[1 + 2 * n_seg]
    for h in range(A_HEADS):
        sl = slice(h * A_HPAD, (h + 1) * A_HPAD)
        q = q_ref[:, sl]
        dn = (((1,), (1,)), ((), ()))
        s = [lax.dot_general(q, kv_refs[2 * j][:, sl], dn, preferred_element_type=F32) for j in range(n_seg)]
        m = s[0].max(axis=1, keepdims=True)
        for sj in s[1:]:
            m = jnp.maximum(m, sj.max(axis=1, keepdims=True))
        p = [jnp.exp(sj - m) for sj in s]
        l = p[0].sum(axis=1, keepdims=True)
        for pj in p[1:]:
            l = l + pj.sum(axis=1, keepdims=True)
        o = jnp.dot(p[0].astype(BF16), kv_refs[1][:, sl], preferred_element_type=F32)
        for j in range(1, n_seg):
            o = o + jnp.dot(p[j].astype(BF16), kv_refs[2 * j + 1][:, sl], preferred_element_type=F32)
        o_ref[:, sl] = (o / l).astype(BF16)


def _attn_ctx_call(q, k, v, n_rows):
    hw = A_HEADS * A_HPAD
    tm = ROW_BLOCK
    row = lambda i: (i, 0)
    return pl.pallas_call(
        functools.partial(_attn_kernel, n_seg=1),
        grid=(n_rows // tm,),
        in_specs=[pl.BlockSpec((tm, hw), row)] * 3,
        out_specs=pl.BlockSpec((tm, hw), row),
        out_shape=jax.ShapeDtypeStruct((n_rows, hw), BF16),
        compiler_params=_cparams(("parallel",)),
        name="mla_attn_ctx",
    )(q, k, v)


def _attn_dec_call(q, k, v, kc, vc, n_ctx_rows, n_dec, dec_len, past_len):
    hw = A_HEADS * A_HPAD
    tm = ROW_BLOCK
    nq = dec_len // tm
    assert n_ctx_rows % dec_len == 0
    qmap = lambda b, i: (n_ctx_rows // tm + b * nq + i, 0)
    own = lambda b, i: (n_ctx_rows // dec_len + b, 0)
    return pl.pallas_call(
        functools.partial(_attn_kernel, n_seg=2),
        grid=(n_dec, nq),
        in_specs=[pl.BlockSpec((tm, hw), qmap),
                  pl.BlockSpec((past_len, hw), lambda b, i: (b, 0)),
                  pl.BlockSpec((past_len, hw), lambda b, i: (b, 0)),
                  pl.BlockSpec((dec_len, hw), own),
                  pl.BlockSpec((dec_len, hw), own)],
        out_specs=pl.BlockSpec((tm, hw), lambda b, i: (b * nq + i, 0)),
        out_shape=jax.ShapeDtypeStruct((n_dec * dec_len, hw), BF16),
        compiler_params=_cparams(("parallel", "arbitrary")),
        name="mla_attn_dec",
    )(q, kc, vc, k, v)


def _layer_norm(y, g, b, eps=1e-5):
    yc = y - jnp.mean(y, axis=-1, keepdims=True)
    return yc * lax.rsqrt(jnp.mean(yc * yc, axis=-1, keepdims=True) + eps) * g + b


def _outproj_kernel(hf_ref, hb_ref, of_ref, ob_ref, a_ref, mo_ref, gg_ref, x_ref, mod_ref,
                    mng_ref, gng_ref, avg_ref, wm_ref, wg_ref, wa_ref, lg_ref, lb_ref, o_ref, *, alpha):
    d = D_MODEL
    avg = avg_ref[...]
    hm = hf_ref[...] + hb_ref[...]
    hc = hm - _dot_x_c(hm, avg)
    m_out = hc * lax.rsqrt(_dot_x_c(hc * hc, avg) + 1e-6) * mng_ref[...] * _sigmoid(mo_ref[...])
    og = of_ref[...] + ob_ref[...]
    gg = gg_ref[...]
    g_out = og * lax.rsqrt(_dot_x_c(og * og, avg) + 1e-6) * gng_ref[...] * (gg * _sigmoid(gg))
    mix = (_dot(m_out, wm_ref[...]) + _dot(g_out, wg_ref[...])
           + jnp.dot(a_ref[...], wa_ref[...], preferred_element_type=F32))
    mod = mod_ref[0]
    y = alpha * x_ref[...] + mod[:, 2 * d:3 * d] * mix
    o_ref[...] = _layer_norm(y, lg_ref[...], lb_ref[...])


def _outproj_call(hf, hb, of, ob, a, zm, zg, x, mods, mng, gng, avg, wm, wg, wa, lg, lb, mod_map, alpha):
    nt, d = x.shape
    tm = ROW_BLOCK
    hw = M_HEADS * M_DV
    row = lambda i: (i, 0)
    const = lambda i: (0, 0)
    gate_col = lambda i: (i, 2)
    return pl.pallas_call(
        functools.partial(_outproj_kernel, alpha=alpha),
        grid=(nt // tm,),
        in_specs=[pl.BlockSpec((tm, hw), row)] * 4
                 + [pl.BlockSpec((tm, A_HEADS * A_HPAD), row),
                    pl.BlockSpec((tm, hw), gate_col), pl.BlockSpec((tm, hw), gate_col),
                    pl.BlockSpec((tm, d), row),
                    pl.BlockSpec((1, 1, 6 * d), lambda i: (mod_map(i), 0, 0)),
                    pl.BlockSpec((1, hw), const), pl.BlockSpec((1, hw), const),
                    pl.BlockSpec((hw, hw), const),
                    pl.BlockSpec((hw, d), const), pl.BlockSpec((hw, d), const),
                    pl.BlockSpec((A_HEADS * A_HPAD, d), const),
                    pl.BlockSpec((1, d), const), pl.BlockSpec((1, d), const)],
        out_specs=pl.BlockSpec((tm, d), row),
        out_shape=jax.ShapeDtypeStruct((nt, d), F32),
        compiler_params=_cparams(("parallel",)),
        name="out_proj_ln",
    )(hf, hb, of, ob, a, zm, zg, x, mods, mng, gng, avg, wm, wg, wa, lg, lb)


def _ffn_kernel(x_ref, mod_ref, wg_ref, wu_ref, wd_ref, lg_ref, lb_ref, o_ref, *, n_chunks, alpha):
    d = D_MODEL
    mod = mod_ref[0]
    x = x_ref[...]
    hh = (x * (1.0 + mod[:, 4 * d:5 * d]) + mod[:, 3 * d:4 * d]).astype(BF16)
    f = wg_ref.shape[1]
    cw = f // n_chunks
    acc = jnp.zeros(x.shape, F32)
    for c in range(n_chunks):
        sl = slice(c * cw, (c + 1) * cw)
        g = jnp.dot(hh, wg_ref[:, sl], preferred_element_type=F32)
        u = jnp.dot(hh, wu_ref[:, sl], preferred_element_type=F32)
        act = (g * _sigmoid(g) * u).astype(BF16)
        acc = acc + jnp.dot(act, wd_ref[sl, :], preferred_element_type=F32)
    y = alpha * x + mod[:, 5 * d:6 * d] * acc
    o_ref[...] = _layer_norm(y, lg_ref[...], lb_ref[...])


def _ffn_call(x, mods, wg, wu, wd, lg, lb, mod_map, alpha):
    nt, d = x.shape
    f = wg.shape[1]
    tm = ROW_BLOCK
    row = lambda i: (i, 0)
    const = lambda i: (0, 0)
    resident = dict(pipeline_mode=pl.Buffered(1))
    return pl.pallas_call(
        functools.partial(_ffn_kernel, n_chunks=2, alpha=alpha),
        grid=(nt // tm,),
        in_specs=[pl.BlockSpec((tm, d), row),
                  pl.BlockSpec((1, 1, 6 * d), lambda i: (mod_map(i), 0, 0)),
                  pl.BlockSpec((d, f), const, **resident), pl.BlockSpec((d, f), const, **resident),
                  pl.BlockSpec((f, d), const, **resident),
                  pl.BlockSpec((1, d), const), pl.BlockSpec((1, d), const)],
        out_specs=pl.BlockSpec((tm, d), row),
        out_shape=jax.ShapeDtypeStruct((nt, d), F32),
        compiler_params=_cparams(("parallel",)),
        name="ffn_ln",
    )(x, mods, wg, wu, wd, lg, lb)


def _block_diag_heads(t):
    h = t.shape[-3]
    eye = jnp.eye(h, dtype=t.dtype)
    out = jnp.einsum('...hkv,hg->...hkgv', t, eye)
    return out.reshape(t.shape[:-3] + (h * t.shape[-2], h * t.shape[-1]))


def _diag_blocks(t, h, dk, dv):
    t = t.reshape(t.shape[:-2] + (h, dk, h, dv))
    return jnp.stack([t[..., i, :, i, :] for i in range(h)], axis=-3)


def _rope_tables(n_ctx_rows, n_dec, dec_len):
    half = A_DROPE // 4
    inv = ROPE_BASE ** (-jnp.arange(half, dtype=F32) / half)
    pos = jnp.arange(dec_len, dtype=jnp.int32)
    rows = (pos // GRID_W).astype(F32)[:, None] * inv[None, :]
    cols = (pos % GRID_W).astype(F32)[:, None] * inv[None, :]
    cr, sr, cc, sc = jnp.cos(rows), jnp.sin(rows), jnp.cos(cols), jnp.sin(cols)
    cos32 = jnp.concatenate([cr, cr, cc, cc], axis=1)
    sin32 = jnp.concatenate([-sr, sr, -sc, sc], axis=1)
    ones_lo = jnp.ones((dec_len, KR_LO), F32)
    ones_hi = jnp.ones((dec_len, A_HPAD - KR_HI), F32)
    cos = jnp.concatenate([ones_lo, cos32, ones_hi], axis=1)
    sin = jnp.concatenate([0 * ones_lo, sin32, 0 * ones_hi], axis=1)
    kcos = jnp.concatenate([0 * ones_lo, cos32, 0 * ones_hi], axis=1)
    lane = jnp.arange(A_HPAD)
    rope_lane = ((lane >= KR_LO) & (lane < KR_HI)).astype(F32)[None, :]
    ctx_cos = jnp.ones((n_ctx_rows, A_HPAD), F32)
    ctx_sin = jnp.zeros((n_ctx_rows, A_HPAD), F32)
    ctx_kcos = jnp.broadcast_to(rope_lane, (n_ctx_rows, A_HPAD))
    tile = lambda t: jnp.tile(t, (n_dec, 1))
    return (jnp.concatenate([ctx_cos, tile(cos)]), jnp.concatenate([ctx_sin, tile(sin)]),
            jnp.concatenate([ctx_kcos, tile(kcos)]))


def _rope_partner(w):
    e = A_DROPE // 4
    return jnp.concatenate([w[..., e:2 * e], w[..., 0:e], w[..., 3 * e:4 * e], w[..., 2 * e:3 * e]], axis=-1)


def kernel(x_prompt, x_sample, state_mlstm_C, state_mlstm_n, state_mlstm_m, state_gla_S, cache_mla_ckv, cache_mla_krope, c, c_ctx, w_ada, b_ada, w_in, m_gate_b, m_norm_g, g_w2, g_b2, g_norm_g, a_q_norm_g, a_kv_norm_g, a_w_uq, a_w_ukv, w_out, ln1_g, ln1_b, w_ffn_gate, w_ffn_up, w_ffn_down, ln2_g, ln2_b):
    n_ctx, ctx_len, d = x_prompt.shape
    n_dec, dec_len, _ = x_sample.shape
    depth = w_in.shape[0]
    past_len = cache_mla_ckv.shape[2]
    assert d == D_MODEL and ctx_len == ROW_BLOCK and dec_len % ROW_BLOCK == 0 and past_len % 8 == 0
    assert n_dec + 1 <= 8
    n_ctx_rows = n_ctx * ctx_len
    n_dec_rows = n_dec * dec_len
    nt = n_ctx_rows + n_dec_rows
    nb_ctx = n_ctx_rows // ROW_BLOCK
    nb_seq = dec_len // ROW_BLOCK
    nseq = n_ctx + n_dec
    alpha = (2.0 * depth) ** 0.25

    blocks = np.arange(nt // ROW_BLOCK)
    dec_b = np.maximum(blocks - nb_ctx, 0)
    is_ctx = blocks < nb_ctx
    plan = {
        "first": jnp.asarray(np.where(is_ctx, 1, (dec_b % nb_seq) == 0).astype(np.int32)),
        "seq": jnp.asarray(np.where(is_ctx, blocks, n_ctx + dec_b // nb_seq).astype(np.int32)),
        "bwd": jnp.asarray(np.where(is_ctx, blocks,
                                    nb_ctx + (dec_b // nb_seq) * nb_seq + nb_seq - 1 - dec_b % nb_seq).astype(np.int32)),
    }

    def mod_map_for(layer):
        def mod_map(i):
            return layer * 8 + jnp.where(i < nb_ctx, 0, 1 + (i - nb_ctx) // nb_seq)
        return mod_map

    cond8 = jnp.zeros((8, d), F32).at[0].set(c_ctx).at[1:1 + n_dec].set(c)
    mods = _ada_call(cond8, w_ada, b_ada).reshape(depth * 8, 1, 6 * d)

    pts = [int(p) for p in np.cumsum(IN_SPLITS)[:-1]]
    (wmq, wmk, wmv, wmo, wmg, wgq, wgk, wgv, wgg, wga, wacq, wackv, wakr) = jnp.split(w_in, pts, axis=-1)
    zpad = lambda n: jnp.zeros((depth, d, n), F32)
    s1 = jnp.concatenate([wmg, wga, zpad(KR_LO - 48), wakr, zpad(LANES - KR_HI)], axis=-1)
    s2 = jnp.concatenate([zpad(KR_LO), _rope_partner(wakr), zpad(LANES - KR_HI)], axis=-1)
    w_in_p = jnp.concatenate([wmq, wmk, wmv, wmo, wgq, wgk, wgv, wgg, wacq, wackv, s1, s2], axis=-1).astype(BF16)
    w_gt = jnp.swapaxes(wmg, 1, 2).astype(BF16)
    gate_b = jnp.zeros((depth, 1, LANES), F32).at[:, 0, 0:16].set(m_gate_b)
    gate_bt = m_gate_b.reshape(depth, 16, 1)
    w2p = jnp.zeros((depth, 2, LANES, G_HEADS * G_DK), F32)
    w2p = w2p.at[:, 0, 16:16 + G_RANK].set(g_w2[:, 0]).at[:, 1, 16 + G_RANK:16 + 2 * G_RANK].set(g_w2[:, 1]).astype(BF16)
    b2p = g_b2.reshape(depth, 2, 1, G_HEADS * G_DK)
    uq = a_w_uq.reshape(depth, A_DQ, A_HEADS, A_DNOPE + A_DROPE)
    hp = lambda n: jnp.zeros((depth, uq.shape[1], A_HEADS, n), F32)
    wuq = jnp.concatenate([uq, hp(A_HPAD - KR_HI)], axis=-1).reshape(depth, A_DQ, -1).astype(BF16)
    wuqs = jnp.concatenate([hp(KR_LO), _rope_partner(uq[..., A_DNOPE:]), hp(A_HPAD - KR_HI)],
                           axis=-1).reshape(depth, A_DQ, -1).astype(BF16)
    ukv = a_w_ukv.reshape(depth, A_DC, A_HEADS, A_DNOPE + A_DV)
    kp = jnp.zeros((depth, A_DC, A_HEADS, A_HPAD - A_DNOPE), F32)
    wk = jnp.concatenate([ukv[..., :A_DNOPE], kp], axis=-1).reshape(depth, A_DC, -1).astype(BF16)
    wv = jnp.concatenate([ukv[..., A_DNOPE:], kp], axis=-1).reshape(depth, A_DC, -1).astype(BF16)
    mw, gw = M_HEADS * M_DV, G_HEADS * G_DV
    wo_m = w_out[:, 0:mw].astype(BF16)
    wo_g = w_out[:, mw:mw + gw].astype(BF16)
    wo_a = w_out[:, mw + gw:].reshape(depth, A_HEADS, A_DV, d)
    wo_a = jnp.concatenate([wo_a, jnp.zeros((depth, A_HEADS, A_HPAD - A_DV, d), F32)], axis=2)
    wo_a = wo_a.reshape(depth, A_HEADS * A_HPAD, d).astype(BF16)
    wfg, wfu, wfd = w_ffn_gate.astype(BF16), w_ffn_up.astype(BF16), w_ffn_down.astype(BF16)

    ti = np.arange(M_CHUNK)
    trif = jnp.asarray((ti[None, :] <= ti[:, None]).astype(np.float32), BF16)
    trib = jnp.asarray((ti[None, :] >= ti[:, None]).astype(np.float32), BF16)
    mconst = {k_: jnp.asarray(v_, F32 if k_ in ("neg", "bd2") else BF16) for k_, v_ in _mlstm_consts().items()}
    head_lanes = jnp.asarray((np.arange(M_HEADS)[:, None] == (np.arange(MS_W) // M_DV)[None, :]).astype(np.float32))
    gath_np, gtri_np, mcat_np = _gla_consts()
    gath = jnp.asarray(gath_np, BF16)
    gtri = jnp.asarray(gtri_np, BF16)
    gbd = mconst["bd2"][:, 0:MS_W]
    mcat = jnp.asarray(mcat_np, F32)
    hi = np.arange(mw) // M_DV
    avg = jnp.asarray((hi[:, None] == hi[None, :]).astype(np.float32) / M_DV, BF16)
    cos, sin, kcos = _rope_tables(n_ctx_rows, n_dec, dec_len)

    def init_state(block, n_tail):
        z = jnp.zeros((n_ctx,) + block.shape[1:], F32)
        return jnp.concatenate([z, block], axis=0)

    krp_all = jnp.zeros(cache_mla_krope.shape[:-1] + (LANES,), F32).at[..., KR_LO:KR_HI].set(cache_mla_krope)

    x = jnp.concatenate([x_prompt.reshape(n_ctx_rows, d), x_sample.reshape(n_dec_rows, d)], axis=0)
    new_c, new_n, new_m, new_s, new_ckv, new_kr = [], [], [], [], [], []
    for l in range(depth):
        mod_map = mod_map_for(l)
        zm, zg, zq, zs, zt = _inproj_call(x, mods, w_in_p[l], w_gt[l], mod_map)

        n_bd = jnp.einsum('bdhk,hg->bdhkg', state_mlstm_n[:, l], head_lanes)
        cn0 = init_state(jnp.concatenate([_block_diag_heads(state_mlstm_C[:, l]),
                                          n_bd.reshape(n_dec, 2, M_HEADS * M_DK, MS_W)], axis=-1), 0)
        m0 = init_state(jnp.repeat(state_mlstm_m[:, l], M_DV, axis=-1)[:, :, None, :], 0)
        s0 = init_state(_block_diag_heads(state_gla_S[:, l]), 0)
        (hf, hb, cn_fin, m_fin), (of, ob, s_fin) = _scan_call(
            plan,
            [_mlstm2_call(plan, zm, zs, zt, cn0, m0, gate_b[l], gate_bt[l], mconst),
             _gla_call(plan, zg, zs, s0, w2p[l], b2p[l], gath, gtri, mcat,
                       mconst["kmask"], mconst["vmask"], gbd)],
            "mixer_scan")

        q, k, v, ckv = _mla_prep_call(zq, zs, cos, sin, kcos, a_q_norm_g[l][None, :], a_kv_norm_g[l][None, :],
                                      wuq[l], wuqs[l], wk[l], wv[l])
        kc, vc = _cache_kv_call(cache_mla_ckv[:, l].reshape(n_dec * past_len, A_DC),
                                krp_all[:, l].reshape(n_dec * past_len, LANES), wk[l], wv[l])
        a_ctx = _attn_ctx_call(q, k, v, n_ctx_rows)
        a_dec = _attn_dec_call(q, k, v, kc, vc, n_ctx_rows, n_dec, dec_len, past_len)
        a = jnp.concatenate([a_ctx, a_dec], axis=0)

        x = _outproj_call(hf, hb, of, ob, a, zm, zg, x, mods, m_norm_g[l][None, :], g_norm_g[l][None, :], avg,
                          wo_m[l], wo_g[l], wo_a[l], ln1_g[l][None, :], ln1_b[l][None, :], mod_map, alpha)
        x = _ffn_call(x, mods, wfg[l], wfu[l], wfd[l], ln2_g[l][None, :], ln2_b[l][None, :], mod_map, alpha)

        new_c.append(_diag_blocks(cn_fin[:n_ctx, :, :, 0:MS_W], M_HEADS, M_DK, M_DV))
        new_n.append(_diag_blocks(cn_fin[:n_ctx, :, :, MS_W:], M_HEADS, M_DK, M_DV)[..., 0])
        new_m.append(m_fin[:n_ctx, :, 0, ::M_DV])
        new_s.append(_diag_blocks(s_fin[:n_ctx], G_HEADS, G_DK, G_DV))
        new_ckv.append(ckv[:n_ctx_rows].reshape(n_ctx, ctx_len, A_DC))
        new_kr.append(zs[:n_ctx_rows, KR_LO:KR_HI].reshape(n_ctx, ctx_len, A_DROPE))

    y_prompt = x[:n_ctx_rows].reshape(n_ctx, ctx_len, d)
    y_sample = x[n_ctx_rows:].reshape(n_dec, dec_len, d)
    st = lambda xs: jnp.stack(xs, axis=1)
    return (y_prompt, y_sample, st(new_c), st(new_n), st(new_m), st(new_s), st(new_ckv), st(new_kr))
```

```python
import functools

import numpy as np
import jax
import jax.numpy as jnp
from jax import lax
from jax.experimental import pallas as pl
from jax.experimental.pallas import tpu as pltpu

F32 = jnp.float32
BF16 = jnp.bfloat16

D_MODEL = 1024
GRID_W = 64
M_HEADS, M_DK, M_DV = 4, 32, 64
G_HEADS, G_DK, G_DV = 4, 32, 64
G_RANK = 16
G_TAU = 16.0
A_HEADS, A_DNOPE, A_DROPE, A_DV = 8, 64, 32, 64
A_DQ, A_DC = 256, 128
A_SCALE = (A_DNOPE + A_DROPE) ** -0.5
ROPE_BASE = 10000.0
IN_SPLITS = (M_HEADS * M_DK, M_HEADS * M_DK, M_HEADS * M_DV, M_HEADS * M_DV, 4 * M_HEADS,
             G_HEADS * G_DK, G_HEADS * G_DK, G_HEADS * G_DV, G_HEADS * G_DV, 2 * G_RANK,
             A_DQ, A_DC, A_DROPE)

LANES = 128
MXU_TILE = 256
ROW_BLOCK = 256
M_CHUNK = 128
G_CHUNK = 64
G_LEVELS = 6
A_HPAD = 128
A_KEY_TILE = 512
FFN_ROWS = 512
FFN_CHUNKS = 2
OUTPROJ_ROW_GROUPS = 2
VMEM_LIMIT = 56 * 1024 * 1024

ZM_W = 768
ZG_W = 768
ZQ_W = 384
ZS_W = 256
Z_W = ZM_W + ZG_W + ZQ_W + ZS_W
KR_LO, KR_HI = A_DNOPE, A_DNOPE + A_DROPE


def _cparams(sem):
    return pltpu.CompilerParams(dimension_semantics=sem, vmem_limit_bytes=VMEM_LIMIT)


def _dot(a, b):
    return jnp.dot(a.astype(BF16), b.astype(BF16), preferred_element_type=F32)


def _dot_nt(a, b):
    return lax.dot_general(a.astype(BF16), b.astype(BF16), (((1,), (1,)), ((), ())),
                           preferred_element_type=F32)


def _dot_tn(a, b):
    return lax.dot_general(a.astype(BF16), b.astype(BF16), (((0,), (0,)), ((), ())),
                           preferred_element_type=F32)


def _split(x):
    hi = x.astype(BF16)
    lo = (x - hi.astype(F32)).astype(BF16)
    return hi, lo


def _dot_c_x(c, x):
    hi, lo = _split(x)
    return jnp.dot(c, hi, preferred_element_type=F32) + jnp.dot(c, lo, preferred_element_type=F32)


def _dot_x_c(x, c):
    hi, lo = _split(x)
    return jnp.dot(hi, c, preferred_element_type=F32) + jnp.dot(lo, c, preferred_element_type=F32)


def _dot_x_ct(x, c):
    hi, lo = _split(x)
    dn = (((1,), (1,)), ((), ()))
    return (lax.dot_general(hi, c, dn, preferred_element_type=F32)
            + lax.dot_general(lo, c, dn, preferred_element_type=F32))


def _dot_xt_c(x, c):
    hi, lo = _split(x)
    dn = (((0,), (0,)), ((), ()))
    return (lax.dot_general(hi, c, dn, preferred_element_type=F32)
            + lax.dot_general(lo, c, dn, preferred_element_type=F32))


def _log_sigmoid(x):
    return jnp.minimum(x, 0.0) - jnp.log1p(jnp.exp(-jnp.abs(x)))


def _sigmoid(x):
    return 1.0 / (1.0 + jnp.exp(-x))


def _ada_kernel(c_ref, w_ref, b_ref, o_ref):
    c = c_ref[...]
    s = c * _sigmoid(c)
    o_ref[0] = _dot(s, w_ref[0]) + b_ref[0]


def _ada_call(cond8, w_ada, b_ada):
    depth, d, n = w_ada.shape
    tn = 1536
    return pl.pallas_call(
        _ada_kernel,
        grid=(depth, n // tn),
        in_specs=[pl.BlockSpec((8, d), lambda l, j: (0, 0)),
                  pl.BlockSpec((1, d, tn), lambda l, j: (l, 0, j)),
                  pl.BlockSpec((1, 1, tn), lambda l, j: (l, 0, j))],
        out_specs=pl.BlockSpec((1, 8, tn), lambda l, j: (l, 0, j)),
        out_shape=jax.ShapeDtypeStruct((depth, 8, n), F32),
        compiler_params=_cparams(("arbitrary", "arbitrary")),
        name="ada_mod",
    )(cond8, w_ada, b_ada.reshape(depth, 1, n))


def _inproj_kernel(x_ref, mod_ref, w_ref, wgt_ref, zm_ref, zg_ref, zq_ref, zs_ref, zt_ref):
    d = D_MODEL
    mod = mod_ref[0]
    h = (x_ref[...] * (1.0 + mod[:, d:2 * d]) + mod[:, 0:d]).astype(BF16)
    z = jnp.dot(h, w_ref[...], preferred_element_type=F32)
    zm_ref[...] = z[:, 0:ZM_W]
    zg_ref[...] = z[:, ZM_W:ZM_W + ZG_W]
    zq_ref[...] = z[:, ZM_W + ZG_W:ZM_W + ZG_W + ZQ_W]
    zs_ref[...] = z[:, ZM_W + ZG_W + ZQ_W:Z_W]
    zt_ref[...] = lax.dot_general(wgt_ref[...], h, (((1,), (1,)), ((), ())), preferred_element_type=F32)


def _inproj_call(x, mods, w_in_p, w_gt, mod_map):
    nt, d = x.shape
    tm = ROW_BLOCK
    row = lambda i: (i, 0)
    const = lambda i: (0, 0)
    return pl.pallas_call(
        _inproj_kernel,
        grid=(nt // tm,),
        in_specs=[pl.BlockSpec((tm, d), row),
                  pl.BlockSpec((1, 1, 6 * d), lambda i: (mod_map(i), 0, 0)),
                  pl.BlockSpec((d, Z_W), const),
                  pl.BlockSpec((16, d), const)],
        out_specs=[pl.BlockSpec((tm, ZM_W), row), pl.BlockSpec((tm, ZG_W), row),
                   pl.BlockSpec((tm, ZQ_W), row), pl.BlockSpec((tm, ZS_W), row),
                   pl.BlockSpec((16, tm), lambda i: (0, i))],
        out_shape=[jax.ShapeDtypeStruct((nt, ZM_W), F32), jax.ShapeDtypeStruct((nt, ZG_W), F32),
                   jax.ShapeDtypeStruct((nt, ZQ_W), F32), jax.ShapeDtypeStruct((nt, ZS_W), F32),
                   jax.ShapeDtypeStruct((16, nt), F32)],
        compiler_params=_cparams(("parallel",)),
        name="in_proj",
    )(x, mods, w_in_p, w_gt)


def _mlstm_chunk(d, q, k, v, graw, graw_t, tri):
    lm = q.shape[0]
    io, fo = 8 * d, 8 * d + 4
    last = lm - 1 if d == 0 else 0
    lf_c = _log_sigmoid(graw)
    lf_t = _log_sigmoid(graw_t)
    bcol = _dot_c_x(tri, lf_c)
    brow = _dot_x_ct(lf_t, tri)
    ri = lax.broadcasted_iota(jnp.int32, (lm, lm), 0)
    ci = lax.broadcasted_iota(jnp.int32, (lm, lm), 1)
    causal = (ci <= ri) if d == 0 else (ci >= ri)
    head32 = lax.broadcasted_iota(jnp.int32, (lm, LANES), 1) // M_DK
    head64 = lax.broadcasted_iota(jnp.int32, (lm, M_HEADS * M_DV), 1) // M_DV
    rowhead = lax.broadcasted_iota(jnp.int32, (M_HEADS * M_DK, 1), 0) // M_DK
    lane1 = lax.broadcasted_iota(jnp.int32, (1, LANES), 1)

    qb = q.astype(BF16)
    vb = v.astype(BF16)

    num = jnp.zeros((lm, M_HEADS * M_DV), F32)
    w_exp = jnp.zeros((lm, LANES), F32)
    per_head = []
    for h in range(M_HEADS):
        col_b = bcol[:, fo + h:fo + h + 1]
        row_g = graw_t[io + h:io + h + 1, :] - brow[fo + h:fo + h + 1, :]
        dmat = jnp.where(causal, col_b + row_g, -jnp.inf)
        mintra = jnp.max(dmat, axis=1, keepdims=True)
        qk = lax.dot_general(qb, jnp.where(head32 == h, k, 0.0).astype(BF16),
                             (((1,), (1,)), ((), ())), preferred_element_type=F32)
        p = qk * jnp.exp(dmat - mintra)
        den_a = jnp.sum(p, axis=1, keepdims=True)
        num = num + jnp.dot(p.astype(BF16), jnp.where(head64 == h, v, 0.0).astype(BF16),
                            preferred_element_type=F32)
        bl = col_b[last:last + 1, :]
        g = bl - col_b + graw[:, io + h:io + h + 1]
        gmax = jnp.max(g, axis=0, keepdims=True)
        w_exp = jnp.where(head32 == h, jnp.exp(g - gmax), w_exp)
        per_head.append((col_b, mintra, den_a, bl, gmax))
    kw = (k * w_exp).astype(BF16)
    dn = (((0,), (0,)), ((), ()))
    upd_c = lax.dot_general(kw, vb, dn, preferred_element_type=F32)
    upd_n = lax.dot_general(kw, jnp.ones((lm, LANES), BF16), dn, preferred_element_type=F32)
    rc = lax.broadcasted_iota(jnp.int32, (M_HEADS * M_DK, M_HEADS * M_DV), 0) // M_DK
    cc = lax.broadcasted_iota(jnp.int32, (M_HEADS * M_DK, M_HEADS * M_DV), 1) // M_DV
    rn = lax.broadcasted_iota(jnp.int32, (M_HEADS * M_DK, LANES), 0) // M_DK
    cn = lax.broadcasted_iota(jnp.int32, (M_HEADS * M_DK, LANES), 1)
    upd_c = jnp.where(rc == cc, upd_c, 0.0)
    upd_n = jnp.where(rn == cn, upd_n, 0.0)

    def finish(c_st, n_st, m_st):
        q_c = jnp.dot(qb, c_st.astype(BF16), preferred_element_type=F32)
        q_n = jnp.dot(qb, n_st.astype(BF16), preferred_element_type=F32)
        f1_exp = jnp.zeros((lm, M_HEADS * M_DV), F32)
        f2_exp = jnp.zeros((lm, M_HEADS * M_DV), F32)
        r_exp = jnp.zeros((lm, M_HEADS * M_DV), F32)
        a_row = jnp.zeros((M_HEADS * M_DK, 1), F32)
        b_row = jnp.zeros((M_HEADS * M_DK, 1), F32)
        m_row = jnp.zeros((1, LANES), F32)
        for h in range(M_HEADS):
            col_b, mintra, den_a, bl, gmax = per_head[h]
            m_h = m_st[:, h:h + 1]
            a = col_b + m_h
            mloc = jnp.maximum(a, mintra)
            f1 = jnp.exp(mintra - mloc)
            f2 = jnp.exp(a - mloc)
            den = f1 * den_a + f2 * q_n[:, h:h + 1]
            rden = 1.0 / jnp.maximum(jnp.abs(den), jnp.exp(-mloc))
            f1_exp = jnp.where(head64 == h, f1, f1_exp)
            f2_exp = jnp.where(head64 == h, f2, f2_exp)
            r_exp = jnp.where(head64 == h, rden, r_exp)
            m_new = jnp.maximum(bl + m_h, gmax)
            a_row = jnp.where(rowhead == h, jnp.exp(bl + m_h - m_new), a_row)
            b_row = jnp.where(rowhead == h, jnp.exp(gmax - m_new), b_row)
            m_row = jnp.where(lane1 == h, m_new, m_row)
        h_out = (f1_exp * num + f2_exp * q_c) * r_exp
        return h_out, a_row * c_st + b_row * upd_c, a_row * n_st + b_row * upd_n, m_row

    return finish


def _mlstm_kernel(first_ref, seq_ref, bwd_ref,
                  zmf_ref, zsf_ref, ztf_ref, zmb_ref, zsb_ref, ztb_ref,
                  c0_ref, n0_ref, m0_ref, gb_ref, gbt_ref, trif_ref, trib_ref,
                  hf_ref, hb_ref, cout_ref, nout_ref, mout_ref,
                  cs_ref, ns_ref, ms_ref):
    r = pl.program_id(0)

    @pl.when(first_ref[r] == 1)
    def _():
        cs_ref[...] = c0_ref[0]
        ns_ref[...] = n0_ref[0]
        ms_ref[...] = m0_ref[0]

    nch = ROW_BLOCK // M_CHUNK
    qk_w = M_HEADS * M_DK
    for d in (0, 1):
        zm_ref, zs_ref, zt_ref, h_ref = ((zmf_ref, zsf_ref, ztf_ref, hf_ref),
                                         (zmb_ref, zsb_ref, ztb_ref, hb_ref))[d]
        tri = (trif_ref, trib_ref)[d][...]
        order = range(nch) if d == 0 else range(nch - 1, -1, -1)
        finishers = []
        for c in order:
            rows = slice(c * M_CHUNK, (c + 1) * M_CHUNK)
            q = zm_ref[rows, 0:qk_w]
            k = zm_ref[rows, qk_w:2 * qk_w] * (M_DK ** -0.5)
            v = zm_ref[rows, 2 * qk_w:2 * qk_w + M_HEADS * M_DV]
            graw = zs_ref[rows, :] + gb_ref[...]
            graw_t = zt_ref[:, rows] + gbt_ref[...]
            finishers.append((rows, _mlstm_chunk(d, q, k, v, graw, graw_t, tri)))
        state = (cs_ref[d], ns_ref[d], ms_ref[d])
        for rows, finish in finishers:
            h_out, *state = finish(*state)
            h_ref[rows, :] = h_out
        cs_ref[d], ns_ref[d], ms_ref[d] = state
        cout_ref[0, d], nout_ref[0, d], mout_ref[0, d] = state


def _mlstm_call(plan, zm, zs, zt, c0, n0, m0, gb, gbt, trif, trib):
    nt = zm.shape[0]
    nb = nt // ROW_BLOCK
    nseq = c0.shape[0]
    hw = M_HEADS * M_DV
    kd = M_HEADS * M_DK
    fwd = lambda r, first, seq, bwd: (r, 0)
    bwd_ = lambda r, first, seq, bwd: (bwd[r], 0)
    fwd_t = lambda r, first, seq, bwd: (0, r)
    bwd_t = lambda r, first, seq, bwd: (0, bwd[r])
    seq4 = lambda r, first, seq, bwd: (seq[r], 0, 0, 0)
    const = lambda r, first, seq, bwd: (0, 0)
    grid_spec = pltpu.PrefetchScalarGridSpec(
        num_scalar_prefetch=3,
        grid=(nb,),
        in_specs=[pl.BlockSpec((ROW_BLOCK, 512), fwd), pl.BlockSpec((ROW_BLOCK, LANES), fwd),
                  pl.BlockSpec((16, ROW_BLOCK), fwd_t),
                  pl.BlockSpec((ROW_BLOCK, 512), bwd_), pl.BlockSpec((ROW_BLOCK, LANES), bwd_),
                  pl.BlockSpec((16, ROW_BLOCK), bwd_t),
                  pl.BlockSpec((1, 2, kd, hw), seq4), pl.BlockSpec((1, 2, kd, LANES), seq4),
                  pl.BlockSpec((1, 2, 1, LANES), seq4),
                  pl.BlockSpec((1, LANES), const), pl.BlockSpec((16, 1), const),
                  pl.BlockSpec((M_CHUNK, M_CHUNK), const), pl.BlockSpec((M_CHUNK, M_CHUNK), const)],
        out_specs=[pl.BlockSpec((ROW_BLOCK, hw), fwd), pl.BlockSpec((ROW_BLOCK, hw), bwd_),
                   pl.BlockSpec((1, 2, kd, hw), seq4), pl.BlockSpec((1, 2, kd, LANES), seq4),
                   pl.BlockSpec((1, 2, 1, LANES), seq4)],
        scratch_shapes=[pltpu.VMEM((2, kd, hw), F32), pltpu.VMEM((2, kd, LANES), F32),
                        pltpu.VMEM((2, 1, LANES), F32)],
    )
    return pl.pallas_call(
        _mlstm_kernel,
        grid_spec=grid_spec,
        out_shape=[jax.ShapeDtypeStruct((nt, hw), F32), jax.ShapeDtypeStruct((nt, hw), F32),
                   jax.ShapeDtypeStruct((nseq, 2, kd, hw), F32),
                   jax.ShapeDtypeStruct((nseq, 2, kd, LANES), F32),
                   jax.ShapeDtypeStruct((nseq, 2, 1, LANES), F32)],
        compiler_params=_cparams(("arbitrary",)),
        name="mlstm_scan",
    )(plan["first"], plan["seq"], plan["bwd"], zm, zs, zt, zm, zs, zt, c0, n0, m0, gb, gbt, trif, trib)


MS_CHUNK = 64
MS_W = M_HEADS * M_DV


def _mlstm_consts():
    n = MS_CHUNK
    t = np.arange(n)[:, None]
    s = np.arange(n)[None, :]
    tri = [(s <= t), (s >= t)]
    head_of = np.arange(MS_W) // M_DV
    kmask = (head_of[:, None] == (np.arange(M_HEADS * M_DK) // M_DK)[None, :]).astype(np.float32)
    vmask = (head_of[:, None] == head_of[None, :]).astype(np.float32)
    bd = ((np.arange(M_HEADS * M_DK) // M_DK)[:, None] == head_of[None, :]).astype(np.float32)
    sel, tritile, neg = [], [], []
    for d in (0, 1):
        io, fo = 8 * d, 8 * d + 4
        m = np.zeros((LANES, 3 * MS_W), np.float32)
        for h in range(M_HEADS):
            m[fo + h, h * M_DV:(h + 1) * M_DV] = 1.0
            m[io + h, MS_W + h * M_DV:MS_W + (h + 1) * M_DV] = 1.0
            m[fo + h, 2 * MS_W + h * M_DK:2 * MS_W + (h + 1) * M_DK] = 1.0
            m[io + h, 2 * MS_W + LANES + h * M_DK:2 * MS_W + LANES + (h + 1) * M_DK] = 1.0
        sel.append(m)
        tritile.append(np.tile(tri[d].T.astype(np.float32), (1, M_HEADS)))
        neg.append(np.tile(np.where(tri[d], 0.0, -np.inf).astype(np.float32), (1, M_HEADS)))
    return dict(tri=np.stack(tri).astype(np.float32), sel=np.stack(sel), tritile=np.stack(tritile),
                neg=np.stack(neg), kmask=kmask, vmask=vmask, bd2=np.concatenate([bd, bd], axis=1))


def _mlstm_unit(d, q, k, v, graw, graw_t, tri, sel, tritile, neg, kmask, vmask):
    n = q.shape[0]
    io, fo = 8 * d, 8 * d + 4
    last = n - 1 if d == 0 else 0
    lane = lax.broadcasted_iota(jnp.int32, (n, LANES), 1)
    lf = _log_sigmoid(graw)
    bcol = _dot_c_x(tri, lf)
    x = jnp.where((lane >= fo) & (lane < fo + M_HEADS), bcol, graw)
    ex = _dot_x_c(x, sel)
    b256, i256 = ex[:, 0:MS_W], ex[:, MS_W:2 * MS_W]
    b128, i128 = ex[:, 2 * MS_W:2 * MS_W + LANES], ex[:, 2 * MS_W + LANES:]
    rows_t = _dot_x_c(_log_sigmoid(graw_t), tritile)
    head_row = lax.broadcasted_iota(jnp.int32, (1, MS_W), 1) // M_DV
    b_row = jnp.zeros((1, MS_W), F32)
    for h in range(M_HEADS):
        b_row = jnp.where(head_row == h, rows_t[fo + h:fo + h + 1, :], b_row)
    decay = jnp.exp(b256 - b_row + neg)
    qb = q.astype(BF16)
    kb = k.astype(BF16)
    k_bd = jnp.concatenate([kb] * M_HEADS, axis=0) * kmask
    qk = lax.dot_general(qb, k_bd, (((1,), (1,)), ((), ())), preferred_element_type=F32)
    p = (qk * decay).astype(BF16)
    i_max = jnp.max(i256, axis=0, keepdims=True)
    scale = jnp.exp(i256 - i_max)
    vs = (v * scale).astype(BF16)
    sb = scale.astype(BF16)
    ve = jnp.concatenate([jnp.concatenate([vs] * M_HEADS, axis=0) * vmask,
                          jnp.concatenate([sb] * M_HEADS, axis=0) * vmask], axis=1)
    nd = jnp.dot(p, ve, preferred_element_type=F32)
    bl256 = b256[last:last + 1, :]
    g_max = jnp.max(bl256 - b256 + i256, axis=0, keepdims=True)
    g128 = b128[last:last + 1, :] - b128 + i128
    w = jnp.exp(g128 - jnp.max(g128, axis=0, keepdims=True))
    kw = (k * w).astype(BF16)
    v1 = jnp.concatenate([v.astype(BF16), jnp.ones((n, MS_W), BF16)], axis=1)
    upd = lax.dot_general(kw, v1, (((0,), (0,)), ((), ())), preferred_element_type=F32)

    def finish(cn, m, bd2):
        qcn = jnp.dot(qb, cn.astype(BF16), preferred_element_type=F32)
        a = b256 + m
        sig = jnp.maximum(i_max, a)
        f1 = jnp.exp(i_max - sig)
        f2 = jnp.exp(a - sig)
        num = f1 * nd[:, 0:MS_W] + f2 * qcn[:, 0:MS_W]
        den = f1 * nd[:, MS_W:] + f2 * qcn[:, MS_W:]
        h_out = num / jnp.maximum(jnp.abs(den), jnp.exp(-sig))
        m_new = jnp.maximum(bl256 + m, g_max)
        alpha = jnp.exp(bl256 + m - m_new)
        beta = jnp.exp(g_max - m_new)
        alpha2 = jnp.concatenate([alpha, alpha], axis=1)
        beta2 = jnp.concatenate([beta, beta], axis=1)
        return h_out, alpha2 * cn + (beta2 * bd2) * upd, m_new

    return finish


def _mlstm2_kernel(first_ref, seq_ref, bwd_ref,
                   zmf_ref, zsf_ref, ztf_ref, zmb_ref, zsb_ref, ztb_ref,
                   cn0_ref, m0_ref, gb_ref, gbt_ref,
                   tri_ref, sel_ref, tritile_ref, neg_ref, kmask_ref, vmask_ref, bd2_ref,
                   hf_ref, hb_ref, cnout_ref, mout_ref,
                   cns_ref, ms_ref):
    r = pl.program_id(0)

    fresh = first_ref[r] != 0
    nch = ROW_BLOCK // MS_CHUNK
    qk_w = M_HEADS * M_DK
    kmask = kmask_ref[...]
    vmask = vmask_ref[...]
    bd2 = bd2_ref[...]
    units = []
    for ci in range(nch):
        for d in (0, 1):
            zm_ref, zs_ref, zt_ref = ((zmf_ref, zsf_ref, ztf_ref), (zmb_ref, zsb_ref, ztb_ref))[d]
            c = ci if d == 0 else nch - 1 - ci
            rows = slice(c * MS_CHUNK, (c + 1) * MS_CHUNK)
            q = zm_ref[rows, 0:qk_w]
            k = zm_ref[rows, qk_w:2 * qk_w] * (M_DK ** -0.5)
            v = zm_ref[rows, 2 * qk_w:2 * qk_w + MS_W]
            graw = zs_ref[rows, :] + gb_ref[...]
            graw_t = zt_ref[:, rows] + gbt_ref[...]
            units.append((d, rows, _mlstm_unit(d, q, k, v, graw, graw_t, tri_ref[d], sel_ref[d],
                                               tritile_ref[d], neg_ref[d], kmask, vmask)))
    state = [(jnp.where(fresh, cn0_ref[0, d], cns_ref[d]), jnp.where(fresh, m0_ref[0, d], ms_ref[d]))
             for d in (0, 1)]
    for d, rows, finish in units:
        h_out, cn_new, m_new = finish(*state[d], bd2)
        (hf_ref, hb_ref)[d][rows, :] = h_out
        state[d] = (cn_new, m_new)
    for d in (0, 1):
        cns_ref[d], ms_ref[d] = state[d]
        cnout_ref[0, d], mout_ref[0, d] = state[d]


def _mlstm2_kernel_sm(first_ref, seq_ref, bwd_ref,
                      zmf_ref, zsf_ref, ztf_ref, zmb_ref, zsb_ref, ztb_ref,
                      cn0_ref, m0_ref, gb_ref, gbt_ref,
                      tri_ref, sel_ref, tritile_ref, neg_ref, kmask_ref, vmask_ref, bd2_ref,
                      hf_ref, hb_ref, cout_ref, nout_ref, mout_ref,
                      cns_ref, ms_ref):
    r = pl.program_id(0)
    fresh = first_ref[r] != 0
    n = MS_CHUNK
    nch = ROW_BLOCK // n
    qk_w = M_HEADS * M_DK
    kmask = kmask_ref[...]
    vmask = vmask_ref[...]
    bd2 = bd2_ref[...]
    nt_dims = (((1,), (1,)), ((), ()))
    tn_dims = (((0,), (0,)), ((), ()))
    units = [(d, ci if d == 0 else nch - 1 - ci) for ci in range(nch) for d in (0, 1)]
    rows = lambda c: slice(c * n, (c + 1) * n)
    zm_refs, zs_refs, zt_refs, h_refs = (zmf_ref, zmb_ref), (zsf_ref, zsb_ref), (ztf_ref, ztb_ref), (hf_ref, hb_ref)
    fo = lambda d: 8 * d + 4
    last = lambda d: n - 1 if d == 0 else 0
    lane = lax.broadcasted_iota(jnp.int32, (n, LANES), 1)
    head_row = lax.broadcasted_iota(jnp.int32, (1, MS_W), 1) // M_DV
    f32dot = lambda a, b: jnp.dot(a, b, preferred_element_type=F32)

    def per_unit(f):
        return [f(d, c) for d, c in units]

    def dot_split(c_left, x_split, c_right):
        hi, lo = x_split
        if c_left is not None:
            return f32dot(c_left, hi) + f32dot(c_left, lo)
        return f32dot(hi, c_right) + f32dot(lo, c_right)

    q = per_unit(lambda d, c: zm_refs[d][rows(c), 0:qk_w].astype(BF16))
    k = per_unit(lambda d, c: zm_refs[d][rows(c), qk_w:2 * qk_w] * (M_DK ** -0.5))
    v = per_unit(lambda d, c: zm_refs[d][rows(c), 2 * qk_w:2 * qk_w + MS_W])
    graw = per_unit(lambda d, c: zs_refs[d][rows(c), :] + gb_ref[...])
    graw_t = per_unit(lambda d, c: zt_refs[d][:, rows(c)] + gbt_ref[...])
    lf = [_split(_log_sigmoid(g)) for g in graw]
    lf_t = [_split(_log_sigmoid(g)) for g in graw_t]
    bcol = [dot_split(tri_ref[d], s, None) for (d, _), s in zip(units, lf)]
    xs = [_split(jnp.where((lane >= fo(d)) & (lane < fo(d) + M_HEADS), b, g))
          for (d, _), b, g in zip(units, bcol, graw)]
    ex = [dot_split(None, s, sel_ref[d]) for (d, _), s in zip(units, xs)]
    b256 = [e[:, 0:MS_W] for e in ex]
    i256 = [e[:, MS_W:2 * MS_W] for e in ex]
    b128 = [e[:, 2 * MS_W:2 * MS_W + LANES] for e in ex]
    i128 = [e[:, 2 * MS_W + LANES:] for e in ex]
    rows_t = [dot_split(None, s, tritile_ref[d]) for (d, _), s in zip(units, lf_t)]
    b_row = []
    for (d, _), rt in zip(units, rows_t):
        br = jnp.zeros((1, MS_W), F32)
        for h in range(M_HEADS):
            br = jnp.where(head_row == h, rt[fo(d) + h:fo(d) + h + 1, :], br)
        b_row.append(br)
    decay = [jnp.exp(b - br + neg_ref[d]) for (d, _), b, br in zip(units, b256, b_row)]
    k_bd = [jnp.concatenate([ki.astype(BF16)] * M_HEADS, axis=0) * kmask for ki in k]
    qk = [lax.dot_general(qi, ki, nt_dims, preferred_element_type=F32) for qi, ki in zip(q, k_bd)]
    p = [(a * b).astype(BF16) for a, b in zip(qk, decay)]
    i_max = [jnp.max(i, axis=0, keepdims=True) for i in i256]
    scale = [jnp.exp(i - im) for i, im in zip(i256, i_max)]
    ve = [jnp.concatenate([jnp.concatenate([(vi * sc).astype(BF16)] * M_HEADS, axis=0) * vmask,
                           jnp.concatenate([sc.astype(BF16)] * M_HEADS, axis=0) * vmask], axis=1)
          for vi, sc in zip(v, scale)]
    nd = [f32dot(pi, vei) for pi, vei in zip(p, ve)]
    bl256 = [b[last(d):last(d) + 1, :] for (d, _), b in zip(units, b256)]
    g_max = [jnp.max(bl - b + i, axis=0, keepdims=True) for bl, b, i in zip(bl256, b256, i256)]
    g128 = [b[last(d):last(d) + 1, :] - b + i for (d, _), b, i in zip(units, b128, i128)]
    w = [jnp.exp(g - jnp.max(g, axis=0, keepdims=True)) for g in g128]
    kw = [(ki * wi).astype(BF16) for ki, wi in zip(k, w)]
    ones = jnp.ones((n, MS_W), BF16)
    upd = [lax.dot_general(kwi, jnp.concatenate([vi.astype(BF16), ones], axis=1), tn_dims,
                           preferred_element_type=F32) for kwi, vi in zip(kw, v)]
    zero = first_ref[r] == 2
    state = [(jnp.where(fresh, jnp.where(zero, 0.0, cn0_ref[0, d]), cns_ref[d]),
              jnp.where(fresh, jnp.where(zero, 0.0, m0_ref[0, d]), ms_ref[d])) for d in (0, 1)]
    outs = []
    for i, (d, c) in enumerate(units):
        cn, m = state[d]
        qcn = f32dot(q[i], cn.astype(BF16))
        a = b256[i] + m
        sig = jnp.maximum(i_max[i], a)
        f1 = jnp.exp(i_max[i] - sig)
        f2 = jnp.exp(a - sig)
        num = f1 * nd[i][:, 0:MS_W] + f2 * qcn[:, 0:MS_W]
        den = f1 * nd[i][:, MS_W:] + f2 * qcn[:, MS_W:]
        outs.append(num / jnp.maximum(jnp.abs(den), jnp.exp(-sig)))
        m_new = jnp.maximum(bl256[i] + m, g_max[i])
        alpha = jnp.exp(bl256[i] + m - m_new)
        beta = jnp.exp(g_max[i] - m_new)
        state[d] = (jnp.concatenate([alpha, alpha], axis=1) * cn
                    + (jnp.concatenate([beta, beta], axis=1) * bd2) * upd[i], m_new)
    for (d, c), o in zip(units, outs):
        h_refs[d][rows(c), :] = o
    for d in (0, 1):
        cn, m = state[d]
        cns_ref[d], ms_ref[d] = cn, m
        mout_ref[0, d] = m
        for h in range(M_HEADS):
            blk = cn[h * M_DK:(h + 1) * M_DK, :]
            cout_ref[0, d, h] = blk[:, h * M_DV:(h + 1) * M_DV]
            nout_ref[0, d, h] = blk[:, MS_W + h * M_DV:MS_W + (h + 1) * M_DV]


def _mlstm2_call(n_ctx, zm, zs, zt, cn0, m0, gb, gbt, mc):
    nt = zm.shape[0]
    kd = M_HEADS * M_DK
    n = MS_CHUNK
    fwd = lambda r, first, seq, bwd: (r, 0)
    bwd_ = lambda r, first, seq, bwd: (bwd[r], 0)
    fwd_t = lambda r, first, seq, bwd: (0, r)
    bwd_t = lambda r, first, seq, bwd: (0, bwd[r])
    seq4 = lambda r, first, seq, bwd: (jnp.maximum(seq[r] - n_ctx, 0), 0, 0, 0)
    out4 = lambda r, first, seq, bwd: (jnp.minimum(seq[r], n_ctx), 0, 0, 0)
    out5 = lambda r, first, seq, bwd: (jnp.minimum(seq[r], n_ctx), 0, 0, 0, 0)
    const = lambda r, first, seq, bwd: (0, 0)
    const3 = lambda r, first, seq, bwd: (0, 0, 0)
    return dict(
        body=_mlstm2_kernel_sm,
        in_specs=[pl.BlockSpec((ROW_BLOCK, 512), fwd), pl.BlockSpec((ROW_BLOCK, LANES), fwd),
                  pl.BlockSpec((16, ROW_BLOCK), fwd_t),
                  pl.BlockSpec((ROW_BLOCK, 512), bwd_), pl.BlockSpec((ROW_BLOCK, LANES), bwd_),
                  pl.BlockSpec((16, ROW_BLOCK), bwd_t),
                  pl.BlockSpec((1, 2, kd, 2 * MS_W), seq4), pl.BlockSpec((1, 2, 1, MS_W), seq4),
                  pl.BlockSpec((1, LANES), const), pl.BlockSpec((16, 1), const),
                  pl.BlockSpec((2, n, n), const3), pl.BlockSpec((2, LANES, 3 * MS_W), const3),
                  pl.BlockSpec((2, n, MS_W), const3), pl.BlockSpec((2, n, MS_W), const3),
                  pl.BlockSpec((MS_W, kd), const), pl.BlockSpec((MS_W, MS_W), const),
                  pl.BlockSpec((kd, 2 * MS_W), const)],
        out_specs=[pl.BlockSpec((ROW_BLOCK, MS_W), fwd), pl.BlockSpec((ROW_BLOCK, MS_W), bwd_),
                   pl.BlockSpec((1, 2, M_HEADS, M_DK, M_DV), out5), pl.BlockSpec((1, 2, M_HEADS, M_DK, M_DV), out5),
                   pl.BlockSpec((1, 2, 1, MS_W), out4)],
        scratch_shapes=[pltpu.VMEM((2, kd, 2 * MS_W), F32), pltpu.VMEM((2, 1, MS_W), F32)],
        out_shape=[jax.ShapeDtypeStruct((nt, MS_W), F32), jax.ShapeDtypeStruct((nt, MS_W), F32),
                   jax.ShapeDtypeStruct((n_ctx + 1, 2, M_HEADS, M_DK, M_DV), F32),
                   jax.ShapeDtypeStruct((n_ctx + 1, 2, M_HEADS, M_DK, M_DV), F32),
                   jax.ShapeDtypeStruct((n_ctx + 1, 2, 1, MS_W), F32)],
        operands=[zm, zs, zt, zm, zs, zt, cn0, m0, gb, gbt,
                  mc["tri"], mc["sel"], mc["tritile"], mc["neg"], mc["kmask"], mc["vmask"], mc["bd2"]])


def _scan_call(plan, parts, name):
    n_in = [len(p["operands"]) for p in parts]
    n_out = [len(p["out_specs"]) for p in parts]
    n_scr = [len(p["scratch_shapes"]) for p in parts]

    def kern(first_ref, seq_ref, bwd_ref, *refs):
        ins, outs, scrs = refs[:sum(n_in)], refs[sum(n_in):sum(n_in) + sum(n_out)], refs[sum(n_in) + sum(n_out):]
        oi = oo = os_ = 0
        for p, a, b, c in zip(parts, n_in, n_out, n_scr):
            p["body"](first_ref, seq_ref, bwd_ref, *ins[oi:oi + a], *outs[oo:oo + b], *scrs[os_:os_ + c])
            oi, oo, os_ = oi + a, oo + b, os_ + c

    nb = plan["first"].shape[0]
    grid_spec = pltpu.PrefetchScalarGridSpec(
        num_scalar_prefetch=3,
        grid=(nb,),
        in_specs=[s for p in parts for s in p["in_specs"]],
        out_specs=[s for p in parts for s in p["out_specs"]],
        scratch_shapes=[s for p in parts for s in p["scratch_shapes"]],
    )
    outs = pl.pallas_call(
        kern,
        grid_spec=grid_spec,
        out_shape=[s for p in parts for s in p["out_shape"]],
        compiler_params=_cparams(("arbitrary",)),
        name=name,
    )(plan["first"], plan["seq"], plan["bwd"], *[o for p in parts for o in p["operands"]])
    res, oo = [], 0
    for b in n_out:
        res.append(outs[oo:oo + b])
        oo += b
    return res


def _gla_consts():
    n = G_CHUNK
    t = np.arange(n)[:, None]
    u = np.arange(n)[None, :]
    a_blocks, b_blocks, masks = [], [], [np.eye(n)]
    for lev in range(G_LEVELS):
        b = 1 << lev
        mid = (t // (2 * b)) * (2 * b) + b
        right = (t % (2 * b)) >= b
        a_blocks.append((right & (u >= mid) & (u <= t)).astype(np.float32))
        b_blocks.append((~right & (u > t) & (u <= mid - 1)).astype(np.float32))
        same_parent = (t // (2 * b)) == (u // (2 * b))
        masks.append((same_parent & right & ((u % (2 * b)) < b)).astype(np.float32))
    mcat_f = np.stack([np.tile(m, (1, G_HEADS)) for m in masks])
    flip = lambda m: m[::-1, ::-1]
    mcat_b = np.stack([np.tile(flip(m), (1, G_HEADS)) for m in masks])
    gath = []
    for d in (0, 1):
        g = np.zeros((G_SMALL_LEVELS * n, n), np.float32)
        for lev in range(G_SMALL_LEVELS):
            b = 1 << lev
            for row in range(n):
                g[lev * n + row, _gla_boundary(row, b, d)] = 1.0
        gath.append(g)
    tri = np.stack([(u <= t), (u >= t)]).astype(np.float32)
    return np.stack(gath), tri, np.stack([mcat_f, mcat_b])


G_SMALL_LEVELS = 3


def _gla_boundary(row, b, d):
    pstart = (row // (2 * b)) * (2 * b)
    return pstart + b - 1 if d == 0 else pstart + b


def _gla_kernel(first_ref, seq_ref, bwd_ref,
                zgf_ref, zsf_ref, zgb_ref, zsb_ref, s0_ref, w2_ref, b2_ref,
                tri_ref, gath_ref, mcat_ref, kmask_ref, vmask_ref, bd_ref,
                of_ref, ob_ref, sout_ref, ss_ref):
    r = pl.program_id(0)

    fresh = first_ref[r] != 0
    n = G_CHUNK
    nch = ROW_BLOCK // n
    kd = G_HEADS * G_DK
    vd = G_HEADS * G_DV
    ones_v = jnp.ones((n, vd), BF16)
    kmask = kmask_ref[...]
    vmask = vmask_ref[...]
    bd = bd_ref[...]
    nt_dims = (((1,), (1,)), ((), ()))

    def bd_k(kt):
        return jnp.concatenate([kt.astype(BF16)] * G_HEADS, axis=0) * kmask

    states = [jnp.where(fresh, s0_ref[0, d], ss_ref[d]) for d in (0, 1)]
    pending = []
    for ci in range(nch):
        for d in (0, 1):
            zg_ref, zs_ref, o_ref = ((zgf_ref, zsf_ref, of_ref), (zgb_ref, zsb_ref, ob_ref))[d]
            c = ci if d == 0 else nch - 1 - ci
            rows = slice(c * n, (c + 1) * n)
            q = zg_ref[rows, 0:kd] * (G_DK ** -0.5)
            k = zg_ref[rows, kd:2 * kd]
            v = zg_ref[rows, 2 * kd:2 * kd + vd].astype(BF16)
            x = jnp.dot(zs_ref[rows, :].astype(BF16), w2_ref[d], preferred_element_type=F32) + b2_ref[d]
            lg = _log_sigmoid(x) * (1.0 / G_TAU)
            bc = _dot_c_x(tri_ref[d], lg)
            r_small = _dot_c_x(gath_ref[d], bc)
            qb = q.astype(BF16)
            att = mcat_ref[d, 0] * lax.dot_general(qb, bd_k(k), nt_dims, preferred_element_type=F32)
            for lev in range(G_LEVELS):
                b = 1 << lev
                if lev < G_SMALL_LEVELS:
                    r_lev = r_small[lev * n:(lev + 1) * n]
                else:
                    r_lev = jnp.concatenate(
                        [jnp.broadcast_to(bc[_gla_boundary(p0, b, d):_gla_boundary(p0, b, d) + 1, :], (2 * b, kd))
                         for p0 in range(0, n, 2 * b)], axis=0)
                e_lev = jnp.exp(-jnp.abs(bc - r_lev))
                att = att + mcat_ref[d, lev + 1] * lax.dot_general(
                    (q * e_lev).astype(BF16), bd_k(k * e_lev), nt_dims, preferred_element_type=F32)
            v_bd = jnp.concatenate([v] * G_HEADS, axis=0) * vmask
            intra = jnp.dot(att.astype(BF16), v_bd, preferred_element_type=F32)
            inter = _dot(q * jnp.exp(bc), states[d])
            pending.append((o_ref, rows, inter + intra))
            last = n - 1 if d == 0 else 0
            decay = jnp.exp(_dot_xt_c(lg, ones_v))
            upd = lax.dot_general((k * jnp.exp(bc[last:last + 1, :] - bc)).astype(BF16), v,
                                  (((0,), (0,)), ((), ())), preferred_element_type=F32)
            states[d] = decay * states[d] + bd * upd
    for o_ref, rows, val in pending:
        o_ref[rows, :] = val
    for d in (0, 1):
        ss_ref[d] = states[d]
        sout_ref[0, d] = states[d]


def _gla_kernel_sm(first_ref, seq_ref, bwd_ref,
                   zgf_ref, zsf_ref, zgb_ref, zsb_ref, s0_ref, w2_ref, b2_ref,
                   tri_ref, gath_ref, mcat_ref, kmask_ref, vmask_ref, bd_ref,
                   of_ref, ob_ref, sout_ref, ss_ref):
    r = pl.program_id(0)
    fresh = first_ref[r] != 0
    n = G_CHUNK
    nch = ROW_BLOCK // n
    kd = G_HEADS * G_DK
    vd = G_HEADS * G_DV
    ones_v = jnp.ones((n, vd), BF16)
    kmask = kmask_ref[...]
    vmask = vmask_ref[...]
    bd = bd_ref[...]
    nt_dims = (((1,), (1,)), ((), ()))
    tn_dims = (((0,), (0,)), ((), ()))
    units = [(d, ci if d == 0 else nch - 1 - ci) for ci in range(nch) for d in (0, 1)]
    rows = lambda c: slice(c * n, (c + 1) * n)
    zg_refs, zs_refs, o_refs = (zgf_ref, zgb_ref), (zsf_ref, zsb_ref), (of_ref, ob_ref)

    def per_unit(f):
        return [f(d, c) for d, c in units]

    def bd_k(kt):
        return jnp.concatenate([kt] * G_HEADS, axis=0) * kmask

    q = per_unit(lambda d, c: zg_refs[d][rows(c), 0:kd] * (G_DK ** -0.5))
    k = per_unit(lambda d, c: zg_refs[d][rows(c), kd:2 * kd])
    v = per_unit(lambda d, c: zg_refs[d][rows(c), 2 * kd:2 * kd + vd].astype(BF16))
    x = per_unit(lambda d, c: jnp.dot(zs_refs[d][rows(c), :].astype(BF16), w2_ref[d],
                                      preferred_element_type=F32) + b2_ref[d])
    lg = [_log_sigmoid(xi) * (1.0 / G_TAU) for xi in x]
    lg_split = [_split(l) for l in lg]
    bc = [jnp.dot(tri_ref[d], hi, preferred_element_type=F32) + jnp.dot(tri_ref[d], lo, preferred_element_type=F32)
          for (d, _), (hi, lo) in zip(units, lg_split)]
    bc_split = [_split(b) for b in bc]
    r_small = [jnp.dot(gath_ref[d], hi, preferred_element_type=F32) + jnp.dot(gath_ref[d], lo, preferred_element_type=F32)
               for (d, _), (hi, lo) in zip(units, bc_split)]
    qb = [qi.astype(BF16) for qi in q]
    kbd = [bd_k(ki.astype(BF16)) for ki in k]
    att = [mcat_ref[d, 0] * lax.dot_general(qi, ki, nt_dims, preferred_element_type=F32)
           for (d, _), qi, ki in zip(units, qb, kbd)]
    for lev in range(G_LEVELS):
        b = 1 << lev
        if lev < G_SMALL_LEVELS:
            r_lev = [rs[lev * n:(lev + 1) * n] for rs in r_small]
        else:
            r_lev = [jnp.concatenate(
                [jnp.broadcast_to(bci[_gla_boundary(p0, b, d):_gla_boundary(p0, b, d) + 1, :], (2 * b, kd))
                 for p0 in range(0, n, 2 * b)], axis=0) for (d, _), bci in zip(units, bc)]
        e_lev = [jnp.exp(-jnp.abs(bci - ri)) for bci, ri in zip(bc, r_lev)]
        qt = [(qi * ei).astype(BF16) for qi, ei in zip(q, e_lev)]
        kt = [bd_k((ki * ei).astype(BF16)) for ki, ei in zip(k, e_lev)]
        p = [lax.dot_general(qi, ki, nt_dims, preferred_element_type=F32) for qi, ki in zip(qt, kt)]
        att = [ai + mcat_ref[d, lev + 1] * pi for (d, _), ai, pi in zip(units, att, p)]
    v_bd = [jnp.concatenate([vi] * G_HEADS, axis=0) * vmask for vi in v]
    intra = [jnp.dot(ai.astype(BF16), vi, preferred_element_type=F32) for ai, vi in zip(att, v_bd)]
    q_inc = [(qi * jnp.exp(bci)).astype(BF16) for qi, bci in zip(q, bc)]
    k_suf = [(ki * jnp.exp(bci[(n - 1 if d == 0 else 0):(n if d == 0 else 1), :] - bci)).astype(BF16)
             for (d, _), ki, bci in zip(units, k, bc)]
    decay = [jnp.exp(lax.dot_general(hi, ones_v, tn_dims, preferred_element_type=F32)
                     + lax.dot_general(lo, ones_v, tn_dims, preferred_element_type=F32))
             for hi, lo in lg_split]
    upd = [bd * lax.dot_general(ki, vi, tn_dims, preferred_element_type=F32) for ki, vi in zip(k_suf, v)]
    zero = first_ref[r] == 2
    states = [jnp.where(fresh, jnp.where(zero, 0.0, s0_ref[0, d]), ss_ref[d]) for d in (0, 1)]
    outs = []
    for i, (d, c) in enumerate(units):
        inter = jnp.dot(q_inc[i], states[d].astype(BF16), preferred_element_type=F32)
        outs.append(inter + intra[i])
        states[d] = decay[i] * states[d] + upd[i]
    for (d, c), o in zip(units, outs):
        o_refs[d][rows(c), :] = o
    for d in (0, 1):
        ss_ref[d] = states[d]
        for h in range(G_HEADS):
            sout_ref[0, d, h] = states[d][h * G_DK:(h + 1) * G_DK, h * G_DV:(h + 1) * G_DV]


def _gla_call(n_ctx, zg, zs, s0, w2p, b2p, gath, tri, mcat, kmask, vmask, bd):
    nt = zg.shape[0]
    kd = G_HEADS * G_DK
    vd = G_HEADS * G_DV
    n = G_CHUNK
    fwd = lambda r, first, seq, bwd: (r, 0)
    bwd_ = lambda r, first, seq, bwd: (bwd[r], 0)
    seq4 = lambda r, first, seq, bwd: (jnp.maximum(seq[r] - n_ctx, 0), 0, 0, 0)
    out5 = lambda r, first, seq, bwd: (jnp.minimum(seq[r], n_ctx), 0, 0, 0, 0)
    const3 = lambda r, first, seq, bwd: (0, 0, 0)
    const4 = lambda r, first, seq, bwd: (0, 0, 0, 0)
    return dict(
        body=_gla_kernel_sm,
        in_specs=[pl.BlockSpec((ROW_BLOCK, 512), fwd), pl.BlockSpec((ROW_BLOCK, LANES), fwd),
                  pl.BlockSpec((ROW_BLOCK, 512), bwd_), pl.BlockSpec((ROW_BLOCK, LANES), bwd_),
                  pl.BlockSpec((1, 2, kd, vd), seq4),
                  pl.BlockSpec((2, LANES, kd), const3), pl.BlockSpec((2, 1, kd), const3),
                  pl.BlockSpec((2, n, n), const3),
                  pl.BlockSpec((2, G_SMALL_LEVELS * n, n), const3),
                  pl.BlockSpec((2, G_LEVELS + 1, n, G_HEADS * n), const4),
                  pl.BlockSpec((G_HEADS * n, kd), lambda r, first, seq, bwd: (0, 0)),
                  pl.BlockSpec((G_HEADS * n, vd), lambda r, first, seq, bwd: (0, 0)),
                  pl.BlockSpec((kd, vd), lambda r, first, seq, bwd: (0, 0))],
        out_specs=[pl.BlockSpec((ROW_BLOCK, vd), fwd), pl.BlockSpec((ROW_BLOCK, vd), bwd_),
                   pl.BlockSpec((1, 2, G_HEADS, G_DK, G_DV), out5)],
        scratch_shapes=[pltpu.VMEM((2, kd, vd), F32)],
        out_shape=[jax.ShapeDtypeStruct((nt, vd), F32), jax.ShapeDtypeStruct((nt, vd), F32),
                   jax.ShapeDtypeStruct((n_ctx + 1, 2, G_HEADS, G_DK, G_DV), F32)],
        operands=[zg, zs, zg, zs, s0, w2p, b2p, tri, gath, mcat, kmask, vmask, bd])


def _with_ones_lane(v):
    lane = lax.broadcasted_iota(jnp.int32, v.shape, 1) % A_HPAD
    return jnp.where(lane == A_DV, 1.0, v)


def _rms(x, g, eps=1e-6):
    return x * lax.rsqrt(jnp.mean(x * x, axis=-1, keepdims=True) + eps) * g


def _mla_prep_kernel(zq_ref, zs_ref, cos_ref, sin_ref, kcos_ref, gq_ref, gkv_ref,
                     wuq_ref, wuqs_ref, wk_ref, wv_ref, q_ref, k_ref, v_ref, ckv_ref):
    cq = _rms(zq_ref[:, 0:A_DQ], gq_ref[...]).astype(BF16)
    qn = jnp.dot(cq, wuq_ref[...], preferred_element_type=F32)
    qs = jnp.dot(cq, wuqs_ref[...], preferred_element_type=F32)
    cos = cos_ref[...]
    sin = sin_ref[...]
    for h in range(A_HEADS):
        sl = slice(h * A_HPAD, (h + 1) * A_HPAD)
        q_ref[:, sl] = ((qn[:, sl] * cos + qs[:, sl] * sin) * A_SCALE).astype(BF16)
    ckv = _rms(zq_ref[:, A_DQ:A_DQ + A_DC], gkv_ref[...])
    ckv_ref[...] = ckv
    ckvb = ckv.astype(BF16)
    kn = jnp.dot(ckvb, wk_ref[...], preferred_element_type=F32)
    v_ref[...] = _with_ones_lane(jnp.dot(ckvb, wv_ref[...], preferred_element_type=F32)).astype(BF16)
    kr = zs_ref[:, 0:LANES] * kcos_ref[...] + zs_ref[:, LANES:2 * LANES] * sin
    for h in range(A_HEADS):
        sl = slice(h * A_HPAD, (h + 1) * A_HPAD)
        k_ref[:, sl] = (kn[:, sl] + kr).astype(BF16)


def _mla_prep_call(zq, zs, cos, sin, kcos, gq, gkv, wuq, wuqs, wk, wv):
    nt = zq.shape[0]
    tm = ROW_BLOCK
    hw = A_HEADS * A_HPAD
    row = lambda i: (i, 0)
    const = lambda i: (0, 0)
    return pl.pallas_call(
        _mla_prep_kernel,
        grid=(nt // tm,),
        in_specs=[pl.BlockSpec((tm, ZQ_W), row), pl.BlockSpec((tm, ZS_W), row),
                  pl.BlockSpec((tm, LANES), row), pl.BlockSpec((tm, LANES), row), pl.BlockSpec((tm, LANES), row),
                  pl.BlockSpec((1, A_DQ), const), pl.BlockSpec((1, A_DC), const),
                  pl.BlockSpec((A_DQ, hw), const), pl.BlockSpec((A_DQ, hw), const),
                  pl.BlockSpec((A_DC, hw), const), pl.BlockSpec((A_DC, hw), const)],
        out_specs=[pl.BlockSpec((tm, hw), row), pl.BlockSpec((tm, hw), row), pl.BlockSpec((tm, hw), row),
                   pl.BlockSpec((tm, A_DC), row)],
        out_shape=[jax.ShapeDtypeStruct((nt, hw), BF16), jax.ShapeDtypeStruct((nt, hw), BF16),
                   jax.ShapeDtypeStruct((nt, hw), BF16), jax.ShapeDtypeStruct((nt, A_DC), F32)],
        compiler_params=_cparams(("parallel",)),
        name="mla_prep",
    )(zq, zs, cos, sin, kcos, gq, gkv, wuq, wuqs, wk, wv)


def _cache_kv_kernel(ckv_ref, kr_ref, wk_ref, wv_ref, k_ref, v_ref):
    ckvb = ckv_ref[...].astype(BF16)
    kn = jnp.dot(ckvb, wk_ref[...], preferred_element_type=F32)
    v_ref[...] = _with_ones_lane(jnp.dot(ckvb, wv_ref[...], preferred_element_type=F32)).astype(BF16)
    kr = kr_ref[...]
    for h in range(A_HEADS):
        sl = slice(h * A_HPAD, (h + 1) * A_HPAD)
        k_ref[:, sl] = (kn[:, sl] + kr).astype(BF16)


def _cache_kv_call(ckv, krp, wk, wv):
    nt = ckv.shape[0]
    tm = ROW_BLOCK
    hw = A_HEADS * A_HPAD
    row = lambda i: (i, 0)
    const = lambda i: (0, 0)
    return pl.pallas_call(
        _cache_kv_kernel,
        grid=(nt // tm,),
        in_specs=[pl.BlockSpec((tm, A_DC), row), pl.BlockSpec((tm, LANES), row),
                  pl.BlockSpec((A_DC, hw), const), pl.BlockSpec((A_DC, hw), const)],
        out_specs=[pl.BlockSpec((tm, hw), row), pl.BlockSpec((tm, hw), row)],
        out_shape=[jax.ShapeDtypeStruct((nt, hw), BF16), jax.ShapeDtypeStruct((nt, hw), BF16)],
        compiler_params=_cparams(("parallel",)),
        name="mla_cache_kv",
    )(ckv, krp, wk, wv)


def _attn_kernel(*refs, n_seg, heads_per_group):
    q_ref = refs[0]
    kv_refs = refs[1:1 + 2 * n_seg]
    o_ref = refs[1 + 2 * n_seg]
    dn = (((1,), (1,)), ((), ()))
    tiles = []
    for j in range(n_seg):
        nk = kv_refs[2 * j].shape[0]
        tk = min(nk, A_KEY_TILE)
        tiles += [(j, t0, tk) for t0 in range(0, nk, tk)]
    head = lambda h: slice(h * A_HPAD, (h + 1) * A_HPAD)

    def scores(h, tile):
        j, t0, tk = tile
        return lax.dot_general(q_ref[:, head(h)], kv_refs[2 * j][t0:t0 + tk, head(h)], dn,
                               preferred_element_type=F32)

    def lane_max(acc, s):
        for c0 in range(0, s.shape[1], LANES):
            part = s[:, c0:c0 + LANES]
            acc = part if acc is None else jnp.maximum(acc, part)
        return acc

    groups = [list(range(g0, min(g0 + heads_per_group, A_HEADS))) for g0 in range(0, A_HEADS, heads_per_group)]
    s_cur = {h: [scores(h, t) for t in tiles] for h in groups[0]}
    for gi, grp in enumerate(groups):
        nxt = groups[gi + 1] if gi + 1 < len(groups) else []
        m = {}
        for h in grp:
            mx = None
            for s in s_cur[h]:
                mx = lane_max(mx, s)
            m[h] = jnp.max(mx, axis=1, keepdims=True)
        s_next = {h: [] for h in nxt}
        acc = {h: None for h in grp}
        for ti, tile in enumerate(tiles):
            j, t0, tk = tile
            for h in nxt:
                s_next[h].append(scores(h, tile))
            p = {h: jnp.exp((s_cur[h][ti] - m[h]).astype(BF16)) for h in grp}
            for h in grp:
                pv = jnp.dot(p[h], kv_refs[2 * j + 1][t0:t0 + tk, head(h)], preferred_element_type=F32)
                acc[h] = pv if acc[h] is None else acc[h] + pv
        for h in grp:
            o_ref[:, head(h)] = (acc[h] / acc[h][:, A_DV:A_DV + 1]).astype(BF16)
        s_cur = s_next


def _attn_ctx_call(q, k, v, n_rows):
    hw = A_HEADS * A_HPAD
    tm = ROW_BLOCK
    row = lambda i: (i, 0)
    return pl.pallas_call(
        functools.partial(_attn_kernel, n_seg=1, heads_per_group=A_HEADS),
        grid=(n_rows // tm,),
        in_specs=[pl.BlockSpec((tm, hw), row)] * 3,
        out_specs=pl.BlockSpec((tm, hw), row),
        out_shape=jax.ShapeDtypeStruct((n_rows, hw), BF16),
        compiler_params=_cparams(("parallel",)),
        name="mla_attn_ctx",
    )(q, k, v)


def _attn_dec_call(q, k, v, kc, vc, n_ctx_rows, n_dec, dec_len, past_len):
    hw = A_HEADS * A_HPAD
    tm = ROW_BLOCK
    nq = dec_len // tm
    assert n_ctx_rows % dec_len == 0
    qmap = lambda b, i: (n_ctx_rows // tm + b * nq + i, 0)
    own = lambda b, i: (n_ctx_rows // dec_len + b, 0)
    return pl.pallas_call(
        functools.partial(_attn_kernel, n_seg=2, heads_per_group=2),
        grid=(n_dec, nq),
        in_specs=[pl.BlockSpec((tm, hw), qmap),
                  pl.BlockSpec((past_len, hw), lambda b, i: (b, 0)),
                  pl.BlockSpec((past_len, hw), lambda b, i: (b, 0)),
                  pl.BlockSpec((dec_len, hw), own),
                  pl.BlockSpec((dec_len, hw), own)],
        out_specs=pl.BlockSpec((tm, hw), lambda b, i: (b * nq + i, 0)),
        out_shape=jax.ShapeDtypeStruct((n_dec * dec_len, hw), BF16),
        compiler_params=_cparams(("parallel", "arbitrary")),
        name="mla_attn_dec",
    )(q, kc, vc, k, v)


def _layer_norm(y, g, b, eps=1e-5):
    yc = y - jnp.mean(y, axis=-1, keepdims=True)
    return yc * lax.rsqrt(jnp.mean(yc * yc, axis=-1, keepdims=True) + eps) * g + b


def _outproj_kernel(hf_ref, hb_ref, of_ref, ob_ref, actx_ref, adec_ref, mo_ref, gg_ref, x_ref, mod_ref,
                    mng_ref, gng_ref, avg_ref, wm_ref, wg_ref, wa_ref, lg_ref, lb_ref, o_ref, *, alpha, nb_ctx):
    d = D_MODEL
    avg = avg_ref[...]
    mod = mod_ref[0]
    is_ctx = pl.program_id(0) < nb_ctx
    tm = x_ref.shape[0]
    parts = [slice(p0, p0 + tm // OUTPROJ_ROW_GROUPS) for p0 in range(0, tm, tm // OUTPROJ_ROW_GROUPS)]
    each = lambda f: [f(s) for s in parts]
    both = lambda f, xs, ys: [f(x, y) for x, y in zip(xs, ys)]
    a = each(lambda s: jnp.where(is_ctx, actx_ref[s, :], adec_ref[s, :]))
    hm = each(lambda s: hf_ref[s, :] + hb_ref[s, :])
    og = each(lambda s: of_ref[s, :] + ob_ref[s, :])
    hc = both(lambda x, mu: x - mu, hm, [_dot_x_c(x, avg) for x in hm])
    hvar = [_dot_x_c(x * x, avg) for x in hc]
    gms = [_dot_x_c(x * x, avg) for x in og]
    m_out = [x * lax.rsqrt(v + 1e-6) * mng_ref[...] * _sigmoid(mo_ref[s, :]) for x, v, s in zip(hc, hvar, parts)]
    gg = each(lambda s: gg_ref[s, :])
    g_out = [x * lax.rsqrt(v + 1e-6) * gng_ref[...] * (g * _sigmoid(g)) for x, v, g in zip(og, gms, gg)]
    mix = [_dot(m, wm_ref[...]) + _dot(g, wg_ref[...]) + jnp.dot(ai, wa_ref[...], preferred_element_type=F32)
           for m, g, ai in zip(m_out, g_out, a)]
    y = [alpha * x_ref[s, :] + mod[:, 2 * d:3 * d] * mi for s, mi in zip(parts, mix)]
    out = [_layer_norm(yi, lg_ref[...], lb_ref[...]) for yi in y]
    for s, oi in zip(parts, out):
        o_ref[s, :] = oi


def _outproj_call(hf, hb, of, ob, a_ctx, a_dec, zm, zg, x, mods, mng, gng, avg, wm, wg, wa, lg, lb, mod_map, alpha):
    nt, d = x.shape
    nb_ctx = a_ctx.shape[0] // ROW_BLOCK
    tm = ROW_BLOCK
    hw = M_HEADS * M_DV
    row = lambda i: (i, 0)
    const = lambda i: (0, 0)
    gate_col = lambda i: (i, 2)
    return pl.pallas_call(
        functools.partial(_outproj_kernel, alpha=alpha, nb_ctx=nb_ctx),
        grid=(nt // tm,),
        in_specs=[pl.BlockSpec((tm, hw), row)] * 4
                 + [pl.BlockSpec((tm, A_HEADS * A_HPAD), lambda i: (jnp.minimum(i, nb_ctx - 1), 0)),
                    pl.BlockSpec((tm, A_HEADS * A_HPAD), lambda i: (jnp.maximum(i - nb_ctx, 0), 0)),
                    pl.BlockSpec((tm, hw), gate_col), pl.BlockSpec((tm, hw), gate_col),
                    pl.BlockSpec((tm, d), row),
                    pl.BlockSpec((1, 1, 6 * d), lambda i: (mod_map(i), 0, 0)),
                    pl.BlockSpec((1, hw), const), pl.BlockSpec((1, hw), const),
                    pl.BlockSpec((hw, hw), const),
                    pl.BlockSpec((hw, d), const), pl.BlockSpec((hw, d), const),
                    pl.BlockSpec((A_HEADS * A_HPAD, d), const),
                    pl.BlockSpec((1, d), const), pl.BlockSpec((1, d), const)],
        out_specs=pl.BlockSpec((tm, d), row),
        out_shape=jax.ShapeDtypeStruct((nt, d), F32),
        compiler_params=_cparams(("parallel",)),
        name="out_proj_ln",
    )(hf, hb, of, ob, a_ctx, a_dec, zm, zg, x, mods, mng, gng, avg, wm, wg, wa, lg, lb)


def _ffn_kernel(x_ref, mod_ref, wg_ref, wu_ref, wd_ref, lg_ref, lb_ref, o_ref, *, n_chunks, alpha):
    d = D_MODEL
    mod = mod_ref[0]
    x = x_ref[...]
    hh = (x * (1.0 + mod[:, 4 * d:5 * d]) + mod[:, 3 * d:4 * d]).astype(BF16)
    f = wg_ref.shape[1]
    tiles = -(-f // MXU_TILE)
    edges = [min(f, MXU_TILE * (tiles * c // n_chunks)) for c in range(n_chunks)] + [f]
    acc = jnp.zeros(x.shape, F32)
    for c in range(n_chunks):
        sl = slice(edges[c], edges[c + 1])
        g = jnp.dot(hh, wg_ref[:, sl], preferred_element_type=F32)
        u = jnp.dot(hh, wu_ref[:, sl], preferred_element_type=F32)
        act = (g * _sigmoid(g) * u).astype(BF16)
        acc = acc + jnp.dot(act, wd_ref[sl, :], preferred_element_type=F32)
    y = alpha * x + mod[:, 5 * d:6 * d] * acc
    o_ref[...] = _layer_norm(y, lg_ref[...], lb_ref[...])


def _ffn_call(x, mods, wg, wu, wd, lg, lb, mod_map, alpha):
    nt, d = x.shape
    f = wg.shape[1]
    tm = FFN_ROWS
    row = lambda i: (i, 0)
    const = lambda i: (0, 0)
    resident = dict(pipeline_mode=pl.Buffered(1))
    return pl.pallas_call(
        functools.partial(_ffn_kernel, n_chunks=FFN_CHUNKS, alpha=alpha),
        grid=(nt // tm,),
        in_specs=[pl.BlockSpec((tm, d), row),
                  pl.BlockSpec((1, 1, 6 * d), lambda i: (mod_map(i * (tm // ROW_BLOCK)), 0, 0)),
                  pl.BlockSpec((d, f), const, **resident), pl.BlockSpec((d, f), const, **resident),
                  pl.BlockSpec((f, d), const, **resident),
                  pl.BlockSpec((1, d), const), pl.BlockSpec((1, d), const)],
        out_specs=pl.BlockSpec((tm, d), row),
        out_shape=jax.ShapeDtypeStruct((nt, d), F32),
        compiler_params=_cparams(("parallel",)),
        name="ffn_ln",
    )(x, mods, wg, wu, wd, lg, lb)


def _block_diag_heads(t):
    h = t.shape[-3]
    eye = jnp.eye(h, dtype=t.dtype)
    out = jnp.einsum('...hkv,hg->...hkgv', t, eye)
    return out.reshape(t.shape[:-3] + (h * t.shape[-2], h * t.shape[-1]))


def _diag_blocks(t, h, dk, dv):
    t = t.reshape(t.shape[:-2] + (h, dk, h, dv))
    return jnp.stack([t[..., i, :, i, :] for i in range(h)], axis=-3)


def _rope_tables(n_ctx_rows, n_dec, dec_len):
    half = A_DROPE // 4
    inv = ROPE_BASE ** (-jnp.arange(half, dtype=F32) / half)
    pos = jnp.arange(dec_len, dtype=jnp.int32)
    rows = (pos // GRID_W).astype(F32)[:, None] * inv[None, :]
    cols = (pos % GRID_W).astype(F32)[:, None] * inv[None, :]
    cr, sr, cc, sc = jnp.cos(rows), jnp.sin(rows), jnp.cos(cols), jnp.sin(cols)
    cos32 = jnp.concatenate([cr, cr, cc, cc], axis=1)
    sin32 = jnp.concatenate([-sr, sr, -sc, sc], axis=1)
    ones_lo = jnp.ones((dec_len, KR_LO), F32)
    ones_hi = jnp.ones((dec_len, A_HPAD - KR_HI), F32)
    cos = jnp.concatenate([ones_lo, cos32, ones_hi], axis=1)
    sin = jnp.concatenate([0 * ones_lo, sin32, 0 * ones_hi], axis=1)
    kcos = jnp.concatenate([0 * ones_lo, cos32, 0 * ones_hi], axis=1)
    lane = jnp.arange(A_HPAD)
    rope_lane = ((lane >= KR_LO) & (lane < KR_HI)).astype(F32)[None, :]
    ctx_cos = jnp.ones((n_ctx_rows, A_HPAD), F32)
    ctx_sin = jnp.zeros((n_ctx_rows, A_HPAD), F32)
    ctx_kcos = jnp.broadcast_to(rope_lane, (n_ctx_rows, A_HPAD))
    tile = lambda t: jnp.tile(t, (n_dec, 1))
    return (jnp.concatenate([ctx_cos, tile(cos)]), jnp.concatenate([ctx_sin, tile(sin)]),
            jnp.concatenate([ctx_kcos, tile(kcos)]))


def _rope_partner(w):
    e = A_DROPE // 4
    return jnp.concatenate([w[..., e:2 * e], w[..., 0:e], w[..., 3 * e:4 * e], w[..., 2 * e:3 * e]], axis=-1)


def kernel(x_prompt, x_sample, state_mlstm_C, state_mlstm_n, state_mlstm_m, state_gla_S, cache_mla_ckv, cache_mla_krope, c, c_ctx, w_ada, b_ada, w_in, m_gate_b, m_norm_g, g_w2, g_b2, g_norm_g, a_q_norm_g, a_kv_norm_g, a_w_uq, a_w_ukv, w_out, ln1_g, ln1_b, w_ffn_gate, w_ffn_up, w_ffn_down, ln2_g, ln2_b):
    n_ctx, ctx_len, d = x_prompt.shape
    n_dec, dec_len, _ = x_sample.shape
    depth = w_in.shape[0]
    past_len = cache_mla_ckv.shape[2]
    assert d == D_MODEL and ctx_len == ROW_BLOCK and dec_len % ROW_BLOCK == 0 and past_len % 8 == 0
    assert (n_ctx * ctx_len) % FFN_ROWS == 0 and dec_len % FFN_ROWS == 0
    assert n_dec + 1 <= 8
    n_ctx_rows = n_ctx * ctx_len
    n_dec_rows = n_dec * dec_len
    nt = n_ctx_rows + n_dec_rows
    nb_ctx = n_ctx_rows // ROW_BLOCK
    nb_seq = dec_len // ROW_BLOCK
    nseq = n_ctx + n_dec
    alpha = (2.0 * depth) ** 0.25

    blocks = np.arange(nt // ROW_BLOCK)
    dec_b = np.maximum(blocks - nb_ctx, 0)
    is_ctx = blocks < nb_ctx
    plan = {
        "first": jnp.asarray(np.where(is_ctx, 2, (dec_b % nb_seq) == 0).astype(np.int32)),
        "seq": jnp.asarray(np.where(is_ctx, blocks, n_ctx + dec_b // nb_seq).astype(np.int32)),
        "bwd": jnp.asarray(np.where(is_ctx, blocks,
                                    nb_ctx + (dec_b // nb_seq) * nb_seq + nb_seq - 1 - dec_b % nb_seq).astype(np.int32)),
    }

    def mod_map_for(layer):
        def mod_map(i):
            return layer * 8 + jnp.where(i < nb_ctx, 0, 1 + (i - nb_ctx) // nb_seq)
        return mod_map

    cond8 = jnp.zeros((8, d), F32).at[0].set(c_ctx).at[1:1 + n_dec].set(c)
    mods = _ada_call(cond8, w_ada, b_ada).reshape(depth * 8, 1, 6 * d)

    pts = [int(p) for p in np.cumsum(IN_SPLITS)[:-1]]
    (wmq, wmk, wmv, wmo, wmg, wgq, wgk, wgv, wgg, wga, wacq, wackv, wakr) = jnp.split(w_in, pts, axis=-1)
    zpad = lambda n: jnp.zeros((depth, d, n), F32)
    s1 = jnp.concatenate([wmg, wga, zpad(KR_LO - 48), wakr, zpad(LANES - KR_HI)], axis=-1)
    s2 = jnp.concatenate([zpad(KR_LO), _rope_partner(wakr), zpad(LANES - KR_HI)], axis=-1)
    w_in_p = jnp.concatenate([wmq, wmk, wmv, wmo, wgq, wgk, wgv, wgg, wacq, wackv, s1, s2], axis=-1).astype(BF16)
    w_gt = jnp.swapaxes(wmg, 1, 2).astype(BF16)
    gate_b = jnp.zeros((depth, 1, LANES), F32).at[:, 0, 0:16].set(m_gate_b)
    gate_bt = m_gate_b.reshape(depth, 16, 1)
    w2p = jnp.zeros((depth, 2, LANES, G_HEADS * G_DK), F32)
    w2p = w2p.at[:, 0, 16:16 + G_RANK].set(g_w2[:, 0]).at[:, 1, 16 + G_RANK:16 + 2 * G_RANK].set(g_w2[:, 1]).astype(BF16)
    b2p = g_b2.reshape(depth, 2, 1, G_HEADS * G_DK)
    uq = a_w_uq.reshape(depth, A_DQ, A_HEADS, A_DNOPE + A_DROPE)
    hp = lambda n: jnp.zeros((depth, uq.shape[1], A_HEADS, n), F32)
    wuq = jnp.concatenate([uq, hp(A_HPAD - KR_HI)], axis=-1).reshape(depth, A_DQ, -1).astype(BF16)
    wuqs = jnp.concatenate([hp(KR_LO), _rope_partner(uq[..., A_DNOPE:]), hp(A_HPAD - KR_HI)],
                           axis=-1).reshape(depth, A_DQ, -1).astype(BF16)
    ukv = a_w_ukv.reshape(depth, A_DC, A_HEADS, A_DNOPE + A_DV)
    kp = jnp.zeros((depth, A_DC, A_HEADS, A_HPAD - A_DNOPE), F32)
    wk = jnp.concatenate([ukv[..., :A_DNOPE], kp], axis=-1).reshape(depth, A_DC, -1).astype(BF16)
    wv = jnp.concatenate([ukv[..., A_DNOPE:], kp], axis=-1).reshape(depth, A_DC, -1).astype(BF16)
    mw, gw = M_HEADS * M_DV, G_HEADS * G_DV
    wo_m = w_out[:, 0:mw].astype(BF16)
    wo_g = w_out[:, mw:mw + gw].astype(BF16)
    wo_a = w_out[:, mw + gw:].reshape(depth, A_HEADS, A_DV, d)
    wo_a = jnp.concatenate([wo_a, jnp.zeros((depth, A_HEADS, A_HPAD - A_DV, d), F32)], axis=2)
    wo_a = wo_a.reshape(depth, A_HEADS * A_HPAD, d).astype(BF16)
    wfg, wfu, wfd = w_ffn_gate.astype(BF16), w_ffn_up.astype(BF16), w_ffn_down.astype(BF16)

    ti = np.arange(M_CHUNK)
    trif = jnp.asarray((ti[None, :] <= ti[:, None]).astype(np.float32), BF16)
    trib = jnp.asarray((ti[None, :] >= ti[:, None]).astype(np.float32), BF16)
    mconst = {k_: jnp.asarray(v_, F32 if k_ in ("neg", "bd2") else BF16) for k_, v_ in _mlstm_consts().items()}
    head_lanes = jnp.asarray((np.arange(M_HEADS)[:, None] == (np.arange(MS_W) // M_DV)[None, :]).astype(np.float32))
    gath_np, gtri_np, mcat_np = _gla_consts()
    gath = jnp.asarray(gath_np, BF16)
    gtri = jnp.asarray(gtri_np, BF16)
    gbd = mconst["bd2"][:, 0:MS_W]
    mcat = jnp.asarray(mcat_np, F32)
    hi = np.arange(mw) // M_DV
    avg = jnp.asarray((hi[:, None] == hi[None, :]).astype(np.float32) / M_DV, BF16)
    cos, sin, kcos = _rope_tables(n_ctx_rows, n_dec, dec_len)

    def init_state(block, n_tail):
        z = jnp.zeros((n_ctx,) + block.shape[1:], F32)
        return jnp.concatenate([z, block], axis=0)

    krp_all = jnp.zeros(cache_mla_krope.shape[:-1] + (LANES,), F32).at[..., KR_LO:KR_HI].set(cache_mla_krope)

    x = jnp.concatenate([x_prompt.reshape(n_ctx_rows, d), x_sample.reshape(n_dec_rows, d)], axis=0)
    new_c, new_n, new_m, new_s, new_ckv, new_kr = [], [], [], [], [], []
    for l in range(depth):
        mod_map = mod_map_for(l)
        zm, zg, zq, zs, zt = _inproj_call(x, mods, w_in_p[l], w_gt[l], mod_map)

        n_bd = jnp.einsum('bdhk,hg->bdhkg', state_mlstm_n[:, l], head_lanes)
        cn0 = jnp.concatenate([_block_diag_heads(state_mlstm_C[:, l]),
                               n_bd.reshape(n_dec, 2, M_HEADS * M_DK, MS_W)], axis=-1)
        m0 = jnp.repeat(state_mlstm_m[:, l], M_DV, axis=-1)[:, :, None, :]
        s0 = _block_diag_heads(state_gla_S[:, l])
        (hf, hb, c_fin, n_fin, m_fin), (of, ob, s_fin) = _scan_call(
            plan,
            [_mlstm2_call(n_ctx, zm, zs, zt, cn0, m0, gate_b[l], gate_bt[l], mconst),
             _gla_call(n_ctx, zg, zs, s0, w2p[l], b2p[l], gath, gtri, mcat,
                       mconst["kmask"], mconst["vmask"], gbd)],
            "mixer_scan")

        q, k, v, ckv = _mla_prep_call(zq, zs, cos, sin, kcos, a_q_norm_g[l][None, :], a_kv_norm_g[l][None, :],
                                      wuq[l], wuqs[l], wk[l], wv[l])
        kc, vc = _cache_kv_call(cache_mla_ckv[:, l].reshape(n_dec * past_len, A_DC),
                                krp_all[:, l].reshape(n_dec * past_len, LANES), wk[l], wv[l])
        a_ctx = _attn_ctx_call(q, k, v, n_ctx_rows)
        a_dec = _attn_dec_call(q, k, v, kc, vc, n_ctx_rows, n_dec, dec_len, past_len)

        x = _outproj_call(hf, hb, of, ob, a_ctx, a_dec, zm, zg, x, mods, m_norm_g[l][None, :], g_norm_g[l][None, :], avg,
                          wo_m[l], wo_g[l], wo_a[l], ln1_g[l][None, :], ln1_b[l][None, :], mod_map, alpha)
        x = _ffn_call(x, mods, wfg[l], wfu[l], wfd[l], ln2_g[l][None, :], ln2_b[l][None, :], mod_map, alpha)

        new_c.append(c_fin[:n_ctx])
        new_n.append(n_fin[:n_ctx, ..., 0])
        new_m.append(m_fin[:n_ctx, :, 0, ::M_DV])
        new_s.append(s_fin[:n_ctx])
        new_ckv.append(ckv[:n_ctx_rows].reshape(n_ctx, ctx_len, A_DC))
        new_kr.append(zs[:n_ctx_rows, KR_LO:KR_HI].reshape(n_ctx, ctx_len, A_DROPE))

    y_prompt = x[:n_ctx_rows].reshape(n_ctx, ctx_len, d)
    y_sample = x[n_ctx_rows:].reshape(n_dec, dec_len, d)
    st = lambda xs: jnp.stack(xs, axis=1)
    return (y_prompt, y_sample, st(new_c), st(new_n), st(new_m), st(new_s), st(new_ckv), st(new_kr))
```

```python
import functools

import numpy as np
import jax
import jax.numpy as jnp
from jax import lax
from jax.experimental import pallas as pl
from jax.experimental.pallas import tpu as pltpu

F32 = jnp.float32
BF16 = jnp.bfloat16

D_MODEL = 1024
GRID_W = 64
M_HEADS, M_DK, M_DV = 4, 32, 64
G_HEADS, G_DK, G_DV = 4, 32, 64
G_RANK = 16
G_TAU = 16.0
A_HEADS, A_DNOPE, A_DROPE, A_DV = 8, 64, 32, 64
A_DQ, A_DC = 256, 128
A_SCALE = (A_DNOPE + A_DROPE) ** -0.5
ROPE_BASE = 10000.0
IN_SPLITS = (M_HEADS * M_DK, M_HEADS * M_DK, M_HEADS * M_DV, M_HEADS * M_DV, 4 * M_HEADS,
             G_HEADS * G_DK, G_HEADS * G_DK, G_HEADS * G_DV, G_HEADS * G_DV, 2 * G_RANK,
             A_DQ, A_DC, A_DROPE)

LANES = 128
MXU_TILE = 256
VMEM_LIMIT = 56 * 1024 * 1024
ROW_BLOCK = 256
WIDE_ROWS = 512
S_CHUNK = 64
S_W = M_HEADS * M_DV
G_LEVELS = 6
G_SMALL_LEVELS = 3
A_HPAD = 128
A_KEY_TILE = 512
FFN_CHUNKS = 2

ZM_W = 768
ZG_W = 768
ZQ_W = 384
ZS_W = 256
Z_W = ZM_W + ZG_W + ZQ_W + ZS_W
KR_LO, KR_HI = A_DNOPE, A_DNOPE + A_DROPE

assert (M_HEADS, M_DK, M_DV) == (G_HEADS, G_DK, G_DV) and S_CHUNK == M_DV and M_HEADS * M_DK == LANES


def _cparams(sem):
    return pltpu.CompilerParams(dimension_semantics=sem, vmem_limit_bytes=VMEM_LIMIT)


def _f32dot(a, b):
    return jnp.dot(a, b, preferred_element_type=F32)


NT_DIMS = (((1,), (1,)), ((), ()))
TN_DIMS = (((0,), (0,)), ((), ()))


def _split(x):
    hi = x.astype(BF16)
    lo = (x - hi.astype(F32)).astype(BF16)
    return hi, lo


def _dot_c_x(c, x):
    hi, lo = _split(x)
    return _f32dot(c, hi) + _f32dot(c, lo)


def _dot_x_c(x, c):
    hi, lo = _split(x)
    return _f32dot(hi, c) + _f32dot(lo, c)


def _log_sigmoid(x):
    return jnp.minimum(x, 0.0) - jnp.log1p(jnp.exp(-jnp.abs(x)))


def _sigmoid(x):
    return 1.0 / (1.0 + jnp.exp(-x))


def _rms(x, g, eps=1e-6):
    return x * lax.rsqrt(jnp.mean(x * x, axis=-1, keepdims=True) + eps) * g


def _layer_norm(y, g, b, eps=1e-5):
    yc = y - jnp.mean(y, axis=-1, keepdims=True)
    return yc * lax.rsqrt(jnp.mean(yc * yc, axis=-1, keepdims=True) + eps) * g + b


def _with_ones_lane(v):
    lane = lax.broadcasted_iota(jnp.int32, v.shape, 1) % A_HPAD
    return jnp.where(lane == A_DV, 1.0, v)


def _x_specs(x, tm):
    if not isinstance(x, tuple):
        return [pl.BlockSpec((tm, x.shape[1]), lambda i: (i, 0))], [x]
    nb0 = x[0].shape[0] // tm
    return ([pl.BlockSpec((tm, x[0].shape[1]), lambda i: (jnp.minimum(i, nb0 - 1), 0)),
             pl.BlockSpec((tm, x[1].shape[1]), lambda i: (jnp.maximum(i - nb0, 0), 0))], list(x))


def _read_x(refs, nb0):
    if len(refs) == 1:
        return refs[0][...]
    return jnp.where(pl.program_id(0) < nb0, refs[0][...], refs[1][...])


def _ada_kernel(c_ref, w_ref, b_ref, o_ref):
    c = c_ref[...]
    s = c * _sigmoid(c)
    o_ref[0] = _f32dot(s.astype(BF16), w_ref[0].astype(BF16)) + b_ref[0]


def _ada_call(cond8, w_ada, b_ada):
    depth, d, n = w_ada.shape
    tn = 1536
    return pl.pallas_call(
        _ada_kernel,
        grid=(depth, n // tn),
        in_specs=[pl.BlockSpec((8, d), lambda l, j: (0, 0)),
                  pl.BlockSpec((1, d, tn), lambda l, j: (l, 0, j)),
                  pl.BlockSpec((1, 1, tn), lambda l, j: (l, 0, j))],
        out_specs=pl.BlockSpec((1, 8, tn), lambda l, j: (l, 0, j)),
        out_shape=jax.ShapeDtypeStruct((depth, 8, n), F32),
        compiler_params=_cparams(("arbitrary", "arbitrary")),
        name="ada_mod",
    )(cond8, w_ada, b_ada.reshape(depth, 1, n))


def _in_mla_kernel(*refs, n_x, nb0):
    x_refs = refs[:n_x]
    (mod_ref, w_ref, wgt_ref, cos_ref, sin_ref, kcos_ref, gq_ref, gkv_ref, wuq_ref, wuqs_ref, wk_ref, wv_ref,
     zm_ref, zg_ref, zs_ref, zt_ref, q_ref, k_ref, v_ref, ckv_ref) = refs[n_x:]
    d = D_MODEL
    mod = mod_ref[0]
    h = (_read_x(x_refs, nb0) * (1.0 + mod[:, d:2 * d]) + mod[:, 0:d]).astype(BF16)
    z = _f32dot(h, w_ref[...])
    zm_ref[...] = z[:, 0:ZM_W]
    zg_ref[...] = z[:, ZM_W:ZM_W + ZG_W]
    zs = z[:, ZM_W + ZG_W + ZQ_W:Z_W]
    zs_ref[...] = zs
    zt_ref[...] = lax.dot_general(wgt_ref[...], h, NT_DIMS, preferred_element_type=F32)
    zq = z[:, ZM_W + ZG_W:ZM_W + ZG_W + ZQ_W]
    cq = _rms(zq[:, 0:A_DQ], gq_ref[...]).astype(BF16)
    qn = _f32dot(cq, wuq_ref[...])
    qs = _f32dot(cq, wuqs_ref[...])
    cos = cos_ref[...]
    sin = sin_ref[...]
    for hd in range(A_HEADS):
        sl = slice(hd * A_HPAD, (hd + 1) * A_HPAD)
        q_ref[:, sl] = ((qn[:, sl] * cos + qs[:, sl] * sin) * A_SCALE).astype(BF16)
    ckv = _rms(zq[:, A_DQ:A_DQ + A_DC], gkv_ref[...])
    ckv_ref[...] = ckv
    ckvb = ckv.astype(BF16)
    kn = _f32dot(ckvb, wk_ref[...])
    v_ref[...] = _with_ones_lane(_f32dot(ckvb, wv_ref[...])).astype(BF16)
    kr = zs[:, 0:LANES] * kcos_ref[...] + zs[:, LANES:2 * LANES] * sin
    for hd in range(A_HEADS):
        sl = slice(hd * A_HPAD, (hd + 1) * A_HPAD)
        k_ref[:, sl] = (kn[:, sl] + kr).astype(BF16)


def _in_mla_call(x, mods, mod_map, w_in_p, w_gt, cos, sin, kcos, gq, gkv, wuq, wuqs, wk, wv):
    tm = WIDE_ROWS
    x_specs, x_ops = _x_specs(x, tm)
    nt = sum(a.shape[0] for a in x_ops)
    d = D_MODEL
    hw = A_HEADS * A_HPAD
    row = lambda i: (i, 0)
    const = lambda i: (0, 0)
    resident = dict(pipeline_mode=pl.Buffered(1))
    return pl.pallas_call(
        functools.partial(_in_mla_kernel, n_x=len(x_ops), nb0=x_ops[0].shape[0] // tm),
        grid=(nt // tm,),
        in_specs=x_specs
                 + [pl.BlockSpec((1, 1, 6 * d), lambda i: (mod_map(i * (tm // ROW_BLOCK)), 0, 0)),
                    pl.BlockSpec((d, Z_W), const, **resident), pl.BlockSpec((16, d), const),
                    pl.BlockSpec((tm, LANES), row), pl.BlockSpec((tm, LANES), row), pl.BlockSpec((tm, LANES), row),
                    pl.BlockSpec((1, A_DQ), const), pl.BlockSpec((1, A_DC), const),
                    pl.BlockSpec((A_DQ, hw), const, **resident), pl.BlockSpec((A_DQ, hw), const, **resident),
                    pl.BlockSpec((A_DC, hw), const, **resident), pl.BlockSpec((A_DC, hw), const, **resident)],
        out_specs=[pl.BlockSpec((tm, ZM_W), row), pl.BlockSpec((tm, ZG_W), row), pl.BlockSpec((tm, ZS_W), row),
                   pl.BlockSpec((16, tm), lambda i: (0, i)),
                   pl.BlockSpec((tm, hw), row), pl.BlockSpec((tm, hw), row), pl.BlockSpec((tm, hw), row),
                   pl.BlockSpec((tm, A_DC), row)],
        out_shape=[jax.ShapeDtypeStruct((nt, ZM_W), F32), jax.ShapeDtypeStruct((nt, ZG_W), F32),
                   jax.ShapeDtypeStruct((nt, ZS_W), F32), jax.ShapeDtypeStruct((16, nt), F32),
                   jax.ShapeDtypeStruct((nt, hw), BF16), jax.ShapeDtypeStruct((nt, hw), BF16),
                   jax.ShapeDtypeStruct((nt, hw), BF16), jax.ShapeDtypeStruct((nt, A_DC), F32)],
        compiler_params=_cparams(("parallel",)),
        name="in_proj_mla",
    )(*x_ops, mods, w_in_p, w_gt, cos, sin, kcos, gq, gkv, wuq, wuqs, wk, wv)


def _scan_consts():
    n = S_CHUNK
    t = np.arange(n)[:, None]
    s = np.arange(n)[None, :]
    tri = [(s <= t), (s >= t)]
    head_of = np.arange(S_W) // M_DV
    khead = np.arange(M_HEADS * M_DK) // M_DK
    kmask = (head_of[:, None] == khead[None, :]).astype(np.float32)
    vmask = (head_of[:, None] == head_of[None, :]).astype(np.float32)
    bd = (khead[:, None] == head_of[None, :]).astype(np.float32)
    sel, tritile, neg = [], [], []
    for d in (0, 1):
        io, fo = 8 * d, 8 * d + 4
        m = np.zeros((LANES, 3 * S_W), np.float32)
        for h in range(M_HEADS):
            m[fo + h, h * M_DV:(h + 1) * M_DV] = 1.0
            m[io + h, S_W + h * M_DV:S_W + (h + 1) * M_DV] = 1.0
            m[fo + h, 2 * S_W + h * M_DK:2 * S_W + (h + 1) * M_DK] = 1.0
            m[io + h, 2 * S_W + LANES + h * M_DK:2 * S_W + LANES + (h + 1) * M_DK] = 1.0
        sel.append(m)
        tritile.append(np.tile(tri[d].T.astype(np.float32), (1, M_HEADS)))
        neg.append(np.tile(np.where(tri[d], 0.0, -np.inf).astype(np.float32), (1, M_HEADS)))
    masks = [np.eye(n)]
    for lev in range(G_LEVELS):
        b = 1 << lev
        right = (t % (2 * b)) >= b
        same_parent = (t // (2 * b)) == (s // (2 * b))
        masks.append((same_parent & right & ((s % (2 * b)) < b)).astype(np.float32))
    flip = lambda m: m[::-1, ::-1]
    mcat = np.stack([np.stack([np.tile(m, (1, G_HEADS)) for m in masks]),
                     np.stack([np.tile(flip(m), (1, G_HEADS)) for m in masks])])
    gath = []
    for d in (0, 1):
        g = np.zeros((G_SMALL_LEVELS * n, n), np.float32)
        for lev in range(G_SMALL_LEVELS):
            for row in range(n):
                g[lev * n + row, _gla_boundary(row, 1 << lev, d)] = 1.0
        gath.append(g)
    f32 = lambda a: jnp.asarray(a, F32)
    b16 = lambda a: jnp.asarray(a, BF16)
    return dict(tri=b16(np.stack(tri).astype(np.float32)), sel=b16(np.stack(sel)), tritile=b16(np.stack(tritile)),
                neg=f32(np.stack(neg)), kmask=b16(kmask), vmask=b16(vmask), bd=f32(bd),
                bd2=f32(np.concatenate([bd, bd], axis=1)), mcat=f32(mcat), gath=b16(np.stack(gath)))


def _gla_boundary(row, b, d):
    pstart = (row // (2 * b)) * (2 * b)
    return pstart + b - 1 if d == 0 else pstart + b


def _scan_units():
    nch = ROW_BLOCK // S_CHUNK
    return [(d, ci if d == 0 else nch - 1 - ci) for ci in range(nch) for d in (0, 1)]


def _chunk_rows(c):
    return slice(c * S_CHUNK, (c + 1) * S_CHUNK)


def _mlstm_body(first_ref, seq_ref, bwd_ref,
                zmf_ref, zsf_ref, ztf_ref, zmb_ref, zsb_ref, ztb_ref,
                cn0_ref, m0_ref, gb_ref, gbt_ref,
                tri_ref, sel_ref, tritile_ref, neg_ref, kmask_ref, vmask_ref, bd2_ref,
                hf_ref, hb_ref, cout_ref, nout_ref, mout_ref,
                cns_ref, ms_ref):
    r = pl.program_id(0)
    n = S_CHUNK
    qk_w = M_HEADS * M_DK
    kmask = kmask_ref[...]
    vmask = vmask_ref[...]
    bd2 = bd2_ref[...]
    units = _scan_units()
    rows = _chunk_rows
    zm_refs, zs_refs, zt_refs, h_refs = (zmf_ref, zmb_ref), (zsf_ref, zsb_ref), (ztf_ref, ztb_ref), (hf_ref, hb_ref)
    fo = lambda d: 8 * d + 4
    last = lambda d: n - 1 if d == 0 else 0
    lane = lax.broadcasted_iota(jnp.int32, (n, LANES), 1)
    head_row = lax.broadcasted_iota(jnp.int32, (1, S_W), 1) // M_DV

    def per_unit(f):
        return [f(d, c) for d, c in units]

    def dot_split(c_left, x_split, c_right):
        hi, lo = x_split
        if c_left is not None:
            return _f32dot(c_left, hi) + _f32dot(c_left, lo)
        return _f32dot(hi, c_right) + _f32dot(lo, c_right)

    q = per_unit(lambda d, c: zm_refs[d][rows(c), 0:qk_w].astype(BF16))
    k = per_unit(lambda d, c: zm_refs[d][rows(c), qk_w:2 * qk_w] * (M_DK ** -0.5))
    v = per_unit(lambda d, c: zm_refs[d][rows(c), 2 * qk_w:2 * qk_w + S_W])
    graw = per_unit(lambda d, c: zs_refs[d][rows(c), :] + gb_ref[...])
    graw_t = per_unit(lambda d, c: zt_refs[d][:, rows(c)] + gbt_ref[...])
    lf = [_split(_log_sigmoid(g)) for g in graw]
    lf_t = [_split(_log_sigmoid(g)) for g in graw_t]
    bcol = [dot_split(tri_ref[d], s, None) for (d, _), s in zip(units, lf)]
    xs = [_split(jnp.where((lane >= fo(d)) & (lane < fo(d) + M_HEADS), b, g))
          for (d, _), b, g in zip(units, bcol, graw)]
    ex = [dot_split(None, s, sel_ref[d]) for (d, _), s in zip(units, xs)]
    b256 = [e[:, 0:S_W] for e in ex]
    i256 = [e[:, S_W:2 * S_W] for e in ex]
    b128 = [e[:, 2 * S_W:2 * S_W + LANES] for e in ex]
    i128 = [e[:, 2 * S_W + LANES:] for e in ex]
    rows_t = [dot_split(None, s, tritile_ref[d]) for (d, _), s in zip(units, lf_t)]
    b_row = []
    for (d, _), rt in zip(units, rows_t):
        br = jnp.zeros((1, S_W), F32)
        for h in range(M_HEADS):
            br = jnp.where(head_row == h, rt[fo(d) + h:fo(d) + h + 1, :], br)
        b_row.append(br)
    decay = [jnp.exp(b - br + neg_ref[d]) for (d, _), b, br in zip(units, b256, b_row)]
    k_bd = [jnp.concatenate([ki.astype(BF16)] * M_HEADS, axis=0) * kmask for ki in k]
    qk = [lax.dot_general(qi, ki, NT_DIMS, preferred_element_type=F32) for qi, ki in zip(q, k_bd)]
    p = [(a * b).astype(BF16) for a, b in zip(qk, decay)]
    i_max = [jnp.max(i, axis=0, keepdims=True) for i in i256]
    scale = [jnp.exp(i - im) for i, im in zip(i256, i_max)]
    ve = [jnp.concatenate([jnp.concatenate([(vi * sc).astype(BF16)] * M_HEADS, axis=0) * vmask,
                           jnp.concatenate([sc.astype(BF16)] * M_HEADS, axis=0) * vmask], axis=1)
          for vi, sc in zip(v, scale)]
    nd = [_f32dot(pi, vei) for pi, vei in zip(p, ve)]
    bl256 = [b[last(d):last(d) + 1, :] for (d, _), b in zip(units, b256)]
    g_max = [jnp.max(bl - b + i, axis=0, keepdims=True) for bl, b, i in zip(bl256, b256, i256)]
    g128 = [b[last(d):last(d) + 1, :] - b + i for (d, _), b, i in zip(units, b128, i128)]
    w = [jnp.exp(g - jnp.max(g, axis=0, keepdims=True)) for g in g128]
    kw = [(ki * wi).astype(BF16) for ki, wi in zip(k, w)]
    ones = jnp.ones((n, S_W), BF16)
    upd = [lax.dot_general(kwi, jnp.concatenate([vi.astype(BF16), ones], axis=1), TN_DIMS,
                           preferred_element_type=F32) for kwi, vi in zip(kw, v)]
    fresh = first_ref[r] != 0
    zero = first_ref[r] == 2
    state = [(jnp.where(fresh, jnp.where(zero, 0.0, cn0_ref[0, d]), cns_ref[d]),
              jnp.where(fresh, jnp.where(zero, 0.0, m0_ref[0, d]), ms_ref[d])) for d in (0, 1)]
    outs = []
    for i, (d, c) in enumerate(units):
        cn, m = state[d]
        qcn = _f32dot(q[i], cn.astype(BF16))
        a = b256[i] + m
        sig = jnp.maximum(i_max[i], a)
        f1 = jnp.exp(i_max[i] - sig)
        f2 = jnp.exp(a - sig)
        num = f1 * nd[i][:, 0:S_W] + f2 * qcn[:, 0:S_W]
        den = f1 * nd[i][:, S_W:] + f2 * qcn[:, S_W:]
        outs.append(num / jnp.maximum(jnp.abs(den), jnp.exp(-sig)))
        m_new = jnp.maximum(bl256[i] + m, g_max[i])
        alpha = jnp.exp(bl256[i] + m - m_new)
        beta = jnp.exp(g_max[i] - m_new)
        state[d] = (jnp.concatenate([alpha, alpha], axis=1) * cn
                    + (jnp.concatenate([beta, beta], axis=1) * bd2) * upd[i], m_new)
    for (d, c), o in zip(units, outs):
        h_refs[d][rows(c), :] = o
    for d in (0, 1):
        cn, m = state[d]
        cns_ref[d], ms_ref[d] = cn, m
        mout_ref[0, d] = m
        for h in range(M_HEADS):
            blk = cn[h * M_DK:(h + 1) * M_DK, :]
            cout_ref[0, d, h] = blk[:, h * M_DV:(h + 1) * M_DV]
            nout_ref[0, d, h] = blk[:, S_W + h * M_DV:S_W + (h + 1) * M_DV]


def _mlstm_part(n_ctx, zm, zs, zt, cn0, m0, gb, gbt, sc):
    nt = zm.shape[0]
    kd = M_HEADS * M_DK
    n = S_CHUNK
    fwd = lambda r, first, seq, bwd: (r, 0)
    bwd_ = lambda r, first, seq, bwd: (bwd[r], 0)
    fwd_t = lambda r, first, seq, bwd: (0, r)
    bwd_t = lambda r, first, seq, bwd: (0, bwd[r])
    seq4 = lambda r, first, seq, bwd: (jnp.maximum(seq[r] - n_ctx, 0), 0, 0, 0)
    out4 = lambda r, first, seq, bwd: (jnp.minimum(seq[r], n_ctx), 0, 0, 0)
    out5 = lambda r, first, seq, bwd: (jnp.minimum(seq[r], n_ctx), 0, 0, 0, 0)
    const = lambda r, first, seq, bwd: (0, 0)
    const3 = lambda r, first, seq, bwd: (0, 0, 0)
    return dict(
        body=_mlstm_body,
        in_specs=[pl.BlockSpec((ROW_BLOCK, 512), fwd), pl.BlockSpec((ROW_BLOCK, LANES), fwd),
                  pl.BlockSpec((16, ROW_BLOCK), fwd_t),
                  pl.BlockSpec((ROW_BLOCK, 512), bwd_), pl.BlockSpec((ROW_BLOCK, LANES), bwd_),
                  pl.BlockSpec((16, ROW_BLOCK), bwd_t),
                  pl.BlockSpec((1, 2, kd, 2 * S_W), seq4), pl.BlockSpec((1, 2, 1, S_W), seq4),
                  pl.BlockSpec((1, LANES), const), pl.BlockSpec((16, 1), const),
                  pl.BlockSpec((2, n, n), const3), pl.BlockSpec((2, LANES, 3 * S_W), const3),
                  pl.BlockSpec((2, n, S_W), const3), pl.BlockSpec((2, n, S_W), const3),
                  pl.BlockSpec((S_W, kd), const), pl.BlockSpec((S_W, S_W), const),
                  pl.BlockSpec((kd, 2 * S_W), const)],
        out_specs=[pl.BlockSpec((ROW_BLOCK, S_W), fwd), pl.BlockSpec((ROW_BLOCK, S_W), bwd_),
                   pl.BlockSpec((1, 2, M_HEADS, M_DK, M_DV), out5), pl.BlockSpec((1, 2, M_HEADS, M_DK, M_DV), out5),
                   pl.BlockSpec((1, 2, 1, S_W), out4)],
        scratch_shapes=[pltpu.VMEM((2, kd, 2 * S_W), F32), pltpu.VMEM((2, 1, S_W), F32)],
        out_shape=[jax.ShapeDtypeStruct((nt, S_W), F32), jax.ShapeDtypeStruct((nt, S_W), F32),
                   jax.ShapeDtypeStruct((n_ctx + 1, 2, M_HEADS, M_DK, M_DV), F32),
                   jax.ShapeDtypeStruct((n_ctx + 1, 2, M_HEADS, M_DK, M_DV), F32),
                   jax.ShapeDtypeStruct((n_ctx + 1, 2, 1, S_W), F32)],
        operands=[zm, zs, zt, zm, zs, zt, cn0, m0, gb, gbt,
                  sc["tri"], sc["sel"], sc["tritile"], sc["neg"], sc["kmask"], sc["vmask"], sc["bd2"]])


def _gla_body(first_ref, seq_ref, bwd_ref,
              zgf_ref, zsf_ref, zgb_ref, zsb_ref, s0_ref, w2_ref, b2_ref,
              tri_ref, gath_ref, mcat_ref, kmask_ref, vmask_ref, bd_ref,
              of_ref, ob_ref, sout_ref, ss_ref):
    r = pl.program_id(0)
    n = S_CHUNK
    kd = G_HEADS * G_DK
    vd = G_HEADS * G_DV
    ones_v = jnp.ones((n, vd), BF16)
    kmask = kmask_ref[...]
    vmask = vmask_ref[...]
    bd = bd_ref[...]
    units = _scan_units()
    rows = _chunk_rows
    zg_refs, zs_refs, o_refs = (zgf_ref, zgb_ref), (zsf_ref, zsb_ref), (of_ref, ob_ref)

    def per_unit(f):
        return [f(d, c) for d, c in units]

    def bd_k(kt):
        return jnp.concatenate([kt] * G_HEADS, axis=0) * kmask

    q = per_unit(lambda d, c: zg_refs[d][rows(c), 0:kd] * (G_DK ** -0.5))
    k = per_unit(lambda d, c: zg_refs[d][rows(c), kd:2 * kd])
    v = per_unit(lambda d, c: zg_refs[d][rows(c), 2 * kd:2 * kd + vd].astype(BF16))
    x = per_unit(lambda d, c: _f32dot(zs_refs[d][rows(c), :].astype(BF16), w2_ref[d]) + b2_ref[d])
    lg = [_log_sigmoid(xi) * (1.0 / G_TAU) for xi in x]
    lg_split = [_split(l) for l in lg]
    bc = [_f32dot(tri_ref[d], hi) + _f32dot(tri_ref[d], lo)
          for (d, _), (hi, lo) in zip(units, lg_split)]
    bc_split = [_split(b) for b in bc]
    r_small = [_f32dot(gath_ref[d], hi) + _f32dot(gath_ref[d], lo)
               for (d, _), (hi, lo) in zip(units, bc_split)]
    qb = [qi.astype(BF16) for qi in q]
    kbd = [bd_k(ki.astype(BF16)) for ki in k]
    att = [mcat_ref[d, 0] * lax.dot_general(qi, ki, NT_DIMS, preferred_element_type=F32)
           for (d, _), qi, ki in zip(units, qb, kbd)]
    for lev in range(G_LEVELS):
        b = 1 << lev
        if lev < G_SMALL_LEVELS:
            r_lev = [rs[lev * n:(lev + 1) * n] for rs in r_small]
        else:
            r_lev = [jnp.concatenate(
                [jnp.broadcast_to(bci[_gla_boundary(p0, b, d):_gla_boundary(p0, b, d) + 1, :], (2 * b, kd))
                 for p0 in range(0, n, 2 * b)], axis=0) for (d, _), bci in zip(units, bc)]
        e_lev = [jnp.exp(-jnp.abs(bci - ri)) for bci, ri in zip(bc, r_lev)]
        qt = [(qi * ei).astype(BF16) for qi, ei in zip(q, e_lev)]
        kt = [bd_k((ki * ei).astype(BF16)) for ki, ei in zip(k, e_lev)]
        p = [lax.dot_general(qi, ki, NT_DIMS, preferred_element_type=F32) for qi, ki in zip(qt, kt)]
        att = [ai + mcat_ref[d, lev + 1] * pi for (d, _), ai, pi in zip(units, att, p)]
    v_bd = [jnp.concatenate([vi] * G_HEADS, axis=0) * vmask for vi in v]
    intra = [_f32dot(ai.astype(BF16), vi) for ai, vi in zip(att, v_bd)]
    q_inc = [(qi * jnp.exp(bci)).astype(BF16) for qi, bci in zip(q, bc)]
    k_suf = [(ki * jnp.exp(bci[(n - 1 if d == 0 else 0):(n if d == 0 else 1), :] - bci)).astype(BF16)
             for (d, _), ki, bci in zip(units, k, bc)]
    decay = [jnp.exp(lax.dot_general(hi, ones_v, TN_DIMS, preferred_element_type=F32)
                     + lax.dot_general(lo, ones_v, TN_DIMS, preferred_element_type=F32))
             for hi, lo in lg_split]
    upd = [bd * lax.dot_general(ki, vi, TN_DIMS, preferred_element_type=F32) for ki, vi in zip(k_suf, v)]
    fresh = first_ref[r] != 0
    zero = first_ref[r] == 2
    states = [jnp.where(fresh, jnp.where(zero, 0.0, s0_ref[0, d]), ss_ref[d]) for d in (0, 1)]
    outs = []
    for i, (d, c) in enumerate(units):
        inter = _f32dot(q_inc[i], states[d].astype(BF16))
        outs.append(inter + intra[i])
        states[d] = decay[i] * states[d] + upd[i]
    for (d, c), o in zip(units, outs):
        o_refs[d][rows(c), :] = o
    for d in (0, 1):
        ss_ref[d] = states[d]
        for h in range(G_HEADS):
            sout_ref[0, d, h] = states[d][h * G_DK:(h + 1) * G_DK, h * G_DV:(h + 1) * G_DV]


def _gla_part(n_ctx, zg, zs, s0, w2p, b2p, sc):
    nt = zg.shape[0]
    kd = G_HEADS * G_DK
    vd = G_HEADS * G_DV
    n = S_CHUNK
    fwd = lambda r, first, seq, bwd: (r, 0)
    bwd_ = lambda r, first, seq, bwd: (bwd[r], 0)
    seq4 = lambda r, first, seq, bwd: (jnp.maximum(seq[r] - n_ctx, 0), 0, 0, 0)
    out5 = lambda r, first, seq, bwd: (jnp.minimum(seq[r], n_ctx), 0, 0, 0, 0)
    const = lambda r, first, seq, bwd: (0, 0)
    const3 = lambda r, first, seq, bwd: (0, 0, 0)
    const4 = lambda r, first, seq, bwd: (0, 0, 0, 0)
    return dict(
        body=_gla_body,
        in_specs=[pl.BlockSpec((ROW_BLOCK, 512), fwd), pl.BlockSpec((ROW_BLOCK, LANES), fwd),
                  pl.BlockSpec((ROW_BLOCK, 512), bwd_), pl.BlockSpec((ROW_BLOCK, LANES), bwd_),
                  pl.BlockSpec((1, 2, kd, vd), seq4),
                  pl.BlockSpec((2, LANES, kd), const3), pl.BlockSpec((2, 1, kd), const3),
                  pl.BlockSpec((2, n, n), const3),
                  pl.BlockSpec((2, G_SMALL_LEVELS * n, n), const3),
                  pl.BlockSpec((2, G_LEVELS + 1, n, G_HEADS * n), const4),
                  pl.BlockSpec((G_HEADS * n, kd), const), pl.BlockSpec((G_HEADS * n, vd), const),
                  pl.BlockSpec((kd, vd), const)],
        out_specs=[pl.BlockSpec((ROW_BLOCK, vd), fwd), pl.BlockSpec((ROW_BLOCK, vd), bwd_),
                   pl.BlockSpec((1, 2, G_HEADS, G_DK, G_DV), out5)],
        scratch_shapes=[pltpu.VMEM((2, kd, vd), F32)],
        out_shape=[jax.ShapeDtypeStruct((nt, vd), F32), jax.ShapeDtypeStruct((nt, vd), F32),
                   jax.ShapeDtypeStruct((n_ctx + 1, 2, G_HEADS, G_DK, G_DV), F32)],
        operands=[zg, zs, zg, zs, s0, w2p, b2p, sc["tri"], sc["gath"], sc["mcat"],
                  sc["kmask"], sc["vmask"], sc["bd"]])


def _scan_call(plan, parts, name):
    n_in = [len(p["operands"]) for p in parts]
    n_out = [len(p["out_specs"]) for p in parts]
    n_scr = [len(p["scratch_shapes"]) for p in parts]

    def kern(first_ref, seq_ref, bwd_ref, *refs):
        ins, outs, scrs = refs[:sum(n_in)], refs[sum(n_in):sum(n_in) + sum(n_out)], refs[sum(n_in) + sum(n_out):]
        oi = oo = os_ = 0
        for p, a, b, c in zip(parts, n_in, n_out, n_scr):
            p["body"](first_ref, seq_ref, bwd_ref, *ins[oi:oi + a], *outs[oo:oo + b], *scrs[os_:os_ + c])
            oi, oo, os_ = oi + a, oo + b, os_ + c

    nb = plan["first"].shape[0]
    grid_spec = pltpu.PrefetchScalarGridSpec(
        num_scalar_prefetch=3,
        grid=(nb,),
        in_specs=[s for p in parts for s in p["in_specs"]],
        out_specs=[s for p in parts for s in p["out_specs"]],
        scratch_shapes=[s for p in parts for s in p["scratch_shapes"]],
    )
    outs = pl.pallas_call(
        kern,
        grid_spec=grid_spec,
        out_shape=[s for p in parts for s in p["out_shape"]],
        compiler_params=_cparams(("arbitrary",)),
        name=name,
    )(plan["first"], plan["seq"], plan["bwd"], *[o for p in parts for o in p["operands"]])
    res, oo = [], 0
    for b in n_out:
        res.append(outs[oo:oo + b])
        oo += b
    return res


def _cache_kv_kernel(ckv_ref, kr_ref, wk_ref, wv_ref, k_ref, v_ref):
    ckvb = ckv_ref[...].astype(BF16)
    kn = _f32dot(ckvb, wk_ref[...])
    v_ref[...] = _with_ones_lane(_f32dot(ckvb, wv_ref[...])).astype(BF16)
    kr = kr_ref[...]
    for h in range(A_HEADS):
        sl = slice(h * A_HPAD, (h + 1) * A_HPAD)
        k_ref[:, sl] = (kn[:, sl] + kr).astype(BF16)


def _cache_kv_call(ckv, krp, wk, wv):
    nt = ckv.shape[0]
    tm = ROW_BLOCK
    hw = A_HEADS * A_HPAD
    row = lambda i: (i, 0)
    const = lambda i: (0, 0)
    return pl.pallas_call(
        _cache_kv_kernel,
        grid=(nt // tm,),
        in_specs=[pl.BlockSpec((tm, A_DC), row), pl.BlockSpec((tm, LANES), row),
                  pl.BlockSpec((A_DC, hw), const), pl.BlockSpec((A_DC, hw), const)],
        out_specs=[pl.BlockSpec((tm, hw), row), pl.BlockSpec((tm, hw), row)],
        out_shape=[jax.ShapeDtypeStruct((nt, hw), BF16), jax.ShapeDtypeStruct((nt, hw), BF16)],
        compiler_params=_cparams(("parallel",)),
        name="mla_cache_kv",
    )(ckv, krp, wk, wv)


def _attn_kernel(*refs, n_seg, heads_per_group):
    q_ref = refs[0]
    kv_refs = refs[1:1 + 2 * n_seg]
    o_ref = refs[1 + 2 * n_seg]
    tiles = []
    for j in range(n_seg):
        nk = kv_refs[2 * j].shape[0]
        tk = min(nk, A_KEY_TILE)
        tiles += [(j, t0, tk) for t0 in range(0, nk, tk)]
    head = lambda h: slice(h * A_HPAD, (h + 1) * A_HPAD)

    def scores(h, tile):
        j, t0, tk = tile
        return lax.dot_general(q_ref[:, head(h)], kv_refs[2 * j][t0:t0 + tk, head(h)], NT_DIMS,
                               preferred_element_type=F32)

    def lane_max(acc, s):
        for c0 in range(0, s.shape[1], LANES):
            part = s[:, c0:c0 + LANES]
            acc = part if acc is None else jnp.maximum(acc, part)
        return acc

    groups = [list(range(g0, min(g0 + heads_per_group, A_HEADS))) for g0 in range(0, A_HEADS, heads_per_group)]
    s_cur = {h: [scores(h, t) for t in tiles] for h in groups[0]}
    for gi, grp in enumerate(groups):
        nxt = groups[gi + 1] if gi + 1 < len(groups) else []
        m = {}
        for h in grp:
            mx = None
            for s in s_cur[h]:
                mx = lane_max(mx, s)
            m[h] = jnp.max(mx, axis=1, keepdims=True)
        s_next = {h: [] for h in nxt}
        acc = {h: None for h in grp}
        for ti, tile in enumerate(tiles):
            j, t0, tk = tile
            for h in nxt:
                s_next[h].append(scores(h, tile))
            p = {h: jnp.exp((s_cur[h][ti] - m[h]).astype(BF16)) for h in grp}
            for h in grp:
                pv = _f32dot(p[h], kv_refs[2 * j + 1][t0:t0 + tk, head(h)])
                acc[h] = pv if acc[h] is None else acc[h] + pv
        for h in grp:
            o_ref[:, head(h)] = (acc[h] / acc[h][:, A_DV:A_DV + 1]).astype(BF16)
        s_cur = s_next


def _attn_ctx_call(q, k, v, n_rows):
    hw = A_HEADS * A_HPAD
    tm = ROW_BLOCK
    row = lambda i: (i, 0)
    return pl.pallas_call(
        functools.partial(_attn_kernel, n_seg=1, heads_per_group=A_HEADS),
        grid=(n_rows // tm,),
        in_specs=[pl.BlockSpec((tm, hw), row)] * 3,
        out_specs=pl.BlockSpec((tm, hw), row),
        out_shape=jax.ShapeDtypeStruct((n_rows, hw), BF16),
        compiler_params=_cparams(("parallel",)),
        name="mla_attn_ctx",
    )(q, k, v)


def _attn_dec_call(q, k, v, kc, vc, n_ctx_rows, n_dec, dec_len, past_len):
    hw = A_HEADS * A_HPAD
    tm = ROW_BLOCK
    nq = dec_len // tm
    assert n_ctx_rows % dec_len == 0
    qmap = lambda b, i: (n_ctx_rows // tm + b * nq + i, 0)
    own = lambda b, i: (n_ctx_rows // dec_len + b, 0)
    return pl.pallas_call(
        functools.partial(_attn_kernel, n_seg=2, heads_per_group=2),
        grid=(n_dec, nq),
        in_specs=[pl.BlockSpec((tm, hw), qmap),
                  pl.BlockSpec((past_len, hw), lambda b, i: (b, 0)),
                  pl.BlockSpec((past_len, hw), lambda b, i: (b, 0)),
                  pl.BlockSpec((dec_len, hw), own),
                  pl.BlockSpec((dec_len, hw), own)],
        out_specs=pl.BlockSpec((tm, hw), lambda b, i: (b * nq + i, 0)),
        out_shape=jax.ShapeDtypeStruct((n_dec * dec_len, hw), BF16),
        compiler_params=_cparams(("parallel", "arbitrary")),
        name="mla_attn_dec",
    )(q, kc, vc, k, v)


def _out_ffn_kernel(*refs, n_x, nb0, nb_ctx, alpha):
    (hf_ref, hb_ref, of_ref, ob_ref, actx_ref, adec_ref, mo_ref, gg_ref) = refs[:8]
    x_refs = refs[8:8 + n_x]
    (mod_ref, mng_ref, gng_ref, avg_ref, wm_ref, wg_ref, wa_ref, l1g_ref, l1b_ref,
     wfg_ref, wfu_ref, wfd_ref, l2g_ref, l2b_ref, o_ref) = refs[8 + n_x:]
    d = D_MODEL
    avg = avg_ref[...]
    mod = mod_ref[0]
    is_ctx = pl.program_id(0) < nb_ctx
    x_in = _read_x(x_refs, nb0)
    tm = o_ref.shape[0]
    parts = [slice(p0, p0 + ROW_BLOCK) for p0 in range(0, tm, ROW_BLOCK)]
    each = lambda f: [f(s) for s in parts]
    a = each(lambda s: jnp.where(is_ctx, actx_ref[s, :], adec_ref[s, :]))
    hm = each(lambda s: hf_ref[s, :] + hb_ref[s, :])
    og = each(lambda s: of_ref[s, :] + ob_ref[s, :])
    hc = [x - _dot_x_c(x, avg) for x in hm]
    hvar = [_dot_x_c(x * x, avg) for x in hc]
    gms = [_dot_x_c(x * x, avg) for x in og]
    m_out = [x * lax.rsqrt(v + 1e-6) * mng_ref[...] * _sigmoid(mo_ref[s, :]) for x, v, s in zip(hc, hvar, parts)]
    gg = each(lambda s: gg_ref[s, :])
    g_out = [x * lax.rsqrt(v + 1e-6) * gng_ref[...] * (g * _sigmoid(g)) for x, v, g in zip(og, gms, gg)]
    mix = [_f32dot(m.astype(BF16), wm_ref[...]) + _f32dot(g.astype(BF16), wg_ref[...]) + _f32dot(ai, wa_ref[...])
           for m, g, ai in zip(m_out, g_out, a)]
    y = [alpha * x_in[s, :] + mod[:, 2 * d:3 * d] * mi for s, mi in zip(parts, mix)]
    x1 = jnp.concatenate([_layer_norm(yi, l1g_ref[...], l1b_ref[...]) for yi in y], axis=0)
    hh = (x1 * (1.0 + mod[:, 4 * d:5 * d]) + mod[:, 3 * d:4 * d]).astype(BF16)
    f = wfg_ref.shape[1]
    tiles = -(-f // MXU_TILE)
    edges = [min(f, MXU_TILE * (tiles * c // FFN_CHUNKS)) for c in range(FFN_CHUNKS)] + [f]
    acc = jnp.zeros(x1.shape, F32)
    for c in range(FFN_CHUNKS):
        sl = slice(edges[c], edges[c + 1])
        g = _f32dot(hh, wfg_ref[:, sl])
        u = _f32dot(hh, wfu_ref[:, sl])
        act = (g * _sigmoid(g) * u).astype(BF16)
        acc = acc + _f32dot(act, wfd_ref[sl, :])
    y2 = alpha * x1 + mod[:, 5 * d:6 * d] * acc
    o_ref[...] = _layer_norm(y2, l2g_ref[...], l2b_ref[...])


def _out_ffn_call(hf, hb, of, ob, a_ctx, a_dec, zm, zg, x, mods, mod_map, mng, gng, avg, wm, wg, wa, l1g, l1b,
                  wfg, wfu, wfd, l2g, l2b, alpha):
    tm = WIDE_ROWS
    x_specs, x_ops = _x_specs(x, tm)
    nt = sum(a.shape[0] for a in x_ops)
    d = D_MODEL
    f = wfg.shape[1]
    hw = M_HEADS * M_DV
    ahw = A_HEADS * A_HPAD
    nb_ctx = a_ctx.shape[0] // tm
    row = lambda i: (i, 0)
    const = lambda i: (0, 0)
    gate_col = lambda i: (i, 2)
    resident = dict(pipeline_mode=pl.Buffered(1))
    return pl.pallas_call(
        functools.partial(_out_ffn_kernel, n_x=len(x_ops), nb0=x_ops[0].shape[0] // tm, nb_ctx=nb_ctx, alpha=alpha),
        grid=(nt // tm,),
        in_specs=[pl.BlockSpec((tm, hw), row)] * 4
                 + [pl.BlockSpec((tm, ahw), lambda i: (jnp.minimum(i, nb_ctx - 1), 0)),
                    pl.BlockSpec((tm, ahw), lambda i: (jnp.maximum(i - nb_ctx, 0), 0)),
                    pl.BlockSpec((tm, hw), gate_col), pl.BlockSpec((tm, hw), gate_col)]
                 + x_specs
                 + [pl.BlockSpec((1, 1, 6 * d), lambda i: (mod_map(i * (tm // ROW_BLOCK)), 0, 0)),
                    pl.BlockSpec((1, hw), const), pl.BlockSpec((1, hw), const),
                    pl.BlockSpec((hw, hw), const),
                    pl.BlockSpec((hw, d), const, **resident), pl.BlockSpec((hw, d), const, **resident),
                    pl.BlockSpec((ahw, d), const, **resident),
                    pl.BlockSpec((1, d), const), pl.BlockSpec((1, d), const),
                    pl.BlockSpec((d, f), const, **resident), pl.BlockSpec((d, f), const, **resident),
                    pl.BlockSpec((f, d), const, **resident),
                    pl.BlockSpec((1, d), const), pl.BlockSpec((1, d), const)],
        out_specs=pl.BlockSpec((tm, d), row),
        out_shape=jax.ShapeDtypeStruct((nt, d), F32),
        compiler_params=_cparams(("parallel",)),
        name="out_proj_ffn",
    )(hf, hb, of, ob, a_ctx, a_dec, zm, zg, *x_ops, mods, mng, gng, avg, wm, wg, wa, l1g, l1b,
      wfg, wfu, wfd, l2g, l2b)


def _block_diag_heads(t):
    h = t.shape[-3]
    eye = jnp.eye(h, dtype=t.dtype)
    out = jnp.einsum('...hkv,hg->...hkgv', t, eye)
    return out.reshape(t.shape[:-3] + (h * t.shape[-2], h * t.shape[-1]))


def _rope_tables(n_ctx_rows, n_dec, dec_len):
    half = A_DROPE // 4
    inv = ROPE_BASE ** (-jnp.arange(half, dtype=F32) / half)
    pos = jnp.arange(dec_len, dtype=jnp.int32)
    rows = (pos // GRID_W).astype(F32)[:, None] * inv[None, :]
    cols = (pos % GRID_W).astype(F32)[:, None] * inv[None, :]
    cr, sr, cc, sc = jnp.cos(rows), jnp.sin(rows), jnp.cos(cols), jnp.sin(cols)
    cos32 = jnp.concatenate([cr, cr, cc, cc], axis=1)
    sin32 = jnp.concatenate([-sr, sr, -sc, sc], axis=1)
    ones_lo = jnp.ones((dec_len, KR_LO), F32)
    ones_hi = jnp.ones((dec_len, A_HPAD - KR_HI), F32)
    cos = jnp.concatenate([ones_lo, cos32, ones_hi], axis=1)
    sin = jnp.concatenate([0 * ones_lo, sin32, 0 * ones_hi], axis=1)
    kcos = jnp.concatenate([0 * ones_lo, cos32, 0 * ones_hi], axis=1)
    lane = jnp.arange(A_HPAD)
    rope_lane = ((lane >= KR_LO) & (lane < KR_HI)).astype(F32)[None, :]
    ctx_cos = jnp.ones((n_ctx_rows, A_HPAD), F32)
    ctx_sin = jnp.zeros((n_ctx_rows, A_HPAD), F32)
    ctx_kcos = jnp.broadcast_to(rope_lane, (n_ctx_rows, A_HPAD))
    tile = lambda t: jnp.tile(t, (n_dec, 1))
    return (jnp.concatenate([ctx_cos, tile(cos)]), jnp.concatenate([ctx_sin, tile(sin)]),
            jnp.concatenate([ctx_kcos, tile(kcos)]))


def _rope_partner(w):
    e = A_DROPE // 4
    return jnp.concatenate([w[..., e:2 * e], w[..., 0:e], w[..., 3 * e:4 * e], w[..., 2 * e:3 * e]], axis=-1)


def kernel(x_prompt, x_sample, state_mlstm_C, state_mlstm_n, state_mlstm_m, state_gla_S, cache_mla_ckv, cache_mla_krope, c, c_ctx, w_ada, b_ada, w_in, m_gate_b, m_norm_g, g_w2, g_b2, g_norm_g, a_q_norm_g, a_kv_norm_g, a_w_uq, a_w_ukv, w_out, ln1_g, ln1_b, w_ffn_gate, w_ffn_up, w_ffn_down, ln2_g, ln2_b):
    n_ctx, ctx_len, d = x_prompt.shape
    n_dec, dec_len, _ = x_sample.shape
    depth = w_in.shape[0]
    past_len = cache_mla_ckv.shape[2]
    assert d == D_MODEL and ctx_len == ROW_BLOCK and dec_len % ROW_BLOCK == 0 and past_len % 8 == 0
    assert (n_ctx * ctx_len) % WIDE_ROWS == 0 and dec_len % WIDE_ROWS == 0
    assert n_dec + 1 <= 8
    n_ctx_rows = n_ctx * ctx_len
    n_dec_rows = n_dec * dec_len
    nt = n_ctx_rows + n_dec_rows
    nb_ctx = n_ctx_rows // ROW_BLOCK
    nb_seq = dec_len // ROW_BLOCK
    alpha = (2.0 * depth) ** 0.25

    blocks = np.arange(nt // ROW_BLOCK)
    dec_b = np.maximum(blocks - nb_ctx, 0)
    is_ctx = blocks < nb_ctx
    plan = {
        "first": jnp.asarray(np.where(is_ctx, 2, (dec_b % nb_seq) == 0).astype(np.int32)),
        "seq": jnp.asarray(np.where(is_ctx, blocks, n_ctx + dec_b // nb_seq).astype(np.int32)),
        "bwd": jnp.asarray(np.where(is_ctx, blocks,
                                    nb_ctx + (dec_b // nb_seq) * nb_seq + nb_seq - 1 - dec_b % nb_seq).astype(np.int32)),
    }

    def mod_map_for(layer):
        def mod_map(i):
            return layer * 8 + jnp.where(i < nb_ctx, 0, 1 + (i - nb_ctx) // nb_seq)
        return mod_map

    cond8 = jnp.zeros((8, d), F32).at[0].set(c_ctx).at[1:1 + n_dec].set(c)
    mods = _ada_call(cond8, w_ada, b_ada).reshape(depth * 8, 1, 6 * d)

    pts = [int(p) for p in np.cumsum(IN_SPLITS)[:-1]]
    (wmq, wmk, wmv, wmo, wmg, wgq, wgk, wgv, wgg, wga, wacq, wackv, wakr) = jnp.split(w_in, pts, axis=-1)
    zpad = lambda n: jnp.zeros((depth, d, n), F32)
    s1 = jnp.concatenate([wmg, wga, zpad(KR_LO - 48), wakr, zpad(LANES - KR_HI)], axis=-1)
    s2 = jnp.concatenate([zpad(KR_LO), _rope_partner(wakr), zpad(LANES - KR_HI)], axis=-1)
    w_in_p = jnp.concatenate([wmq, wmk, wmv, wmo, wgq, wgk, wgv, wgg, wacq, wackv, s1, s2], axis=-1).astype(BF16)
    w_gt = jnp.swapaxes(wmg, 1, 2).astype(BF16)
    gate_b = jnp.zeros((depth, 1, LANES), F32).at[:, 0, 0:16].set(m_gate_b)
    gate_bt = m_gate_b.reshape(depth, 16, 1)
    w2p = jnp.zeros((depth, 2, LANES, G_HEADS * G_DK), F32)
    w2p = w2p.at[:, 0, 16:16 + G_RANK].set(g_w2[:, 0]).at[:, 1, 16 + G_RANK:16 + 2 * G_RANK].set(g_w2[:, 1]).astype(BF16)
    b2p = g_b2.reshape(depth, 2, 1, G_HEADS * G_DK)
    uq = a_w_uq.reshape(depth, A_DQ, A_HEADS, A_DNOPE + A_DROPE)
    hp = lambda n: jnp.zeros((depth, uq.shape[1], A_HEADS, n), F32)
    wuq = jnp.concatenate([uq, hp(A_HPAD - KR_HI)], axis=-1).reshape(depth, A_DQ, -1).astype(BF16)
    wuqs = jnp.concatenate([hp(KR_LO), _rope_partner(uq[..., A_DNOPE:]), hp(A_HPAD - KR_HI)],
                           axis=-1).reshape(depth, A_DQ, -1).astype(BF16)
    ukv = a_w_ukv.reshape(depth, A_DC, A_HEADS, A_DNOPE + A_DV)
    kp = jnp.zeros((depth, A_DC, A_HEADS, A_HPAD - A_DNOPE), F32)
    wk = jnp.concatenate([ukv[..., :A_DNOPE], kp], axis=-1).reshape(depth, A_DC, -1).astype(BF16)
    wv = jnp.concatenate([ukv[..., A_DNOPE:], kp], axis=-1).reshape(depth, A_DC, -1).astype(BF16)
    mw, gw = M_HEADS * M_DV, G_HEADS * G_DV
    wo_m = w_out[:, 0:mw].astype(BF16)
    wo_g = w_out[:, mw:mw + gw].astype(BF16)
    wo_a = w_out[:, mw + gw:].reshape(depth, A_HEADS, A_DV, d)
    wo_a = jnp.concatenate([wo_a, jnp.zeros((depth, A_HEADS, A_HPAD - A_DV, d), F32)], axis=2)
    wo_a = wo_a.reshape(depth, A_HEADS * A_HPAD, d).astype(BF16)
    wfg, wfu, wfd = w_ffn_gate.astype(BF16), w_ffn_up.astype(BF16), w_ffn_down.astype(BF16)

    sconst = _scan_consts()
    head_lanes = jnp.asarray((np.arange(M_HEADS)[:, None] == (np.arange(S_W) // M_DV)[None, :]).astype(np.float32))
    hi = np.arange(mw) // M_DV
    avg = jnp.asarray((hi[:, None] == hi[None, :]).astype(np.float32) / M_DV, BF16)
    cos, sin, kcos = _rope_tables(n_ctx_rows, n_dec, dec_len)
    krp_all = jnp.zeros(cache_mla_krope.shape[:-1] + (LANES,), F32).at[..., KR_LO:KR_HI].set(cache_mla_krope)

    x = (x_prompt.reshape(n_ctx_rows, d), x_sample.reshape(n_dec_rows, d))
    new_c, new_n, new_m, new_s, new_ckv, new_kr = [], [], [], [], [], []
    for l in range(depth):
        mod_map = mod_map_for(l)
        zm, zg, zs, zt, q, k, v, ckv = _in_mla_call(
            x, mods, mod_map, w_in_p[l], w_gt[l], cos, sin, kcos,
            a_q_norm_g[l][None, :], a_kv_norm_g[l][None, :], wuq[l], wuqs[l], wk[l], wv[l])

        n_bd = jnp.einsum('bdhk,hg->bdhkg', state_mlstm_n[:, l], head_lanes)
        cn0 = jnp.concatenate([_block_diag_heads(state_mlstm_C[:, l]),
                               n_bd.reshape(n_dec, 2, M_HEADS * M_DK, S_W)], axis=-1)
        m0 = jnp.repeat(state_mlstm_m[:, l], M_DV, axis=-1)[:, :, None, :]
        s0 = _block_diag_heads(state_gla_S[:, l])
        (hf, hb, c_fin, n_fin, m_fin), (of, ob, s_fin) = _scan_call(
            plan,
            [_mlstm_part(n_ctx, zm, zs, zt, cn0, m0, gate_b[l], gate_bt[l], sconst),
             _gla_part(n_ctx, zg, zs, s0, w2p[l], b2p[l], sconst)],
            "mixer_scan")

        kc, vc = _cache_kv_call(cache_mla_ckv[:, l].reshape(n_dec * past_len, A_DC),
                                krp_all[:, l].reshape(n_dec * past_len, LANES), wk[l], wv[l])
        a_ctx = _attn_ctx_call(q, k, v, n_ctx_rows)
        a_dec = _attn_dec_call(q, k, v, kc, vc, n_ctx_rows, n_dec, dec_len, past_len)

        x = _out_ffn_call(hf, hb, of, ob, a_ctx, a_dec, zm, zg, x, mods, mod_map,
                          m_norm_g[l][None, :], g_norm_g[l][None, :], avg, wo_m[l], wo_g[l], wo_a[l],
                          ln1_g[l][None, :], ln1_b[l][None, :], wfg[l], wfu[l], wfd[l],
                          ln2_g[l][None, :], ln2_b[l][None, :], alpha)

        new_c.append(c_fin[:n_ctx])
        new_n.append(n_fin[:n_ctx, ..., 0])
        new_m.append(m_fin[:n_ctx, :, 0, ::M_DV])
        new_s.append(s_fin[:n_ctx])
        new_ckv.append(ckv[:n_ctx_rows].reshape(n_ctx, ctx_len, A_DC))
        new_kr.append(zs[:n_ctx_rows, KR_LO:KR_HI].reshape(n_ctx, ctx_len, A_DROPE))

    y_prompt = x[:n_ctx_rows].reshape(n_ctx, ctx_len, d)
    y_sample = x[n_ctx_rows:].reshape(n_dec, dec_len, d)
    st = lambda xs: jnp.stack(xs, axis=1)
    return (y_prompt, y_sample, st(new_c), st(new_n), st(new_m), st(new_s), st(new_ckv), st(new_kr))
```

```python
import functools

import numpy as np
import jax
import jax.numpy as jnp
from jax import lax
from jax.experimental import pallas as pl
from jax.experimental.pallas import tpu as pltpu

F32 = jnp.float32
BF16 = jnp.bfloat16

D_MODEL = 1024
GRID_W = 64
M_HEADS, M_DK, M_DV = 4, 32, 64
G_HEADS, G_DK, G_DV = 4, 32, 64
G_RANK = 16
G_TAU = 16.0
A_HEADS, A_DNOPE, A_DROPE, A_DV = 8, 64, 32, 64
A_DQ, A_DC = 256, 128
A_SCALE = (A_DNOPE + A_DROPE) ** -0.5
ROPE_BASE = 10000.0
IN_SPLITS = (M_HEADS * M_DK, M_HEADS * M_DK, M_HEADS * M_DV, M_HEADS * M_DV, 4 * M_HEADS,
             G_HEADS * G_DK, G_HEADS * G_DK, G_HEADS * G_DV, G_HEADS * G_DV, 2 * G_RANK,
             A_DQ, A_DC, A_DROPE)

LANES = 128
MXU_TILE = 256
VMEM_LIMIT = 56 * 1024 * 1024
ROW_BLOCK = 256
WIDE_ROWS = 512
S_CHUNK = 64
S_W = M_HEADS * M_DV
G_LEVELS = 6
G_SMALL_LEVELS = 3
A_HPAD = 128
A_KEY_TILE = 512
FFN_CHUNKS = 2

ZM_W = 768
ZG_W = 768
ZQ_W = 384
ZS_W = 256
Z_W = ZM_W + ZG_W + ZQ_W + ZS_W
KR_LO, KR_HI = A_DNOPE, A_DNOPE + A_DROPE

assert (M_HEADS, M_DK, M_DV) == (G_HEADS, G_DK, G_DV) and S_CHUNK == M_DV and M_HEADS * M_DK == LANES


def _cparams(sem):
    return pltpu.CompilerParams(dimension_semantics=sem, vmem_limit_bytes=VMEM_LIMIT)


def _f32dot(a, b):
    return jnp.dot(a, b, preferred_element_type=F32)


NT_DIMS = (((1,), (1,)), ((), ()))
TN_DIMS = (((0,), (0,)), ((), ()))


def _split(x):
    hi = x.astype(BF16)
    lo = (x - hi.astype(F32)).astype(BF16)
    return hi, lo


def _dot_c_x(c, x):
    hi, lo = _split(x)
    return _f32dot(c, hi) + _f32dot(c, lo)


def _dot_x_c(x, c):
    hi, lo = _split(x)
    return _f32dot(hi, c) + _f32dot(lo, c)


def _log_sigmoid(x):
    return jnp.minimum(x, 0.0) - jnp.log(1.0 + jnp.exp(-jnp.abs(x)))


def _sigmoid(x):
    return 1.0 / (1.0 + jnp.exp(-x))


def _rms(x, g, eps=1e-6):
    return x * lax.rsqrt(jnp.mean(x * x, axis=-1, keepdims=True) + eps) * g


def _layer_norm(y, g, b, eps=1e-5):
    yc = y - jnp.mean(y, axis=-1, keepdims=True)
    return yc * lax.rsqrt(jnp.mean(yc * yc, axis=-1, keepdims=True) + eps) * g + b


def _with_ones_lane(v):
    lane = lax.broadcasted_iota(jnp.int32, v.shape, 1) % A_HPAD
    return jnp.where(lane == A_DV, 1.0, v)


def _x_specs(x, tm, blk0=0):
    if not isinstance(x, tuple):
        return [pl.BlockSpec((tm, x.shape[1]), lambda i: (i + blk0, 0))], [x]
    nb0 = x[0].shape[0] // tm
    return ([pl.BlockSpec((tm, x[0].shape[1]), lambda i: (jnp.minimum(i + blk0, nb0 - 1), 0)),
             pl.BlockSpec((tm, x[1].shape[1]), lambda i: (jnp.maximum(i + blk0 - nb0, 0), 0))], list(x))


def _read_x(refs, nb0, blk0=0):
    if len(refs) == 1:
        return refs[0][...]
    return jnp.where(pl.program_id(0) + blk0 < nb0, refs[0][...], refs[1][...])


def _ada_kernel(c_ref, w_ref, b_ref, o_ref):
    c = c_ref[...]
    s = c * _sigmoid(c)
    o_ref[0] = _f32dot(s.astype(BF16), w_ref[0].astype(BF16)) + b_ref[0]


def _ada_call(cond8, w_ada, b_ada):
    depth, d, n = w_ada.shape
    tn = 1536
    return pl.pallas_call(
        _ada_kernel,
        grid=(depth, n // tn),
        in_specs=[pl.BlockSpec((8, d), lambda l, j: (0, 0)),
                  pl.BlockSpec((1, d, tn), lambda l, j: (l, 0, j)),
                  pl.BlockSpec((1, 1, tn), lambda l, j: (l, 0, j))],
        out_specs=pl.BlockSpec((1, 8, tn), lambda l, j: (l, 0, j)),
        out_shape=jax.ShapeDtypeStruct((depth, 8, n), F32),
        compiler_params=_cparams(("arbitrary", "arbitrary")),
        name="ada_mod",
    )(cond8, w_ada, b_ada.reshape(depth, 1, n))


def _in_mla_kernel(*refs, n_x, nb0):
    x_refs = refs[:n_x]
    (mod_ref, w_ref, wgt_ref, cos_ref, sin_ref, kcos_ref, gq_ref, gkv_ref, wuq_ref, wuqs_ref, wk_ref, wv_ref,
     zm_ref, zg_ref, zs_ref, zt_ref, q_ref, k_ref, v_ref, ckv_ref, kr_ref) = refs[n_x:]
    d = D_MODEL
    mod = mod_ref[0]
    h = (_read_x(x_refs, nb0) * (1.0 + mod[:, d:2 * d]) + mod[:, 0:d]).astype(BF16)
    z = _f32dot(h, w_ref[...])
    zm_ref[...] = z[:, 0:ZM_W]
    zg_ref[...] = z[:, ZM_W:ZM_W + ZG_W]
    zs = z[:, ZM_W + ZG_W + ZQ_W:Z_W]
    zs_ref[...] = zs[:, 0:LANES]
    kr_ref[...] = zs[:, KR_LO:KR_HI]
    zt_ref[...] = lax.dot_general(wgt_ref[...], h, NT_DIMS, preferred_element_type=F32)
    zq = z[:, ZM_W + ZG_W:ZM_W + ZG_W + ZQ_W]
    cq = _rms(zq[:, 0:A_DQ], gq_ref[...]).astype(BF16)
    qn = _f32dot(cq, wuq_ref[...])
    qs = _f32dot(cq, wuqs_ref[...])
    cos = cos_ref[...]
    sin = sin_ref[...]
    for hd in range(A_HEADS):
        sl = slice(hd * A_HPAD, (hd + 1) * A_HPAD)
        q_ref[:, sl] = ((qn[:, sl] * cos + qs[:, sl] * sin) * A_SCALE).astype(BF16)
    ckv = _rms(zq[:, A_DQ:A_DQ + A_DC], gkv_ref[...])
    ckv_ref[...] = ckv
    ckvb = ckv.astype(BF16)
    kn = _f32dot(ckvb, wk_ref[...])
    v_ref[...] = _with_ones_lane(_f32dot(ckvb, wv_ref[...])).astype(BF16)
    kr = zs[:, 0:LANES] * kcos_ref[...] + zs[:, LANES:2 * LANES] * sin
    for hd in range(A_HEADS):
        sl = slice(hd * A_HPAD, (hd + 1) * A_HPAD)
        k_ref[:, sl] = (kn[:, sl] + kr).astype(BF16)


def _layer_spec(arr, layer, **kw):
    zeros = (0,) * (arr.ndim - 1)
    return pl.BlockSpec((None,) + arr.shape[1:], lambda *_: (layer,) + zeros, **kw)


def _in_mla_call(layer, x, mods, mod_map, rope_map, w_in_p, w_gt, cos, sin, kcos, gq, gkv, wuq, wuqs, wk, wv):
    tm = WIDE_ROWS
    x_specs, x_ops = _x_specs(x, tm)
    nt = sum(a.shape[0] for a in x_ops)
    d = D_MODEL
    hw = A_HEADS * A_HPAD
    row = lambda i: (i, 0)
    rope = lambda i: (rope_map(i), 0)
    resident = dict(pipeline_mode=pl.Buffered(1))
    return pl.pallas_call(
        functools.partial(_in_mla_kernel, n_x=len(x_ops), nb0=x_ops[0].shape[0] // tm),
        grid=(nt // tm,),
        in_specs=x_specs
                 + [pl.BlockSpec((1, 1, 6 * d), lambda i: (mod_map(i * (tm // ROW_BLOCK)), 0, 0)),
                    _layer_spec(w_in_p, layer, **resident), _layer_spec(w_gt, layer),
                    pl.BlockSpec((tm, LANES), rope), pl.BlockSpec((tm, LANES), rope), pl.BlockSpec((tm, LANES), rope),
                    _layer_spec(gq, layer), _layer_spec(gkv, layer),
                    _layer_spec(wuq, layer, **resident), _layer_spec(wuqs, layer, **resident),
                    _layer_spec(wk, layer, **resident), _layer_spec(wv, layer, **resident)],
        out_specs=[pl.BlockSpec((tm, ZM_W), row), pl.BlockSpec((tm, ZG_W), row), pl.BlockSpec((tm, LANES), row),
                   pl.BlockSpec((16, tm), lambda i: (0, i)),
                   pl.BlockSpec((tm, hw), row), pl.BlockSpec((tm, hw), row), pl.BlockSpec((tm, hw), row),
                   pl.BlockSpec((tm, A_DC), row), pl.BlockSpec((tm, A_DROPE), row)],
        out_shape=[jax.ShapeDtypeStruct((nt, ZM_W), F32), jax.ShapeDtypeStruct((nt, ZG_W), F32),
                   jax.ShapeDtypeStruct((nt, LANES), F32), jax.ShapeDtypeStruct((16, nt), F32),
                   jax.ShapeDtypeStruct((nt, hw), BF16), jax.ShapeDtypeStruct((nt, hw), BF16),
                   jax.ShapeDtypeStruct((nt, hw), BF16), jax.ShapeDtypeStruct((nt, A_DC), F32),
                   jax.ShapeDtypeStruct((nt, A_DROPE), F32)],
        compiler_params=_cparams(("parallel",)),
        name="in_proj_mla",
    )(*x_ops, mods, w_in_p, w_gt, cos, sin, kcos, gq, gkv, wuq, wuqs, wk, wv)


def _scan_consts():
    n = S_CHUNK
    t = np.arange(n)[:, None]
    s = np.arange(n)[None, :]
    tri = [(s <= t), (s >= t)]
    head_of = np.arange(S_W) // M_DV
    khead = np.arange(M_HEADS * M_DK) // M_DK
    kmask = (head_of[:, None] == khead[None, :]).astype(np.float32)
    vmask = (head_of[:, None] == head_of[None, :]).astype(np.float32)
    bd = (khead[:, None] == head_of[None, :]).astype(np.float32)
    sel, tritile, neg = [], [], []
    for d in (0, 1):
        io, fo = 8 * d, 8 * d + 4
        m = np.zeros((LANES, 3 * S_W), np.float32)
        for h in range(M_HEADS):
            m[fo + h, h * M_DV:(h + 1) * M_DV] = 1.0
            m[io + h, S_W + h * M_DV:S_W + (h + 1) * M_DV] = 1.0
            m[fo + h, 2 * S_W + h * M_DK:2 * S_W + (h + 1) * M_DK] = 1.0
            m[io + h, 2 * S_W + LANES + h * M_DK:2 * S_W + LANES + (h + 1) * M_DK] = 1.0
        sel.append(m)
        tritile.append(np.tile(tri[d].T.astype(np.float32), (1, M_HEADS)))
        neg.append(np.tile(np.where(tri[d], 0.0, -np.inf).astype(np.float32), (1, M_HEADS)))
    masks = [np.eye(n)]
    for lev in range(G_LEVELS):
        b = 1 << lev
        right = (t % (2 * b)) >= b
        same_parent = (t // (2 * b)) == (s // (2 * b))
        masks.append((same_parent & right & ((s % (2 * b)) < b)).astype(np.float32))
    flip = lambda m: m[::-1, ::-1]
    mcat = np.stack([np.stack([np.tile(m, (1, G_HEADS)) for m in masks]),
                     np.stack([np.tile(flip(m), (1, G_HEADS)) for m in masks])])
    gath = []
    for d in (0, 1):
        g = np.zeros((G_SMALL_LEVELS * n, n), np.float32)
        for lev in range(G_SMALL_LEVELS):
            for row in range(n):
                g[lev * n + row, _gla_boundary(row, 1 << lev, d)] = 1.0
        gath.append(g)
    f32 = lambda a: jnp.asarray(a, F32)
    b16 = lambda a: jnp.asarray(a, BF16)
    return dict(tri=b16(np.stack(tri).astype(np.float32)), sel=b16(np.stack(sel)), tritile=b16(np.stack(tritile)),
                neg=f32(np.stack(neg)), kmask=b16(kmask), vmask=b16(vmask), bd=f32(bd),
                bd2=f32(np.concatenate([bd, bd], axis=1)), mcat=f32(mcat), gath=b16(np.stack(gath)))


def _gla_boundary(row, b, d):
    pstart = (row // (2 * b)) * (2 * b)
    return pstart + b - 1 if d == 0 else pstart + b


def _scan_units():
    nch = ROW_BLOCK // S_CHUNK
    return [(d, ci if d == 0 else nch - 1 - ci) for ci in range(nch) for d in (0, 1)]


def _chunk_rows(c):
    return slice(c * S_CHUNK, (c + 1) * S_CHUNK)


def _mlstm_body(first_ref, seq_ref, bwd_ref,
                zmf_ref, zsf_ref, ztf_ref, zmb_ref, zsb_ref, ztb_ref,
                cn0_ref, m0_ref, gb_ref, gbt_ref,
                tri_ref, sel_ref, tritile_ref, neg_ref, kmask_ref, vmask_ref, bd2_ref,
                hf_ref, hb_ref, cout_ref, nout_ref, mout_ref,
                cns_ref, ms_ref):
    r = pl.program_id(0)
    n = S_CHUNK
    qk_w = M_HEADS * M_DK
    kmask = kmask_ref[...]
    vmask = vmask_ref[...]
    bd2 = bd2_ref[...]
    units = _scan_units()
    rows = _chunk_rows
    zm_refs, zs_refs, zt_refs, h_refs = (zmf_ref, zmb_ref), (zsf_ref, zsb_ref), (ztf_ref, ztb_ref), (hf_ref, hb_ref)
    fo = lambda d: 8 * d + 4
    last = lambda d: n - 1 if d == 0 else 0
    lane = lax.broadcasted_iota(jnp.int32, (n, LANES), 1)
    head_row = lax.broadcasted_iota(jnp.int32, (1, S_W), 1) // M_DV

    def per_unit(f):
        return [f(d, c) for d, c in units]

    def dot_split(c_left, x_split, c_right):
        hi, lo = x_split
        if c_left is not None:
            return _f32dot(c_left, hi) + _f32dot(c_left, lo)
        return _f32dot(hi, c_right) + _f32dot(lo, c_right)

    q = per_unit(lambda d, c: zm_refs[d][rows(c), 0:qk_w].astype(BF16))
    k = per_unit(lambda d, c: zm_refs[d][rows(c), qk_w:2 * qk_w] * (M_DK ** -0.5))
    v = per_unit(lambda d, c: zm_refs[d][rows(c), 2 * qk_w:2 * qk_w + S_W])
    graw = per_unit(lambda d, c: zs_refs[d][rows(c), :] + gb_ref[...])
    graw_t = per_unit(lambda d, c: zt_refs[d][:, rows(c)] + gbt_ref[...])
    lf = [_split(_log_sigmoid(g)) for g in graw]
    lf_t = [_split(_log_sigmoid(g)) for g in graw_t]
    bcol = [dot_split(tri_ref[d], s, None) for (d, _), s in zip(units, lf)]
    xs = [_split(jnp.where((lane >= fo(d)) & (lane < fo(d) + M_HEADS), b, g))
          for (d, _), b, g in zip(units, bcol, graw)]
    ex = [dot_split(None, s, sel_ref[d]) for (d, _), s in zip(units, xs)]
    b256 = [e[:, 0:S_W] for e in ex]
    i256 = [e[:, S_W:2 * S_W] for e in ex]
    b128 = [e[:, 2 * S_W:2 * S_W + LANES] for e in ex]
    i128 = [e[:, 2 * S_W + LANES:] for e in ex]
    rows_t = [dot_split(None, s, tritile_ref[d]) for (d, _), s in zip(units, lf_t)]
    b_row = []
    for (d, _), rt in zip(units, rows_t):
        br = jnp.zeros((1, S_W), F32)
        for h in range(M_HEADS):
            br = jnp.where(head_row == h, rt[fo(d) + h:fo(d) + h + 1, :], br)
        b_row.append(br)
    decay = [jnp.exp(b - br + neg_ref[d]) for (d, _), b, br in zip(units, b256, b_row)]
    k_bd = [jnp.concatenate([ki.astype(BF16)] * M_HEADS, axis=0) * kmask for ki in k]
    qk = [lax.dot_general(qi, ki, NT_DIMS, preferred_element_type=F32) for qi, ki in zip(q, k_bd)]
    p = [(a * b).astype(BF16) for a, b in zip(qk, decay)]
    i_max = [jnp.max(i, axis=0, keepdims=True) for i in i256]
    scale = [jnp.exp(i - im) for i, im in zip(i256, i_max)]
    ve = [jnp.concatenate([jnp.concatenate([(vi * sc).astype(BF16)] * M_HEADS, axis=0) * vmask,
                           jnp.concatenate([sc.astype(BF16)] * M_HEADS, axis=0) * vmask], axis=1)
          for vi, sc in zip(v, scale)]
    nd = [_f32dot(pi, vei) for pi, vei in zip(p, ve)]
    bl256 = [b[last(d):last(d) + 1, :] for (d, _), b in zip(units, b256)]
    g_max = [jnp.max(bl - b + i, axis=0, keepdims=True) for bl, b, i in zip(bl256, b256, i256)]
    g128 = [b[last(d):last(d) + 1, :] - b + i for (d, _), b, i in zip(units, b128, i128)]
    w = [jnp.exp(g - jnp.max(g, axis=0, keepdims=True)) for g in g128]
    kw = [(ki * wi).astype(BF16) for ki, wi in zip(k, w)]
    ones = jnp.ones((n, S_W), BF16)
    upd = [lax.dot_general(kwi, jnp.concatenate([vi.astype(BF16), ones], axis=1), TN_DIMS,
                           preferred_element_type=F32) for kwi, vi in zip(kw, v)]
    fresh = first_ref[r] != 0
    zero = first_ref[r] == 2
    state = [(jnp.where(fresh, jnp.where(zero, 0.0, cn0_ref[0, d]), cns_ref[d]),
              jnp.where(fresh, jnp.where(zero, 0.0, m0_ref[0, d]), ms_ref[d])) for d in (0, 1)]
    outs = []
    for i, (d, c) in enumerate(units):
        cn, m = state[d]
        qcn = _f32dot(q[i], cn.astype(BF16))
        a = b256[i] + m
        sig = jnp.maximum(i_max[i], a)
        f1 = jnp.exp(i_max[i] - sig)
        f2 = jnp.exp(a - sig)
        num = f1 * nd[i][:, 0:S_W] + f2 * qcn[:, 0:S_W]
        den = f1 * nd[i][:, S_W:] + f2 * qcn[:, S_W:]
        outs.append(num / jnp.maximum(jnp.abs(den), jnp.exp(-sig)))
        m_new = jnp.maximum(bl256[i] + m, g_max[i])
        alpha = jnp.exp(bl256[i] + m - m_new)
        beta = jnp.exp(g_max[i] - m_new)
        state[d] = (jnp.concatenate([alpha, alpha], axis=1) * cn
                    + (jnp.concatenate([beta, beta], axis=1) * bd2) * upd[i], m_new)
    for (d, c), o in zip(units, outs):
        h_refs[d][rows(c), :] = o
    for d in (0, 1):
        cn, m = state[d]
        cns_ref[d], ms_ref[d] = cn, m
        mout_ref[0, d] = m
        for h in range(M_HEADS):
            blk = cn[h * M_DK:(h + 1) * M_DK, :]
            cout_ref[0, d, h] = blk[:, h * M_DV:(h + 1) * M_DV]
            nout_ref[0, d, h] = blk[:, S_W + h * M_DV:S_W + (h + 1) * M_DV]


def _mlstm_part(layer, n_ctx, zm, zs, zt, cn0, m0, gb, gbt, sc):
    nt = zm.shape[0]
    kd = M_HEADS * M_DK
    n = S_CHUNK
    fwd = lambda r, first, seq, bwd: (r, 0)
    bwd_ = lambda r, first, seq, bwd: (bwd[r], 0)
    fwd_t = lambda r, first, seq, bwd: (0, r)
    bwd_t = lambda r, first, seq, bwd: (0, bwd[r])
    seq5 = lambda r, first, seq, bwd: (jnp.maximum(seq[r] - n_ctx, 0), layer, 0, 0, 0)
    out4 = lambda r, first, seq, bwd: (jnp.minimum(seq[r], n_ctx), 0, 0, 0)
    out5 = lambda r, first, seq, bwd: (jnp.minimum(seq[r], n_ctx), 0, 0, 0, 0)
    const = lambda r, first, seq, bwd: (0, 0)
    const3 = lambda r, first, seq, bwd: (0, 0, 0)
    return dict(
        body=_mlstm_body,
        in_specs=[pl.BlockSpec((ROW_BLOCK, 512), fwd), pl.BlockSpec((ROW_BLOCK, LANES), fwd),
                  pl.BlockSpec((16, ROW_BLOCK), fwd_t),
                  pl.BlockSpec((ROW_BLOCK, 512), bwd_), pl.BlockSpec((ROW_BLOCK, LANES), bwd_),
                  pl.BlockSpec((16, ROW_BLOCK), bwd_t),
                  pl.BlockSpec((1, None, 2, kd, 2 * S_W), seq5), pl.BlockSpec((1, None, 2, 1, S_W), seq5),
                  _layer_spec(gb, layer), _layer_spec(gbt, layer),
                  pl.BlockSpec((2, n, n), const3), pl.BlockSpec((2, LANES, 3 * S_W), const3),
                  pl.BlockSpec((2, n, S_W), const3), pl.BlockSpec((2, n, S_W), const3),
                  pl.BlockSpec((S_W, kd), const), pl.BlockSpec((S_W, S_W), const),
                  pl.BlockSpec((kd, 2 * S_W), const)],
        out_specs=[pl.BlockSpec((ROW_BLOCK, S_W), fwd), pl.BlockSpec((ROW_BLOCK, S_W), bwd_),
                   pl.BlockSpec((1, 2, M_HEADS, M_DK, M_DV), out5), pl.BlockSpec((1, 2, M_HEADS, M_DK, M_DV), out5),
                   pl.BlockSpec((1, 2, 1, S_W), out4)],
        scratch_shapes=[pltpu.VMEM((2, kd, 2 * S_W), F32), pltpu.VMEM((2, 1, S_W), F32)],
        out_shape=[jax.ShapeDtypeStruct((nt, S_W), F32), jax.ShapeDtypeStruct((nt, S_W), F32),
                   jax.ShapeDtypeStruct((n_ctx + 1, 2, M_HEADS, M_DK, M_DV), F32),
                   jax.ShapeDtypeStruct((n_ctx + 1, 2, M_HEADS, M_DK, M_DV), F32),
                   jax.ShapeDtypeStruct((n_ctx + 1, 2, 1, S_W), F32)],
        operands=[zm, zs, zt, zm, zs, zt, cn0, m0, gb, gbt,
                  sc["tri"], sc["sel"], sc["tritile"], sc["neg"], sc["kmask"], sc["vmask"], sc["bd2"]])


def _gla_body(first_ref, seq_ref, bwd_ref,
              zgf_ref, zsf_ref, zgb_ref, zsb_ref, s0_ref, w2_ref, b2_ref,
              tri_ref, gath_ref, mcat_ref, kmask_ref, vmask_ref, bd_ref,
              of_ref, ob_ref, sout_ref, ss_ref):
    r = pl.program_id(0)
    n = S_CHUNK
    kd = G_HEADS * G_DK
    vd = G_HEADS * G_DV
    ones_v = jnp.ones((n, vd), BF16)
    kmask = kmask_ref[...]
    vmask = vmask_ref[...]
    bd = bd_ref[...]
    units = _scan_units()
    rows = _chunk_rows
    zg_refs, zs_refs, o_refs = (zgf_ref, zgb_ref), (zsf_ref, zsb_ref), (of_ref, ob_ref)

    def per_unit(f):
        return [f(d, c) for d, c in units]

    def bd_k(kt):
        return jnp.concatenate([kt] * G_HEADS, axis=0) * kmask

    q = per_unit(lambda d, c: zg_refs[d][rows(c), 0:kd] * (G_DK ** -0.5))
    k = per_unit(lambda d, c: zg_refs[d][rows(c), kd:2 * kd])
    v = per_unit(lambda d, c: zg_refs[d][rows(c), 2 * kd:2 * kd + vd].astype(BF16))
    x = per_unit(lambda d, c: _f32dot(zs_refs[d][rows(c), :].astype(BF16), w2_ref[d]) + b2_ref[d])
    lg = [_log_sigmoid(xi) * (1.0 / G_TAU) for xi in x]
    lg_split = [_split(l) for l in lg]
    bc = [_f32dot(tri_ref[d], hi) + _f32dot(tri_ref[d], lo)
          for (d, _), (hi, lo) in zip(units, lg_split)]
    bc_split = [_split(b) for b in bc]
    r_small = [_f32dot(gath_ref[d], hi) + _f32dot(gath_ref[d], lo)
               for (d, _), (hi, lo) in zip(units, bc_split)]
    qb = [qi.astype(BF16) for qi in q]
    kbd = [bd_k(ki.astype(BF16)) for ki in k]
    att = [mcat_ref[d, 0] * lax.dot_general(qi, ki, NT_DIMS, preferred_element_type=F32)
           for (d, _), qi, ki in zip(units, qb, kbd)]
    for lev in range(G_LEVELS):
        b = 1 << lev
        if lev < G_SMALL_LEVELS:
            r_lev = [rs[lev * n:(lev + 1) * n] for rs in r_small]
        else:
            r_lev = [jnp.concatenate(
                [jnp.broadcast_to(bci[_gla_boundary(p0, b, d):_gla_boundary(p0, b, d) + 1, :], (2 * b, kd))
                 for p0 in range(0, n, 2 * b)], axis=0) for (d, _), bci in zip(units, bc)]
        e_lev = [jnp.exp(-jnp.abs(bci - ri)) for bci, ri in zip(bc, r_lev)]
        qt = [(qi * ei).astype(BF16) for qi, ei in zip(q, e_lev)]
        kt = [bd_k((ki * ei).astype(BF16)) for ki, ei in zip(k, e_lev)]
        p = [lax.dot_general(qi, ki, NT_DIMS, preferred_element_type=F32) for qi, ki in zip(qt, kt)]
        att = [ai + mcat_ref[d, lev + 1] * pi for (d, _), ai, pi in zip(units, att, p)]
    v_bd = [jnp.concatenate([vi] * G_HEADS, axis=0) * vmask for vi in v]
    intra = [_f32dot(ai.astype(BF16), vi) for ai, vi in zip(att, v_bd)]
    q_inc = [(qi * jnp.exp(bci)).astype(BF16) for qi, bci in zip(q, bc)]
    k_suf = [(ki * jnp.exp(bci[(n - 1 if d == 0 else 0):(n if d == 0 else 1), :] - bci)).astype(BF16)
             for (d, _), ki, bci in zip(units, k, bc)]
    decay = [jnp.exp(lax.dot_general(hi, ones_v, TN_DIMS, preferred_element_type=F32)
                     + lax.dot_general(lo, ones_v, TN_DIMS, preferred_element_type=F32))
             for hi, lo in lg_split]
    upd = [bd * lax.dot_general(ki, vi, TN_DIMS, preferred_element_type=F32) for ki, vi in zip(k_suf, v)]
    fresh = first_ref[r] != 0
    zero = first_ref[r] == 2
    states = [jnp.where(fresh, jnp.where(zero, 0.0, s0_ref[0, d]), ss_ref[d]) for d in (0, 1)]
    outs = []
    for i, (d, c) in enumerate(units):
        inter = _f32dot(q_inc[i], states[d].astype(BF16))
        outs.append(inter + intra[i])
        states[d] = decay[i] * states[d] + upd[i]
    for (d, c), o in zip(units, outs):
        o_refs[d][rows(c), :] = o
    for d in (0, 1):
        ss_ref[d] = states[d]
        for h in range(G_HEADS):
            sout_ref[0, d, h] = states[d][h * G_DK:(h + 1) * G_DK, h * G_DV:(h + 1) * G_DV]


def _gla_part(layer, n_ctx, zg, zs, s0, w2p, b2p, sc):
    nt = zg.shape[0]
    kd = G_HEADS * G_DK
    vd = G_HEADS * G_DV
    n = S_CHUNK
    fwd = lambda r, first, seq, bwd: (r, 0)
    bwd_ = lambda r, first, seq, bwd: (bwd[r], 0)
    seq5 = lambda r, first, seq, bwd: (jnp.maximum(seq[r] - n_ctx, 0), layer, 0, 0, 0)
    out5 = lambda r, first, seq, bwd: (jnp.minimum(seq[r], n_ctx), 0, 0, 0, 0)
    const = lambda r, first, seq, bwd: (0, 0)
    const3 = lambda r, first, seq, bwd: (0, 0, 0)
    const4 = lambda r, first, seq, bwd: (0, 0, 0, 0)
    return dict(
        body=_gla_body,
        in_specs=[pl.BlockSpec((ROW_BLOCK, 512), fwd), pl.BlockSpec((ROW_BLOCK, LANES), fwd),
                  pl.BlockSpec((ROW_BLOCK, 512), bwd_), pl.BlockSpec((ROW_BLOCK, LANES), bwd_),
                  pl.BlockSpec((1, None, 2, kd, vd), seq5),
                  _layer_spec(w2p, layer), _layer_spec(b2p, layer),
                  pl.BlockSpec((2, n, n), const3),
                  pl.BlockSpec((2, G_SMALL_LEVELS * n, n), const3),
                  pl.BlockSpec((2, G_LEVELS + 1, n, G_HEADS * n), const4),
                  pl.BlockSpec((G_HEADS * n, kd), const), pl.BlockSpec((G_HEADS * n, vd), const),
                  pl.BlockSpec((kd, vd), const)],
        out_specs=[pl.BlockSpec((ROW_BLOCK, vd), fwd), pl.BlockSpec((ROW_BLOCK, vd), bwd_),
                   pl.BlockSpec((1, 2, G_HEADS, G_DK, G_DV), out5)],
        scratch_shapes=[pltpu.VMEM((2, kd, vd), F32)],
        out_shape=[jax.ShapeDtypeStruct((nt, vd), F32), jax.ShapeDtypeStruct((nt, vd), F32),
                   jax.ShapeDtypeStruct((n_ctx + 1, 2, G_HEADS, G_DK, G_DV), F32)],
        operands=[zg, zs, zg, zs, s0, w2p, b2p, sc["tri"], sc["gath"], sc["mcat"],
                  sc["kmask"], sc["vmask"], sc["bd"]])


def _scan_call(plan, parts, name):
    n_in = [len(p["operands"]) for p in parts]
    n_out = [len(p["out_specs"]) for p in parts]
    n_scr = [len(p["scratch_shapes"]) for p in parts]

    def kern(first_ref, seq_ref, bwd_ref, *refs):
        ins, outs, scrs = refs[:sum(n_in)], refs[sum(n_in):sum(n_in) + sum(n_out)], refs[sum(n_in) + sum(n_out):]
        oi = oo = os_ = 0
        for p, a, b, c in zip(parts, n_in, n_out, n_scr):
            p["body"](first_ref, seq_ref, bwd_ref, *ins[oi:oi + a], *outs[oo:oo + b], *scrs[os_:os_ + c])
            oi, oo, os_ = oi + a, oo + b, os_ + c

    nb = plan["first"].shape[0]
    grid_spec = pltpu.PrefetchScalarGridSpec(
        num_scalar_prefetch=3,
        grid=(nb,),
        in_specs=[s for p in parts for s in p["in_specs"]],
        out_specs=[s for p in parts for s in p["out_specs"]],
        scratch_shapes=[s for p in parts for s in p["scratch_shapes"]],
    )
    outs = pl.pallas_call(
        kern,
        grid_spec=grid_spec,
        out_shape=[s for p in parts for s in p["out_shape"]],
        compiler_params=_cparams(("arbitrary",)),
        name=name,
    )(plan["first"], plan["seq"], plan["bwd"], *[o for p in parts for o in p["operands"]])
    res, oo = [], 0
    for b in n_out:
        res.append(outs[oo:oo + b])
        oo += b
    return res


def _cache_kv_kernel(ckv_ref, kr_ref, wk_ref, wv_ref, k_ref, v_ref):
    ckvb = ckv_ref[...].astype(BF16)
    kn = _f32dot(ckvb, wk_ref[...])
    v_ref[...] = _with_ones_lane(_f32dot(ckvb, wv_ref[...])).astype(BF16)
    kr = kr_ref[...]
    for h in range(A_HEADS):
        sl = slice(h * A_HPAD, (h + 1) * A_HPAD)
        k_ref[:, sl] = (kn[:, sl] + kr).astype(BF16)


def _cache_kv_call(layer, ckv, krp, wk, wv):
    n_dec, _, past_len, _ = ckv.shape
    hw = A_HEADS * A_HPAD
    row = lambda b: (b, 0)
    cache = lambda b: (b, layer, 0, 0)
    return pl.pallas_call(
        _cache_kv_kernel,
        grid=(n_dec,),
        in_specs=[pl.BlockSpec((None, None, past_len, A_DC), cache),
                  pl.BlockSpec((None, None, past_len, LANES), cache),
                  _layer_spec(wk, layer), _layer_spec(wv, layer)],
        out_specs=[pl.BlockSpec((past_len, hw), row), pl.BlockSpec((past_len, hw), row)],
        out_shape=[jax.ShapeDtypeStruct((n_dec * past_len, hw), BF16),
                   jax.ShapeDtypeStruct((n_dec * past_len, hw), BF16)],
        compiler_params=_cparams(("parallel",)),
        name="mla_cache_kv",
    )(ckv, krp, wk, wv)


def _attn_kernel(*refs, n_seg, heads_per_group):
    q_ref = refs[0]
    kv_refs = refs[1:1 + 2 * n_seg]
    o_ref = refs[1 + 2 * n_seg]
    tiles = []
    for j in range(n_seg):
        nk = kv_refs[2 * j].shape[0]
        tk = min(nk, A_KEY_TILE)
        tiles += [(j, t0, tk) for t0 in range(0, nk, tk)]
    head = lambda h: slice(h * A_HPAD, (h + 1) * A_HPAD)

    def scores(h, tile):
        j, t0, tk = tile
        return lax.dot_general(q_ref[:, head(h)], kv_refs[2 * j][t0:t0 + tk, head(h)], NT_DIMS,
                               preferred_element_type=F32)

    def lane_max(acc, s):
        for c0 in range(0, s.shape[1], LANES):
            part = s[:, c0:c0 + LANES]
            acc = part if acc is None else jnp.maximum(acc, part)
        return acc

    groups = [list(range(g0, min(g0 + heads_per_group, A_HEADS))) for g0 in range(0, A_HEADS, heads_per_group)]
    s_cur = {h: [scores(h, t) for t in tiles] for h in groups[0]}
    for gi, grp in enumerate(groups):
        nxt = groups[gi + 1] if gi + 1 < len(groups) else []
        m = {}
        for h in grp:
            mx = None
            for s in s_cur[h]:
                mx = lane_max(mx, s)
            m[h] = jnp.max(mx, axis=1, keepdims=True)
        s_next = {h: [] for h in nxt}
        acc = {h: None for h in grp}
        for ti, tile in enumerate(tiles):
            j, t0, tk = tile
            for h in nxt:
                s_next[h].append(scores(h, tile))
            p = {h: jnp.exp((s_cur[h][ti] - m[h]).astype(BF16)) for h in grp}
            for h in grp:
                pv = _f32dot(p[h], kv_refs[2 * j + 1][t0:t0 + tk, head(h)])
                acc[h] = pv if acc[h] is None else acc[h] + pv
        for h in grp:
            o_ref[:, head(h)] = (acc[h] / acc[h][:, A_DV:A_DV + 1]).astype(BF16)
        s_cur = s_next


def _attn_ctx_call(q, k, v, n_rows):
    hw = A_HEADS * A_HPAD
    tm = ROW_BLOCK
    row = lambda i: (i, 0)
    return pl.pallas_call(
        functools.partial(_attn_kernel, n_seg=1, heads_per_group=A_HEADS),
        grid=(n_rows // tm,),
        in_specs=[pl.BlockSpec((tm, hw), row)] * 3,
        out_specs=pl.BlockSpec((tm, hw), row),
        out_shape=jax.ShapeDtypeStruct((n_rows, hw), BF16),
        compiler_params=_cparams(("parallel",)),
        name="mla_attn_ctx",
    )(q, k, v)


def _attn_dec_call(q, k, v, kc, vc, n_ctx_rows, n_dec, dec_len, past_len):
    hw = A_HEADS * A_HPAD
    tm = ROW_BLOCK
    nq = dec_len // tm
    assert n_ctx_rows % dec_len == 0
    qmap = lambda b, i: (n_ctx_rows // tm + b * nq + i, 0)
    own = lambda b, i: (n_ctx_rows // dec_len + b, 0)
    return pl.pallas_call(
        functools.partial(_attn_kernel, n_seg=2, heads_per_group=2),
        grid=(n_dec, nq),
        in_specs=[pl.BlockSpec((tm, hw), qmap),
                  pl.BlockSpec((past_len, hw), lambda b, i: (b, 0)),
                  pl.BlockSpec((past_len, hw), lambda b, i: (b, 0)),
                  pl.BlockSpec((dec_len, hw), own),
                  pl.BlockSpec((dec_len, hw), own)],
        out_specs=pl.BlockSpec((tm, hw), lambda b, i: (b * nq + i, 0)),
        out_shape=jax.ShapeDtypeStruct((n_dec * dec_len, hw), BF16),
        compiler_params=_cparams(("parallel", "arbitrary")),
        name="mla_attn_dec",
    )(q, kc, vc, k, v)


def _out_ffn_kernel(*refs, n_x, nb0, nb_ctx, alpha, blk0):
    (hf_ref, hb_ref, of_ref, ob_ref, actx_ref, adec_ref, mo_ref, gg_ref) = refs[:8]
    x_refs = refs[8:8 + n_x]
    (mod_ref, mng_ref, gng_ref, avg_ref, wm_ref, wg_ref, wa_ref, l1g_ref, l1b_ref,
     wfg_ref, wfu_ref, wfd_ref, l2g_ref, l2b_ref, o_ref) = refs[8 + n_x:]
    d = D_MODEL
    avg = avg_ref[...]
    mod = mod_ref[0]
    is_ctx = pl.program_id(0) + blk0 < nb_ctx
    x_in = _read_x(x_refs, nb0, blk0)
    tm = o_ref.shape[0]
    parts = [slice(p0, p0 + ROW_BLOCK) for p0 in range(0, tm, ROW_BLOCK)]
    each = lambda f: [f(s) for s in parts]
    a = each(lambda s: jnp.where(is_ctx, actx_ref[s, :], adec_ref[s, :]))
    hm = each(lambda s: hf_ref[s, :] + hb_ref[s, :])
    og = each(lambda s: of_ref[s, :] + ob_ref[s, :])
    hc = [x - _dot_x_c(x, avg) for x in hm]
    hvar = [_dot_x_c(x * x, avg) for x in hc]
    gms = [_dot_x_c(x * x, avg) for x in og]
    m_out = [x * lax.rsqrt(v + 1e-6) * mng_ref[...] * _sigmoid(mo_ref[s, :]) for x, v, s in zip(hc, hvar, parts)]
    gg = each(lambda s: gg_ref[s, :])
    g_out = [x * lax.rsqrt(v + 1e-6) * gng_ref[...] * (g * _sigmoid(g)) for x, v, g in zip(og, gms, gg)]
    mix = [_f32dot(m.astype(BF16), wm_ref[...]) + _f32dot(g.astype(BF16), wg_ref[...]) + _f32dot(ai, wa_ref[...])
           for m, g, ai in zip(m_out, g_out, a)]
    y = [alpha * x_in[s, :] + mod[:, 2 * d:3 * d] * mi for s, mi in zip(parts, mix)]
    x1 = jnp.concatenate([_layer_norm(yi, l1g_ref[...], l1b_ref[...]) for yi in y], axis=0)
    hh = (x1 * (1.0 + mod[:, 4 * d:5 * d]) + mod[:, 3 * d:4 * d]).astype(BF16)
    f = wfg_ref.shape[1]
    tiles = -(-f // MXU_TILE)
    edges = [min(f, MXU_TILE * (tiles * c // FFN_CHUNKS)) for c in range(FFN_CHUNKS)] + [f]
    acc = jnp.zeros(x1.shape, F32)
    for c in range(FFN_CHUNKS):
        sl = slice(edges[c], edges[c + 1])
        g = _f32dot(hh, wfg_ref[:, sl])
        u = _f32dot(hh, wfu_ref[:, sl])
        act = (g * _sigmoid(g) * u).astype(BF16)
        acc = acc + _f32dot(act, wfd_ref[sl, :])
    y2 = alpha * x1 + mod[:, 5 * d:6 * d] * acc
    o_ref[...] = _layer_norm(y2, l2g_ref[...], l2b_ref[...])


def _out_ffn_call(layer, hf, hb, of, ob, a_ctx, a_dec, zm, zg, x, mods, mod_map, mng, gng, avg, wm, wg, wa, l1g, l1b,
                  wfg, wfu, wfd, l2g, l2b, alpha, blk0=0, n_blk=None):
    tm = WIDE_ROWS
    x_specs, x_ops = _x_specs(x, tm, blk0)
    nt = sum(a.shape[0] for a in x_ops)
    n_blk = nt // tm if n_blk is None else n_blk
    d = D_MODEL
    hw = M_HEADS * M_DV
    ahw = A_HEADS * A_HPAD
    nb_ctx = a_ctx.shape[0] // tm
    row = lambda i: (i + blk0, 0)
    gate_col = lambda i: (i + blk0, 2)
    resident = dict(pipeline_mode=pl.Buffered(1))
    return pl.pallas_call(
        functools.partial(_out_ffn_kernel, n_x=len(x_ops), nb0=x_ops[0].shape[0] // tm, nb_ctx=nb_ctx,
                          alpha=alpha, blk0=blk0),
        grid=(n_blk,),
        in_specs=[pl.BlockSpec((tm, hw), row)] * 4
                 + [pl.BlockSpec((tm, ahw), lambda i: (jnp.minimum(i + blk0, nb_ctx - 1), 0)),
                    pl.BlockSpec((tm, ahw), lambda i: (jnp.maximum(i + blk0 - nb_ctx, 0), 0)),
                    pl.BlockSpec((tm, hw), gate_col), pl.BlockSpec((tm, hw), gate_col)]
                 + x_specs
                 + [pl.BlockSpec((1, 1, 6 * d), lambda i: (mod_map((i + blk0) * (tm // ROW_BLOCK)), 0, 0)),
                    _layer_spec(mng, layer), _layer_spec(gng, layer),
                    pl.BlockSpec((hw, hw), lambda i: (0, 0)),
                    _layer_spec(wm, layer, **resident), _layer_spec(wg, layer, **resident),
                    _layer_spec(wa, layer, **resident),
                    _layer_spec(l1g, layer), _layer_spec(l1b, layer),
                    _layer_spec(wfg, layer, **resident), _layer_spec(wfu, layer, **resident),
                    _layer_spec(wfd, layer, **resident),
                    _layer_spec(l2g, layer), _layer_spec(l2b, layer)],
        out_specs=pl.BlockSpec((tm, d), lambda i: (i, 0)),
        out_shape=jax.ShapeDtypeStruct((n_blk * tm, d), F32),
        compiler_params=_cparams(("parallel",)),
        name="out_proj_ffn",
    )(hf, hb, of, ob, a_ctx, a_dec, zm, zg, *x_ops, mods, mng, gng, avg, wm, wg, wa, l1g, l1b,
      wfg, wfu, wfd, l2g, l2b)


def _block_diag_heads(t):
    h = t.shape[-3]
    eye = jnp.eye(h, dtype=t.dtype)
    out = jnp.einsum('...hkv,hg->...hkgv', t, eye)
    return out.reshape(t.shape[:-3] + (h * t.shape[-2], h * t.shape[-1]))


def _rope_tables(dec_len):
    half = A_DROPE // 4
    inv = ROPE_BASE ** (-jnp.arange(half, dtype=F32) / half)
    pos = jnp.arange(dec_len, dtype=jnp.int32)
    rows = (pos // GRID_W).astype(F32)[:, None] * inv[None, :]
    cols = (pos % GRID_W).astype(F32)[:, None] * inv[None, :]
    cr, sr, cc, sc = jnp.cos(rows), jnp.sin(rows), jnp.cos(cols), jnp.sin(cols)
    cos32 = jnp.concatenate([cr, cr, cc, cc], axis=1)
    sin32 = jnp.concatenate([-sr, sr, -sc, sc], axis=1)
    ones_lo = jnp.ones((dec_len, KR_LO), F32)
    ones_hi = jnp.ones((dec_len, A_HPAD - KR_HI), F32)
    cos = jnp.concatenate([ones_lo, cos32, ones_hi], axis=1)
    sin = jnp.concatenate([0 * ones_lo, sin32, 0 * ones_hi], axis=1)
    kcos = jnp.concatenate([0 * ones_lo, cos32, 0 * ones_hi], axis=1)
    lane = jnp.arange(A_HPAD)
    rope_lane = ((lane >= KR_LO) & (lane < KR_HI)).astype(F32)[None, :]
    ctx_cos = jnp.ones((WIDE_ROWS, A_HPAD), F32)
    ctx_sin = jnp.zeros((WIDE_ROWS, A_HPAD), F32)
    ctx_kcos = jnp.broadcast_to(rope_lane, (WIDE_ROWS, A_HPAD))
    return (jnp.concatenate([ctx_cos, cos]), jnp.concatenate([ctx_sin, sin]), jnp.concatenate([ctx_kcos, kcos]))


def _rope_partner(w):
    e = A_DROPE // 4
    return jnp.concatenate([w[..., e:2 * e], w[..., 0:e], w[..., 3 * e:4 * e], w[..., 2 * e:3 * e]], axis=-1)


def kernel(x_prompt, x_sample, state_mlstm_C, state_mlstm_n, state_mlstm_m, state_gla_S, cache_mla_ckv, cache_mla_krope, c, c_ctx, w_ada, b_ada, w_in, m_gate_b, m_norm_g, g_w2, g_b2, g_norm_g, a_q_norm_g, a_kv_norm_g, a_w_uq, a_w_ukv, w_out, ln1_g, ln1_b, w_ffn_gate, w_ffn_up, w_ffn_down, ln2_g, ln2_b):
    n_ctx, ctx_len, d = x_prompt.shape
    n_dec, dec_len, _ = x_sample.shape
    depth = w_in.shape[0]
    past_len = cache_mla_ckv.shape[2]
    assert d == D_MODEL and ctx_len == ROW_BLOCK and dec_len % ROW_BLOCK == 0 and past_len % 8 == 0
    assert (n_ctx * ctx_len) % WIDE_ROWS == 0 and dec_len % WIDE_ROWS == 0
    assert n_dec + 1 <= 8
    n_ctx_rows = n_ctx * ctx_len
    n_dec_rows = n_dec * dec_len
    nt = n_ctx_rows + n_dec_rows
    nb_ctx = n_ctx_rows // ROW_BLOCK
    nb_seq = dec_len // ROW_BLOCK
    alpha = (2.0 * depth) ** 0.25

    blocks = np.arange(nt // ROW_BLOCK)
    dec_b = np.maximum(blocks - nb_ctx, 0)
    is_ctx = blocks < nb_ctx
    plan = {
        "first": jnp.asarray(np.where(is_ctx, 2, (dec_b % nb_seq) == 0).astype(np.int32)),
        "seq": jnp.asarray(np.where(is_ctx, blocks, n_ctx + dec_b // nb_seq).astype(np.int32)),
        "bwd": jnp.asarray(np.where(is_ctx, blocks,
                                    nb_ctx + (dec_b // nb_seq) * nb_seq + nb_seq - 1 - dec_b % nb_seq).astype(np.int32)),
    }

    def mod_map_for(layer):
        def mod_map(i):
            return layer * 8 + jnp.where(i < nb_ctx, 0, 1 + (i - nb_ctx) // nb_seq)
        return mod_map

    cond8 = jnp.zeros((8, d), F32).at[0].set(c_ctx).at[1:1 + n_dec].set(c)
    mods = _ada_call(cond8, w_ada, b_ada).reshape(depth * 8, 1, 6 * d)

    pts = [int(p) for p in np.cumsum(IN_SPLITS)[:-1]]
    (wmq, wmk, wmv, wmo, wmg, wgq, wgk, wgv, wgg, wga, wacq, wackv, wakr) = jnp.split(w_in.astype(BF16), pts, axis=-1)
    zpad = lambda n: jnp.zeros((depth, d, n), BF16)
    s1 = jnp.concatenate([wmg, wga, zpad(KR_LO - 48), wakr, zpad(LANES - KR_HI)], axis=-1)
    s2 = jnp.concatenate([zpad(KR_LO), _rope_partner(wakr), zpad(LANES - KR_HI)], axis=-1)
    w_in_p = jnp.concatenate([wmq, wmk, wmv, wmo, wgq, wgk, wgv, wgg, wacq, wackv, s1, s2], axis=-1)
    w_gt = jnp.swapaxes(wmg, 1, 2)
    gate_b = jnp.zeros((depth, 1, LANES), F32).at[:, 0, 0:16].set(m_gate_b)
    gate_bt = m_gate_b.reshape(depth, 16, 1)
    w2p = jnp.zeros((depth, 2, LANES, G_HEADS * G_DK), F32)
    w2p = w2p.at[:, 0, 16:16 + G_RANK].set(g_w2[:, 0]).at[:, 1, 16 + G_RANK:16 + 2 * G_RANK].set(g_w2[:, 1]).astype(BF16)
    b2p = g_b2.reshape(depth, 2, 1, G_HEADS * G_DK)
    uq = a_w_uq.reshape(depth, A_DQ, A_HEADS, A_DNOPE + A_DROPE)
    hp = lambda n: jnp.zeros((depth, uq.shape[1], A_HEADS, n), F32)
    wuq = jnp.concatenate([uq, hp(A_HPAD - KR_HI)], axis=-1).reshape(depth, A_DQ, -1).astype(BF16)
    wuqs = jnp.concatenate([hp(KR_LO), _rope_partner(uq[..., A_DNOPE:]), hp(A_HPAD - KR_HI)],
                           axis=-1).reshape(depth, A_DQ, -1).astype(BF16)
    ukv = a_w_ukv.reshape(depth, A_DC, A_HEADS, A_DNOPE + A_DV)
    kp = jnp.zeros((depth, A_DC, A_HEADS, A_HPAD - A_DNOPE), F32)
    wk = jnp.concatenate([ukv[..., :A_DNOPE], kp], axis=-1).reshape(depth, A_DC, -1).astype(BF16)
    wv = jnp.concatenate([ukv[..., A_DNOPE:], kp], axis=-1).reshape(depth, A_DC, -1).astype(BF16)
    mw, gw = M_HEADS * M_DV, G_HEADS * G_DV
    wo_m = w_out[:, 0:mw].astype(BF16)
    wo_g = w_out[:, mw:mw + gw].astype(BF16)
    wo_a = w_out[:, mw + gw:].reshape(depth, A_HEADS, A_DV, d)
    wo_a = jnp.concatenate([wo_a, jnp.zeros((depth, A_HEADS, A_HPAD - A_DV, d), F32)], axis=2)
    wo_a = wo_a.reshape(depth, A_HEADS * A_HPAD, d).astype(BF16)
    wfg, wfu, wfd = w_ffn_gate.astype(BF16), w_ffn_up.astype(BF16), w_ffn_down.astype(BF16)

    sconst = _scan_consts()
    head_lanes = jnp.asarray((np.arange(M_HEADS)[:, None] == (np.arange(S_W) // M_DV)[None, :]).astype(np.float32))
    hi = np.arange(mw) // M_DV
    avg = jnp.asarray((hi[:, None] == hi[None, :]).astype(np.float32) / M_DV, BF16)
    cos, sin, kcos = _rope_tables(dec_len)
    nbw_ctx = n_ctx_rows // WIDE_ROWS
    nbw_seq = dec_len // WIDE_ROWS

    def rope_map(i):
        return jnp.where(i < nbw_ctx, 0, 1 + (i - nbw_ctx) % nbw_seq)

    krp_all = jnp.zeros(cache_mla_krope.shape[:-1] + (LANES,), F32).at[..., KR_LO:KR_HI].set(cache_mla_krope)
    n_bd = jnp.einsum('bldhk,hg->bldhkg', state_mlstm_n, head_lanes)
    cn0 = jnp.concatenate([_block_diag_heads(state_mlstm_C),
                           n_bd.reshape(n_dec, depth, 2, M_HEADS * M_DK, S_W)], axis=-1)
    m0 = jnp.repeat(state_mlstm_m, M_DV, axis=-1)[:, :, :, None, :]
    s0 = _block_diag_heads(state_gla_S)
    vec = lambda p: p[:, None, :]

    x = (x_prompt.reshape(n_ctx_rows, d), x_sample.reshape(n_dec_rows, d))
    new_c, new_n, new_m, new_s, new_ckv, new_kr = [], [], [], [], [], []
    for l in range(depth):
        mod_map = mod_map_for(l)
        zm, zg, zs, zt, q, k, v, ckv, kr = _in_mla_call(
            l, x, mods, mod_map, rope_map, w_in_p, w_gt, cos, sin, kcos,
            vec(a_q_norm_g), vec(a_kv_norm_g), wuq, wuqs, wk, wv)

        (hf, hb, c_fin, n_fin, m_fin), (of, ob, s_fin) = _scan_call(
            plan,
            [_mlstm_part(l, n_ctx, zm, zs, zt, cn0, m0, gate_b, gate_bt, sconst),
             _gla_part(l, n_ctx, zg, zs, s0, w2p, b2p, sconst)],
            "mixer_scan")

        kc, vc = _cache_kv_call(l, cache_mla_ckv, krp_all, wk, wv)
        a_ctx = _attn_ctx_call(q, k, v, n_ctx_rows)
        a_dec = _attn_dec_call(q, k, v, kc, vc, n_ctx_rows, n_dec, dec_len, past_len)

        out_ffn = functools.partial(
            _out_ffn_call, l, hf, hb, of, ob, a_ctx, a_dec, zm, zg, x, mods, mod_map,
            vec(m_norm_g), vec(g_norm_g), avg, wo_m, wo_g, wo_a, vec(ln1_g), vec(ln1_b),
            wfg, wfu, wfd, vec(ln2_g), vec(ln2_b), alpha)
        if l + 1 < depth:
            x = out_ffn()
        else:
            y_prompt = out_ffn(blk0=0, n_blk=nbw_ctx).reshape(n_ctx, ctx_len, d)
            y_sample = out_ffn(blk0=nbw_ctx, n_blk=n_dec * nbw_seq).reshape(n_dec, dec_len, d)

        new_c.append(c_fin[:n_ctx])
        new_n.append(n_fin[:n_ctx, ..., 0])
        new_m.append(m_fin[:n_ctx, :, 0, ::M_DV])
        new_s.append(s_fin[:n_ctx])
        new_ckv.append(ckv[:n_ctx_rows].reshape(n_ctx, ctx_len, A_DC))
        new_kr.append(kr[:n_ctx_rows].reshape(n_ctx, ctx_len, A_DROPE))

    st = lambda xs: jnp.stack(xs, axis=1)
    return (y_prompt, y_sample, st(new_c), st(new_n), st(new_m), st(new_s), st(new_ckv), st(new_kr))
```

```python
import functools

import numpy as np
import jax
import jax.numpy as jnp
from jax import lax
from jax.experimental import pallas as pl
from jax.experimental.pallas import tpu as pltpu

F32 = jnp.float32
BF16 = jnp.bfloat16

D_MODEL = 1024
GRID_W = 64
M_HEADS, M_DK, M_DV = 4, 32, 64
G_HEADS, G_DK, G_DV = 4, 32, 64
G_RANK = 16
G_TAU = 16.0
A_HEADS, A_DNOPE, A_DROPE, A_DV = 8, 64, 32, 64
A_DQ, A_DC = 256, 128
A_SCALE = (A_DNOPE + A_DROPE) ** -0.5
ROPE_BASE = 10000.0
IN_SPLITS = (M_HEADS * M_DK, M_HEADS * M_DK, M_HEADS * M_DV, M_HEADS * M_DV, 4 * M_HEADS,
             G_HEADS * G_DK, G_HEADS * G_DK, G_HEADS * G_DV, G_HEADS * G_DV, 2 * G_RANK,
             A_DQ, A_DC, A_DROPE)

LANES = 128
MXU_TILE = 256
VMEM_LIMIT = 56 * 1024 * 1024
ROW_BLOCK = 256
WIDE_ROWS = 512
S_CHUNK = 64
S_W = M_HEADS * M_DV
G_LEVELS = 6
G_SMALL_LEVELS = 3
A_HPAD = 128
A_KEY_TILE = 512
FFN_CHUNKS = 2

ZM_W = 768
ZG_W = 768
ZQ_W = 384
ZS_W = 256
Z_W = ZM_W + ZG_W + ZQ_W + ZS_W
KR_LO, KR_HI = A_DNOPE, A_DNOPE + A_DROPE

assert (M_HEADS, M_DK, M_DV) == (G_HEADS, G_DK, G_DV) and S_CHUNK == M_DV and M_HEADS * M_DK == LANES


def _cparams(sem):
    return pltpu.CompilerParams(dimension_semantics=sem, vmem_limit_bytes=VMEM_LIMIT)


def _f32dot(a, b):
    return jnp.dot(a, b, preferred_element_type=F32)


NT_DIMS = (((1,), (1,)), ((), ()))
TN_DIMS = (((0,), (0,)), ((), ()))


def _split(x):
    hi = x.astype(BF16)
    lo = (x - hi.astype(F32)).astype(BF16)
    return hi, lo


def _dot_c_x(c, x):
    hi, lo = _split(x)
    return _f32dot(c, hi) + _f32dot(c, lo)


def _dot_x_c(x, c):
    hi, lo = _split(x)
    return _f32dot(hi, c) + _f32dot(lo, c)


def _log_sigmoid(x):
    return jnp.minimum(x, 0.0) - jnp.log(1.0 + jnp.exp(-jnp.abs(x)))


def _sigmoid(x):
    return 1.0 / (1.0 + jnp.exp(-x))


def _rms(x, g, eps=1e-6):
    return x * lax.rsqrt(jnp.mean(x * x, axis=-1, keepdims=True) + eps) * g


def _layer_norm(y, g, b, eps=1e-5):
    yc = y - jnp.mean(y, axis=-1, keepdims=True)
    return yc * lax.rsqrt(jnp.mean(yc * yc, axis=-1, keepdims=True) + eps) * g + b


def _with_ones_lane(v):
    lane = lax.broadcasted_iota(jnp.int32, v.shape, 1) % A_HPAD
    return jnp.where(lane == A_DV, 1.0, v)


def _x_specs(x, tm, blk0=0):
    if not isinstance(x, tuple):
        return [pl.BlockSpec((tm, x.shape[1]), lambda i: (i + blk0, 0))], [x]
    nb0 = x[0].shape[0] // tm
    return ([pl.BlockSpec((tm, x[0].shape[1]), lambda i: (jnp.minimum(i + blk0, nb0 - 1), 0)),
             pl.BlockSpec((tm, x[1].shape[1]), lambda i: (jnp.maximum(i + blk0 - nb0, 0), 0))], list(x))


def _read_x(refs, nb0, blk0=0):
    if len(refs) == 1:
        return refs[0][...]
    return jnp.where(pl.program_id(0) + blk0 < nb0, refs[0][...], refs[1][...])


def _ada_kernel(c_ref, w_ref, b_ref, o_ref):
    c = c_ref[...]
    s = c * _sigmoid(c)
    o_ref[0] = _f32dot(s.astype(BF16), w_ref[0].astype(BF16)) + b_ref[0]


def _ada_call(cond8, w_ada, b_ada):
    depth, d, n = w_ada.shape
    tn = 1536
    return pl.pallas_call(
        _ada_kernel,
        grid=(depth, n // tn),
        in_specs=[pl.BlockSpec((8, d), lambda l, j: (0, 0)),
                  pl.BlockSpec((1, d, tn), lambda l, j: (l, 0, j)),
                  pl.BlockSpec((1, 1, tn), lambda l, j: (l, 0, j))],
        out_specs=pl.BlockSpec((1, 8, tn), lambda l, j: (l, 0, j)),
        out_shape=jax.ShapeDtypeStruct((depth, 8, n), F32),
        compiler_params=_cparams(("arbitrary", "arbitrary")),
        name="ada_mod",
    )(cond8, w_ada, b_ada.reshape(depth, 1, n))


def _in_mla_kernel(*refs, n_x, nb0):
    x_refs = refs[:n_x]
    (mod_ref, w_ref, wgt_ref, cos_ref, sin_ref, kcos_ref, gq_ref, gkv_ref, wuq_ref, wuqs_ref, wk_ref, wv_ref,
     zm_ref, zg_ref, zs_ref, zt_ref, q_ref, k_ref, v_ref, ckv_ref, kr_ref) = refs[n_x:]
    d = D_MODEL
    mod = mod_ref[0]
    h = (_read_x(x_refs, nb0) * (1.0 + mod[:, d:2 * d]) + mod[:, 0:d]).astype(BF16)
    z = _f32dot(h, w_ref[...])
    zm_ref[...] = z[:, 0:ZM_W]
    zg_ref[...] = z[:, ZM_W:ZM_W + ZG_W]
    zs = z[:, ZM_W + ZG_W + ZQ_W:Z_W]
    zs_ref[...] = zs[:, 0:LANES]
    kr_ref[...] = zs[:, KR_LO:KR_HI]
    zt_ref[...] = lax.dot_general(wgt_ref[...], h, NT_DIMS, preferred_element_type=F32)
    zq = z[:, ZM_W + ZG_W:ZM_W + ZG_W + ZQ_W]
    cq = _rms(zq[:, 0:A_DQ], gq_ref[...]).astype(BF16)
    qn = _f32dot(cq, wuq_ref[...])
    qs = _f32dot(cq, wuqs_ref[...])
    cos = cos_ref[...]
    sin = sin_ref[...]
    for hd in range(A_HEADS):
        sl = slice(hd * A_HPAD, (hd + 1) * A_HPAD)
        q_ref[:, sl] = ((qn[:, sl] * cos + qs[:, sl] * sin) * A_SCALE).astype(BF16)
    ckv = _rms(zq[:, A_DQ:A_DQ + A_DC], gkv_ref[...])
    ckv_ref[...] = ckv
    ckvb = ckv.astype(BF16)
    kn = _f32dot(ckvb, wk_ref[...])
    v_ref[...] = _with_ones_lane(_f32dot(ckvb, wv_ref[...])).astype(BF16)
    kr = zs[:, 0:LANES] * kcos_ref[...] + zs[:, LANES:2 * LANES] * sin
    for hd in range(A_HEADS):
        sl = slice(hd * A_HPAD, (hd + 1) * A_HPAD)
        k_ref[:, sl] = (kn[:, sl] + kr).astype(BF16)


def _layer_spec(arr, layer, **kw):
    zeros = (0,) * (arr.ndim - 1)
    return pl.BlockSpec((None,) + arr.shape[1:], lambda *_: (layer,) + zeros, **kw)


def _in_mla_call(layer, x, mods, mod_map, rope_map, w_in_p, w_gt, cos, sin, kcos, gq, gkv, wuq, wuqs, wk, wv):
    tm = WIDE_ROWS
    x_specs, x_ops = _x_specs(x, tm)
    nt = sum(a.shape[0] for a in x_ops)
    d = D_MODEL
    hw = A_HEADS * A_HPAD
    row = lambda i: (i, 0)
    rope = lambda i: (rope_map(i), 0)
    resident = dict(pipeline_mode=pl.Buffered(1))
    return pl.pallas_call(
        functools.partial(_in_mla_kernel, n_x=len(x_ops), nb0=x_ops[0].shape[0] // tm),
        grid=(nt // tm,),
        in_specs=x_specs
                 + [pl.BlockSpec((1, 1, 6 * d), lambda i: (mod_map(i * (tm // ROW_BLOCK)), 0, 0)),
                    _layer_spec(w_in_p, layer, **resident), _layer_spec(w_gt, layer),
                    pl.BlockSpec((tm, LANES), rope), pl.BlockSpec((tm, LANES), rope), pl.BlockSpec((tm, LANES), rope),
                    _layer_spec(gq, layer), _layer_spec(gkv, layer),
                    _layer_spec(wuq, layer, **resident), _layer_spec(wuqs, layer, **resident),
                    _layer_spec(wk, layer, **resident), _layer_spec(wv, layer, **resident)],
        out_specs=[pl.BlockSpec((tm, ZM_W), row), pl.BlockSpec((tm, ZG_W), row), pl.BlockSpec((tm, LANES), row),
                   pl.BlockSpec((16, tm), lambda i: (0, i)),
                   pl.BlockSpec((tm, hw), row), pl.BlockSpec((tm, hw), row), pl.BlockSpec((tm, hw), row),
                   pl.BlockSpec((tm, A_DC), row), pl.BlockSpec((tm, A_DROPE), row)],
        out_shape=[jax.ShapeDtypeStruct((nt, ZM_W), F32), jax.ShapeDtypeStruct((nt, ZG_W), F32),
                   jax.ShapeDtypeStruct((nt, LANES), F32), jax.ShapeDtypeStruct((16, nt), F32),
                   jax.ShapeDtypeStruct((nt, hw), BF16), jax.ShapeDtypeStruct((nt, hw), BF16),
                   jax.ShapeDtypeStruct((nt, hw), BF16), jax.ShapeDtypeStruct((nt, A_DC), F32),
                   jax.ShapeDtypeStruct((nt, A_DROPE), F32)],
        compiler_params=_cparams(("parallel",)),
        name="in_proj_mla",
    )(*x_ops, mods, w_in_p, w_gt, cos, sin, kcos, gq, gkv, wuq, wuqs, wk, wv)


def _scan_consts():
    n = S_CHUNK
    t = np.arange(n)[:, None]
    s = np.arange(n)[None, :]
    tri = [(s <= t), (s >= t)]
    head_of = np.arange(S_W) // M_DV
    khead = np.arange(M_HEADS * M_DK) // M_DK
    kmask = (head_of[:, None] == khead[None, :]).astype(np.float32)
    vmask = (head_of[:, None] == head_of[None, :]).astype(np.float32)
    bd = (khead[:, None] == head_of[None, :]).astype(np.float32)
    sel, tritile, neg = [], [], []
    for d in (0, 1):
        io, fo = 8 * d, 8 * d + 4
        m = np.zeros((LANES, 3 * S_W), np.float32)
        for h in range(M_HEADS):
            m[fo + h, h * M_DV:(h + 1) * M_DV] = 1.0
            m[io + h, S_W + h * M_DV:S_W + (h + 1) * M_DV] = 1.0
            m[fo + h, 2 * S_W + h * M_DK:2 * S_W + (h + 1) * M_DK] = 1.0
            m[io + h, 2 * S_W + LANES + h * M_DK:2 * S_W + LANES + (h + 1) * M_DK] = 1.0
        sel.append(m)
        tritile.append(np.tile(tri[d].T.astype(np.float32), (1, M_HEADS)))
        neg.append(np.tile(np.where(tri[d], 0.0, -np.inf).astype(np.float32), (1, M_HEADS)))
    masks = [np.eye(n)]
    for lev in range(G_LEVELS):
        b = 1 << lev
        right = (t % (2 * b)) >= b
        same_parent = (t // (2 * b)) == (s // (2 * b))
        masks.append((same_parent & right & ((s % (2 * b)) < b)).astype(np.float32))
    flip = lambda m: m[::-1, ::-1]
    mcat = np.stack([np.stack([np.tile(m, (1, G_HEADS)) for m in masks]),
                     np.stack([np.tile(flip(m), (1, G_HEADS)) for m in masks])])
    gath = []
    for d in (0, 1):
        g = np.zeros((G_SMALL_LEVELS * n, n), np.float32)
        for lev in range(G_SMALL_LEVELS):
            for row in range(n):
                g[lev * n + row, _gla_boundary(row, 1 << lev, d)] = 1.0
        gath.append(g)
    f32 = lambda a: jnp.asarray(a, F32)
    b16 = lambda a: jnp.asarray(a, BF16)
    return dict(tri=b16(np.stack(tri).astype(np.float32)), sel=b16(np.stack(sel)), tritile=b16(np.stack(tritile)),
                neg=f32(np.stack(neg)), kmask=b16(kmask), vmask=b16(vmask), bd=f32(bd),
                bd2=f32(np.concatenate([bd, bd], axis=1)), mcat=f32(mcat), gath=b16(np.stack(gath)))


def _gla_boundary(row, b, d):
    pstart = (row // (2 * b)) * (2 * b)
    return pstart + b - 1 if d == 0 else pstart + b


def _scan_units():
    nch = ROW_BLOCK // S_CHUNK
    return [(d, ci if d == 0 else nch - 1 - ci) for ci in range(nch) for d in (0, 1)]


def _chunk_rows(c):
    return slice(c * S_CHUNK, (c + 1) * S_CHUNK)


def _mlstm_body(first_ref, seq_ref, bwd_ref,
                zmf_ref, zsf_ref, ztf_ref, zmb_ref, zsb_ref, ztb_ref,
                cn0_ref, m0_ref, gb_ref, gbt_ref,
                tri_ref, sel_ref, tritile_ref, neg_ref, kmask_ref, vmask_ref, bd2_ref,
                hf_ref, hb_ref, cout_ref, nout_ref, mout_ref,
                cns_ref, ms_ref):
    r = pl.program_id(0)
    n = S_CHUNK
    qk_w = M_HEADS * M_DK
    kmask = kmask_ref[...]
    vmask = vmask_ref[...]
    bd2 = bd2_ref[...]
    units = _scan_units()
    rows = _chunk_rows
    zm_refs, zs_refs, zt_refs, h_refs = (zmf_ref, zmb_ref), (zsf_ref, zsb_ref), (ztf_ref, ztb_ref), (hf_ref, hb_ref)
    fo = lambda d: 8 * d + 4
    last = lambda d: n - 1 if d == 0 else 0
    lane = lax.broadcasted_iota(jnp.int32, (n, LANES), 1)
    head_row = lax.broadcasted_iota(jnp.int32, (1, S_W), 1) // M_DV

    def per_unit(f):
        return [f(d, c) for d, c in units]

    def dot_split(c_left, x_split, c_right):
        hi, lo = x_split
        if c_left is not None:
            return _f32dot(c_left, hi) + _f32dot(c_left, lo)
        return _f32dot(hi, c_right) + _f32dot(lo, c_right)

    q = per_unit(lambda d, c: zm_refs[d][rows(c), 0:qk_w].astype(BF16))
    k = per_unit(lambda d, c: zm_refs[d][rows(c), qk_w:2 * qk_w] * (M_DK ** -0.5))
    v = per_unit(lambda d, c: zm_refs[d][rows(c), 2 * qk_w:2 * qk_w + S_W])
    graw = per_unit(lambda d, c: zs_refs[d][rows(c), :] + gb_ref[...])
    graw_t = per_unit(lambda d, c: zt_refs[d][:, rows(c)] + gbt_ref[...])
    lf = [_split(_log_sigmoid(g)) for g in graw]
    lf_t = [_split(_log_sigmoid(g)) for g in graw_t]
    yield
    bcol = [dot_split(tri_ref[d], s, None) for (d, _), s in zip(units, lf)]
    xs = [_split(jnp.where((lane >= fo(d)) & (lane < fo(d) + M_HEADS), b, g))
          for (d, _), b, g in zip(units, bcol, graw)]
    yield
    ex = [dot_split(None, s, sel_ref[d]) for (d, _), s in zip(units, xs)]
    yield
    b256 = [e[:, 0:S_W] for e in ex]
    i256 = [e[:, S_W:2 * S_W] for e in ex]
    b128 = [e[:, 2 * S_W:2 * S_W + LANES] for e in ex]
    i128 = [e[:, 2 * S_W + LANES:] for e in ex]
    rows_t = [dot_split(None, s, tritile_ref[d]) for (d, _), s in zip(units, lf_t)]
    yield
    b_row = []
    for (d, _), rt in zip(units, rows_t):
        br = jnp.zeros((1, S_W), F32)
        for h in range(M_HEADS):
            br = jnp.where(head_row == h, rt[fo(d) + h:fo(d) + h + 1, :], br)
        b_row.append(br)
    decay = [jnp.exp(b - br + neg_ref[d]) for (d, _), b, br in zip(units, b256, b_row)]
    yield
    k_bd = [jnp.concatenate([ki.astype(BF16)] * M_HEADS, axis=0) * kmask for ki in k]
    qk = [lax.dot_general(qi, ki, NT_DIMS, preferred_element_type=F32) for qi, ki in zip(q, k_bd)]
    yield
    p = [(a * b).astype(BF16) for a, b in zip(qk, decay)]
    i_max = [jnp.max(i, axis=0, keepdims=True) for i in i256]
    scale = [jnp.exp(i - im) for i, im in zip(i256, i_max)]
    yield
    ve = [jnp.concatenate([jnp.concatenate([(vi * sc).astype(BF16)] * M_HEADS, axis=0) * vmask,
                           jnp.concatenate([sc.astype(BF16)] * M_HEADS, axis=0) * vmask], axis=1)
          for vi, sc in zip(v, scale)]
    yield
    nd = [_f32dot(pi, vei) for pi, vei in zip(p, ve)]
    yield
    bl256 = [b[last(d):last(d) + 1, :] for (d, _), b in zip(units, b256)]
    g_max = [jnp.max(bl - b + i, axis=0, keepdims=True) for bl, b, i in zip(bl256, b256, i256)]
    g128 = [b[last(d):last(d) + 1, :] - b + i for (d, _), b, i in zip(units, b128, i128)]
    w = [jnp.exp(g - jnp.max(g, axis=0, keepdims=True)) for g in g128]
    kw = [(ki * wi).astype(BF16) for ki, wi in zip(k, w)]
    yield
    ones = jnp.ones((n, S_W), BF16)
    upd = [lax.dot_general(kwi, jnp.concatenate([vi.astype(BF16), ones], axis=1), TN_DIMS,
                           preferred_element_type=F32) for kwi, vi in zip(kw, v)]
    fresh = first_ref[r] != 0
    zero = first_ref[r] == 2
    state = [(jnp.where(fresh, jnp.where(zero, 0.0, cn0_ref[0, d]), cns_ref[d]),
              jnp.where(fresh, jnp.where(zero, 0.0, m0_ref[0, d]), ms_ref[d])) for d in (0, 1)]
    outs = []
    for i, (d, c) in enumerate(units):
        cn, m = state[d]
        qcn = _f32dot(q[i], cn.astype(BF16))
        a = b256[i] + m
        sig = jnp.maximum(i_max[i], a)
        f1 = jnp.exp(i_max[i] - sig)
        f2 = jnp.exp(a - sig)
        num = f1 * nd[i][:, 0:S_W] + f2 * qcn[:, 0:S_W]
        den = f1 * nd[i][:, S_W:] + f2 * qcn[:, S_W:]
        outs.append(num / jnp.maximum(jnp.abs(den), jnp.exp(-sig)))
        m_new = jnp.maximum(bl256[i] + m, g_max[i])
        alpha = jnp.exp(bl256[i] + m - m_new)
        beta = jnp.exp(g_max[i] - m_new)
        state[d] = (jnp.concatenate([alpha, alpha], axis=1) * cn
                    + (jnp.concatenate([beta, beta], axis=1) * bd2) * upd[i], m_new)
        yield
    for (d, c), o in zip(units, outs):
        h_refs[d][rows(c), :] = o
    for d in (0, 1):
        cn, m = state[d]
        cns_ref[d], ms_ref[d] = cn, m
        mout_ref[0, d] = m
        for h in range(M_HEADS):
            blk = cn[h * M_DK:(h + 1) * M_DK, :]
            cout_ref[0, d, h] = blk[:, h * M_DV:(h + 1) * M_DV]
            nout_ref[0, d, h] = blk[:, S_W + h * M_DV:S_W + (h + 1) * M_DV]


def _mlstm_part(layer, n_ctx, zm, zs, zt, cn0, m0, gb, gbt, sc):
    nt = zm.shape[0]
    kd = M_HEADS * M_DK
    n = S_CHUNK
    fwd = lambda r, first, seq, bwd: (r, 0)
    bwd_ = lambda r, first, seq, bwd: (bwd[r], 0)
    fwd_t = lambda r, first, seq, bwd: (0, r)
    bwd_t = lambda r, first, seq, bwd: (0, bwd[r])
    seq5 = lambda r, first, seq, bwd: (jnp.maximum(seq[r] - n_ctx, 0), layer, 0, 0, 0)
    out4 = lambda r, first, seq, bwd: (jnp.minimum(seq[r], n_ctx), 0, 0, 0)
    out5 = lambda r, first, seq, bwd: (jnp.minimum(seq[r], n_ctx), 0, 0, 0, 0)
    const = lambda r, first, seq, bwd: (0, 0)
    const3 = lambda r, first, seq, bwd: (0, 0, 0)
    return dict(
        body=_mlstm_body,
        in_specs=[pl.BlockSpec((ROW_BLOCK, 512), fwd), pl.BlockSpec((ROW_BLOCK, LANES), fwd),
                  pl.BlockSpec((16, ROW_BLOCK), fwd_t),
                  pl.BlockSpec((ROW_BLOCK, 512), bwd_), pl.BlockSpec((ROW_BLOCK, LANES), bwd_),
                  pl.BlockSpec((16, ROW_BLOCK), bwd_t),
                  pl.BlockSpec((1, None, 2, kd, 2 * S_W), seq5), pl.BlockSpec((1, None, 2, 1, S_W), seq5),
                  _layer_spec(gb, layer), _layer_spec(gbt, layer),
                  pl.BlockSpec((2, n, n), const3), pl.BlockSpec((2, LANES, 3 * S_W), const3),
                  pl.BlockSpec((2, n, S_W), const3), pl.BlockSpec((2, n, S_W), const3),
                  pl.BlockSpec((S_W, kd), const), pl.BlockSpec((S_W, S_W), const),
                  pl.BlockSpec((kd, 2 * S_W), const)],
        out_specs=[pl.BlockSpec((ROW_BLOCK, S_W), fwd), pl.BlockSpec((ROW_BLOCK, S_W), bwd_),
                   pl.BlockSpec((1, 2, M_HEADS, M_DK, M_DV), out5), pl.BlockSpec((1, 2, M_HEADS, M_DK, M_DV), out5),
                   pl.BlockSpec((1, 2, 1, S_W), out4)],
        scratch_shapes=[pltpu.VMEM((2, kd, 2 * S_W), F32), pltpu.VMEM((2, 1, S_W), F32)],
        out_shape=[jax.ShapeDtypeStruct((nt, S_W), F32), jax.ShapeDtypeStruct((nt, S_W), F32),
                   jax.ShapeDtypeStruct((n_ctx + 1, 2, M_HEADS, M_DK, M_DV), F32),
                   jax.ShapeDtypeStruct((n_ctx + 1, 2, M_HEADS, M_DK, M_DV), F32),
                   jax.ShapeDtypeStruct((n_ctx + 1, 2, 1, S_W), F32)],
        operands=[zm, zs, zt, zm, zs, zt, cn0, m0, gb, gbt,
                  sc["tri"], sc["sel"], sc["tritile"], sc["neg"], sc["kmask"], sc["vmask"], sc["bd2"]])


def _gla_body(first_ref, seq_ref, bwd_ref,
              zgf_ref, zsf_ref, zgb_ref, zsb_ref, s0_ref, w2_ref, b2_ref,
              tri_ref, gath_ref, mcat_ref, kmask_ref, vmask_ref, bd_ref,
              of_ref, ob_ref, sout_ref, ss_ref):
    r = pl.program_id(0)
    n = S_CHUNK
    kd = G_HEADS * G_DK
    vd = G_HEADS * G_DV
    ones_v = jnp.ones((n, vd), BF16)
    kmask = kmask_ref[...]
    vmask = vmask_ref[...]
    bd = bd_ref[...]
    units = _scan_units()
    rows = _chunk_rows
    zg_refs, zs_refs, o_refs = (zgf_ref, zgb_ref), (zsf_ref, zsb_ref), (of_ref, ob_ref)

    def per_unit(f):
        return [f(d, c) for d, c in units]

    def bd_k(kt):
        return jnp.concatenate([kt] * G_HEADS, axis=0) * kmask

    q = per_unit(lambda d, c: zg_refs[d][rows(c), 0:kd] * (G_DK ** -0.5))
    k = per_unit(lambda d, c: zg_refs[d][rows(c), kd:2 * kd])
    v = per_unit(lambda d, c: zg_refs[d][rows(c), 2 * kd:2 * kd + vd].astype(BF16))
    x = per_unit(lambda d, c: _f32dot(zs_refs[d][rows(c), :].astype(BF16), w2_ref[d]) + b2_ref[d])
    lg = [_log_sigmoid(xi) * (1.0 / G_TAU) for xi in x]
    lg_split = [_split(l) for l in lg]
    yield
    bc = [_f32dot(tri_ref[d], hi) + _f32dot(tri_ref[d], lo)
          for (d, _), (hi, lo) in zip(units, lg_split)]
    bc_split = [_split(b) for b in bc]
    yield
    r_small = [_f32dot(gath_ref[d], hi) + _f32dot(gath_ref[d], lo)
               for (d, _), (hi, lo) in zip(units, bc_split)]
    qb = [qi.astype(BF16) for qi in q]
    kbd = [bd_k(ki.astype(BF16)) for ki in k]
    yield
    att = [mcat_ref[d, 0] * lax.dot_general(qi, ki, NT_DIMS, preferred_element_type=F32)
           for (d, _), qi, ki in zip(units, qb, kbd)]
    for lev in range(G_LEVELS):
        b = 1 << lev
        if lev < G_SMALL_LEVELS:
            r_lev = [rs[lev * n:(lev + 1) * n] for rs in r_small]
        else:
            r_lev = [jnp.concatenate(
                [jnp.broadcast_to(bci[_gla_boundary(p0, b, d):_gla_boundary(p0, b, d) + 1, :], (2 * b, kd))
                 for p0 in range(0, n, 2 * b)], axis=0) for (d, _), bci in zip(units, bc)]
        e_lev = [jnp.exp(-jnp.abs(bci - ri)) for bci, ri in zip(bc, r_lev)]
        qt = [(qi * ei).astype(BF16) for qi, ei in zip(q, e_lev)]
        kt = [bd_k((ki * ei).astype(BF16)) for ki, ei in zip(k, e_lev)]
        yield
        p = [lax.dot_general(qi, ki, NT_DIMS, preferred_element_type=F32) for qi, ki in zip(qt, kt)]
        att = [ai + mcat_ref[d, lev + 1] * pi for (d, _), ai, pi in zip(units, att, p)]
        yield
    v_bd = [jnp.concatenate([vi] * G_HEADS, axis=0) * vmask for vi in v]
    intra = [_f32dot(ai.astype(BF16), vi) for ai, vi in zip(att, v_bd)]
    yield
    q_inc = [(qi * jnp.exp(bci)).astype(BF16) for qi, bci in zip(q, bc)]
    k_suf = [(ki * jnp.exp(bci[(n - 1 if d == 0 else 0):(n if d == 0 else 1), :] - bci)).astype(BF16)
             for (d, _), ki, bci in zip(units, k, bc)]
    decay = [jnp.exp(lax.dot_general(hi, ones_v, TN_DIMS, preferred_element_type=F32)
                     + lax.dot_general(lo, ones_v, TN_DIMS, preferred_element_type=F32))
             for hi, lo in lg_split]
    upd = [bd * lax.dot_general(ki, vi, TN_DIMS, preferred_element_type=F32) for ki, vi in zip(k_suf, v)]
    yield
    fresh = first_ref[r] != 0
    zero = first_ref[r] == 2
    states = [jnp.where(fresh, jnp.where(zero, 0.0, s0_ref[0, d]), ss_ref[d]) for d in (0, 1)]
    outs = []
    for i, (d, c) in enumerate(units):
        inter = _f32dot(q_inc[i], states[d].astype(BF16))
        outs.append(inter + intra[i])
        states[d] = decay[i] * states[d] + upd[i]
        yield
    for (d, c), o in zip(units, outs):
        o_refs[d][rows(c), :] = o
    for d in (0, 1):
        ss_ref[d] = states[d]
        for h in range(G_HEADS):
            sout_ref[0, d, h] = states[d][h * G_DK:(h + 1) * G_DK, h * G_DV:(h + 1) * G_DV]


def _gla_part(layer, n_ctx, zg, zs, s0, w2p, b2p, sc):
    nt = zg.shape[0]
    kd = G_HEADS * G_DK
    vd = G_HEADS * G_DV
    n = S_CHUNK
    fwd = lambda r, first, seq, bwd: (r, 0)
    bwd_ = lambda r, first, seq, bwd: (bwd[r], 0)
    seq5 = lambda r, first, seq, bwd: (jnp.maximum(seq[r] - n_ctx, 0), layer, 0, 0, 0)
    out5 = lambda r, first, seq, bwd: (jnp.minimum(seq[r], n_ctx), 0, 0, 0, 0)
    const = lambda r, first, seq, bwd: (0, 0)
    const3 = lambda r, first, seq, bwd: (0, 0, 0)
    const4 = lambda r, first, seq, bwd: (0, 0, 0, 0)
    return dict(
        body=_gla_body,
        in_specs=[pl.BlockSpec((ROW_BLOCK, 512), fwd), pl.BlockSpec((ROW_BLOCK, LANES), fwd),
                  pl.BlockSpec((ROW_BLOCK, 512), bwd_), pl.BlockSpec((ROW_BLOCK, LANES), bwd_),
                  pl.BlockSpec((1, None, 2, kd, vd), seq5),
                  _layer_spec(w2p, layer), _layer_spec(b2p, layer),
                  pl.BlockSpec((2, n, n), const3),
                  pl.BlockSpec((2, G_SMALL_LEVELS * n, n), const3),
                  pl.BlockSpec((2, G_LEVELS + 1, n, G_HEADS * n), const4),
                  pl.BlockSpec((G_HEADS * n, kd), const), pl.BlockSpec((G_HEADS * n, vd), const),
                  pl.BlockSpec((kd, vd), const)],
        out_specs=[pl.BlockSpec((ROW_BLOCK, vd), fwd), pl.BlockSpec((ROW_BLOCK, vd), bwd_),
                   pl.BlockSpec((1, 2, G_HEADS, G_DK, G_DV), out5)],
        scratch_shapes=[pltpu.VMEM((2, kd, vd), F32)],
        out_shape=[jax.ShapeDtypeStruct((nt, vd), F32), jax.ShapeDtypeStruct((nt, vd), F32),
                   jax.ShapeDtypeStruct((n_ctx + 1, 2, G_HEADS, G_DK, G_DV), F32)],
        operands=[zg, zs, zg, zs, s0, w2p, b2p, sc["tri"], sc["gath"], sc["mcat"],
                  sc["kmask"], sc["vmask"], sc["bd"]])


_DONE = object()


def _scan_call(plan, parts, name):
    n_in = [len(p["operands"]) for p in parts]
    n_out = [len(p["out_specs"]) for p in parts]
    n_scr = [len(p["scratch_shapes"]) for p in parts]

    def kern(first_ref, seq_ref, bwd_ref, *refs):
        ins, outs, scrs = refs[:sum(n_in)], refs[sum(n_in):sum(n_in) + sum(n_out)], refs[sum(n_in) + sum(n_out):]
        @pl.when(pl.program_id(0) == 0)
        def _():
            for s in scrs:
                s[...] = jnp.zeros(s.shape, s.dtype)

        oi = oo = os_ = 0
        bodies = []
        for p, a, b, c in zip(parts, n_in, n_out, n_scr):
            bodies.append(p["body"](first_ref, seq_ref, bwd_ref, *ins[oi:oi + a], *outs[oo:oo + b],
                                    *scrs[os_:os_ + c]))
            oi, oo, os_ = oi + a, oo + b, os_ + c
        while bodies:
            bodies = [g for g in bodies if next(g, _DONE) is not _DONE]

    nb = plan["first"].shape[0]
    grid_spec = pltpu.PrefetchScalarGridSpec(
        num_scalar_prefetch=3,
        grid=(nb,),
        in_specs=[s for p in parts for s in p["in_specs"]],
        out_specs=[s for p in parts for s in p["out_specs"]],
        scratch_shapes=[s for p in parts for s in p["scratch_shapes"]],
    )
    outs = pl.pallas_call(
        kern,
        grid_spec=grid_spec,
        out_shape=[s for p in parts for s in p["out_shape"]],
        compiler_params=_cparams(("arbitrary",)),
        name=name,
    )(plan["first"], plan["seq"], plan["bwd"], *[o for p in parts for o in p["operands"]])
    res, oo = [], 0
    for b in n_out:
        res.append(outs[oo:oo + b])
        oo += b
    return res


def _cache_kv_kernel(ckv_ref, kr_ref, wk_ref, wv_ref, k_ref, v_ref):
    ckvb = ckv_ref[...].astype(BF16)
    kn = _f32dot(ckvb, wk_ref[...])
    v_ref[...] = _with_ones_lane(_f32dot(ckvb, wv_ref[...])).astype(BF16)
    kr = kr_ref[...]
    for h in range(A_HEADS):
        sl = slice(h * A_HPAD, (h + 1) * A_HPAD)
        k_ref[:, sl] = (kn[:, sl] + kr).astype(BF16)


def _cache_kv_call(layer, ckv, krp, wk, wv):
    n_dec, _, past_len, _ = ckv.shape
    hw = A_HEADS * A_HPAD
    row = lambda b: (b, 0)
    cache = lambda b: (b, layer, 0, 0)
    return pl.pallas_call(
        _cache_kv_kernel,
        grid=(n_dec,),
        in_specs=[pl.BlockSpec((None, None, past_len, A_DC), cache),
                  pl.BlockSpec((None, None, past_len, LANES), cache),
                  _layer_spec(wk, layer), _layer_spec(wv, layer)],
        out_specs=[pl.BlockSpec((past_len, hw), row), pl.BlockSpec((past_len, hw), row)],
        out_shape=[jax.ShapeDtypeStruct((n_dec * past_len, hw), BF16),
                   jax.ShapeDtypeStruct((n_dec * past_len, hw), BF16)],
        compiler_params=_cparams(("parallel",)),
        name="mla_cache_kv",
    )(ckv, krp, wk, wv)


def _attn_kernel(*refs, n_seg, heads_per_group):
    q_ref = refs[0]
    kv_refs = refs[1:1 + 2 * n_seg]
    o_ref = refs[1 + 2 * n_seg]
    tiles = []
    for j in range(n_seg):
        nk = kv_refs[2 * j].shape[0]
        tk = min(nk, A_KEY_TILE)
        tiles += [(j, t0, tk) for t0 in range(0, nk, tk)]
    head = lambda h: slice(h * A_HPAD, (h + 1) * A_HPAD)

    def scores(h, tile):
        j, t0, tk = tile
        return lax.dot_general(q_ref[:, head(h)], kv_refs[2 * j][t0:t0 + tk, head(h)], NT_DIMS,
                               preferred_element_type=F32)

    def lane_max(acc, s):
        for c0 in range(0, s.shape[1], LANES):
            part = s[:, c0:c0 + LANES]
            acc = part if acc is None else jnp.maximum(acc, part)
        return acc

    groups = [list(range(g0, min(g0 + heads_per_group, A_HEADS))) for g0 in range(0, A_HEADS, heads_per_group)]
    s_cur = {h: [scores(h, t) for t in tiles] for h in groups[0]}
    for gi, grp in enumerate(groups):
        nxt = groups[gi + 1] if gi + 1 < len(groups) else []
        m = {}
        for h in grp:
            mx = None
            for s in s_cur[h]:
                mx = lane_max(mx, s)
            m[h] = jnp.max(mx, axis=1, keepdims=True)
        s_next = {h: [] for h in nxt}
        acc = {h: None for h in grp}
        for ti, tile in enumerate(tiles):
            j, t0, tk = tile
            for h in nxt:
                s_next[h].append(scores(h, tile))
            p = {h: jnp.exp((s_cur[h][ti] - m[h]).astype(BF16)) for h in grp}
            for h in grp:
                pv = _f32dot(p[h], kv_refs[2 * j + 1][t0:t0 + tk, head(h)])
                acc[h] = pv if acc[h] is None else acc[h] + pv
        for h in grp:
            o_ref[:, head(h)] = (acc[h] / acc[h][:, A_DV:A_DV + 1]).astype(BF16)
        s_cur = s_next


def _attn_ctx_call(q, k, v, n_rows):
    hw = A_HEADS * A_HPAD
    tm = ROW_BLOCK
    row = lambda i: (i, 0)
    return pl.pallas_call(
        functools.partial(_attn_kernel, n_seg=1, heads_per_group=A_HEADS),
        grid=(n_rows // tm,),
        in_specs=[pl.BlockSpec((tm, hw), row)] * 3,
        out_specs=pl.BlockSpec((tm, hw), row),
        out_shape=jax.ShapeDtypeStruct((n_rows, hw), BF16),
        compiler_params=_cparams(("parallel",)),
        name="mla_attn_ctx",
    )(q, k, v)


def _attn_dec_call(q, k, v, kc, vc, n_ctx_rows, n_dec, dec_len, past_len):
    hw = A_HEADS * A_HPAD
    tm = ROW_BLOCK
    nq = dec_len // tm
    assert n_ctx_rows % dec_len == 0
    qmap = lambda b, i: (n_ctx_rows // tm + b * nq + i, 0)
    own = lambda b, i: (n_ctx_rows // dec_len + b, 0)
    return pl.pallas_call(
        functools.partial(_attn_kernel, n_seg=2, heads_per_group=4),
        grid=(n_dec, nq),
        in_specs=[pl.BlockSpec((tm, hw), qmap),
                  pl.BlockSpec((past_len, hw), lambda b, i: (b, 0)),
                  pl.BlockSpec((past_len, hw), lambda b, i: (b, 0)),
                  pl.BlockSpec((dec_len, hw), own),
                  pl.BlockSpec((dec_len, hw), own)],
        out_specs=pl.BlockSpec((tm, hw), lambda b, i: (b * nq + i, 0)),
        out_shape=jax.ShapeDtypeStruct((n_dec * dec_len, hw), BF16),
        compiler_params=_cparams(("parallel", "arbitrary")),
        name="mla_attn_dec",
    )(q, kc, vc, k, v)


def _out_ffn_kernel(*refs, n_x, nb0, nb_ctx, alpha, blk0):
    (hf_ref, hb_ref, of_ref, ob_ref, actx_ref, adec_ref, mo_ref, gg_ref) = refs[:8]
    x_refs = refs[8:8 + n_x]
    (mod_ref, mng_ref, gng_ref, avg_ref, wm_ref, wg_ref, wa_ref, l1g_ref, l1b_ref,
     wfg_ref, wfu_ref, wfd_ref, l2g_ref, l2b_ref, o_ref) = refs[8 + n_x:]
    d = D_MODEL
    avg = avg_ref[...]
    mod = mod_ref[0]
    is_ctx = pl.program_id(0) + blk0 < nb_ctx
    x_in = _read_x(x_refs, nb0, blk0)
    tm = o_ref.shape[0]
    parts = [slice(p0, p0 + ROW_BLOCK) for p0 in range(0, tm, ROW_BLOCK)]
    f = wfg_ref.shape[1]
    tiles = -(-f // MXU_TILE)
    edges = [min(f, MXU_TILE * (tiles * c // FFN_CHUNKS)) for c in range(FFN_CHUNKS)] + [f]

    def ffn_chunk(hh, c):
        sl = slice(edges[c], edges[c + 1])
        g = _f32dot(hh, wfg_ref[:, sl])
        u = _f32dot(hh, wfu_ref[:, sl])
        return _f32dot((g * _sigmoid(g) * u).astype(BF16), wfd_ref[sl, :])

    each = lambda fn: [fn(s) for s in parts]
    a = each(lambda s: jnp.where(is_ctx, actx_ref[s, :], adec_ref[s, :]))
    hm = each(lambda s: hf_ref[s, :] + hb_ref[s, :])
    og = each(lambda s: of_ref[s, :] + ob_ref[s, :])
    hc = [x - _dot_x_c(x, avg) for x in hm]
    hvar = [_dot_x_c(x * x, avg) for x in hc]
    gms = [_dot_x_c(x * x, avg) for x in og]
    m_out = [x * lax.rsqrt(v + 1e-6) * mng_ref[...] * _sigmoid(mo_ref[s, :]) for x, v, s in zip(hc, hvar, parts)]
    gg = each(lambda s: gg_ref[s, :])
    g_out = [x * lax.rsqrt(v + 1e-6) * gng_ref[...] * (g * _sigmoid(g)) for x, v, g in zip(og, gms, gg)]
    mix = [_f32dot(m.astype(BF16), wm_ref[...]) + _f32dot(g.astype(BF16), wg_ref[...]) + _f32dot(ai, wa_ref[...])
           for m, g, ai in zip(m_out, g_out, a)]
    y = [alpha * x_in[s, :] + mod[:, 2 * d:3 * d] * mi for s, mi in zip(parts, mix)]
    x1 = jnp.concatenate([_layer_norm(yi, l1g_ref[...], l1b_ref[...]) for yi in y], axis=0)
    hh = (x1 * (1.0 + mod[:, 4 * d:5 * d]) + mod[:, 3 * d:4 * d]).astype(BF16)
    acc = ffn_chunk(hh, 0)
    for c in range(1, FFN_CHUNKS):
        acc = acc + ffn_chunk(hh, c)
    o_ref[...] = _layer_norm(alpha * x1 + mod[:, 5 * d:6 * d] * acc, l2g_ref[...], l2b_ref[...])


def _out_ffn_call(layer, hf, hb, of, ob, a_ctx, a_dec, zm, zg, x, mods, mod_map, mng, gng, avg, wm, wg, wa, l1g, l1b,
                  wfg, wfu, wfd, l2g, l2b, alpha, blk0=0, n_blk=None):
    tm = WIDE_ROWS
    x_specs, x_ops = _x_specs(x, tm, blk0)
    nt = sum(a.shape[0] for a in x_ops)
    n_blk = nt // tm if n_blk is None else n_blk
    d = D_MODEL
    hw = M_HEADS * M_DV
    ahw = A_HEADS * A_HPAD
    nb_ctx = a_ctx.shape[0] // tm
    row = lambda i: (i + blk0, 0)
    gate_col = lambda i: (i + blk0, 2)
    resident = dict(pipeline_mode=pl.Buffered(1))
    return pl.pallas_call(
        functools.partial(_out_ffn_kernel, n_x=len(x_ops), nb0=x_ops[0].shape[0] // tm, nb_ctx=nb_ctx,
                          alpha=alpha, blk0=blk0),
        grid=(n_blk,),
        in_specs=[pl.BlockSpec((tm, hw), row)] * 4
                 + [pl.BlockSpec((tm, ahw), lambda i: (jnp.minimum(i + blk0, nb_ctx - 1), 0)),
                    pl.BlockSpec((tm, ahw), lambda i: (jnp.maximum(i + blk0 - nb_ctx, 0), 0)),
                    pl.BlockSpec((tm, hw), gate_col), pl.BlockSpec((tm, hw), gate_col)]
                 + x_specs
                 + [pl.BlockSpec((1, 1, 6 * d), lambda i: (mod_map((i + blk0) * (tm // ROW_BLOCK)), 0, 0)),
                    _layer_spec(mng, layer), _layer_spec(gng, layer),
                    pl.BlockSpec((hw, hw), lambda i: (0, 0)),
                    _layer_spec(wm, layer, **resident), _layer_spec(wg, layer, **resident),
                    _layer_spec(wa, layer, **resident),
                    _layer_spec(l1g, layer), _layer_spec(l1b, layer),
                    _layer_spec(wfg, layer, **resident), _layer_spec(wfu, layer, **resident),
                    _layer_spec(wfd, layer, **resident),
                    _layer_spec(l2g, layer), _layer_spec(l2b, layer)],
        out_specs=pl.BlockSpec((tm, d), lambda i: (i, 0)),
        out_shape=jax.ShapeDtypeStruct((n_blk * tm, d), F32),
        compiler_params=_cparams(("parallel",)),
        name="out_proj_ffn",
    )(hf, hb, of, ob, a_ctx, a_dec, zm, zg, *x_ops, mods, mng, gng, avg, wm, wg, wa, l1g, l1b,
      wfg, wfu, wfd, l2g, l2b)


def _block_diag_heads(t):
    h = t.shape[-3]
    eye = jnp.eye(h, dtype=t.dtype)
    out = jnp.einsum('...hkv,hg->...hkgv', t, eye)
    return out.reshape(t.shape[:-3] + (h * t.shape[-2], h * t.shape[-1]))


def _rope_tables(dec_len):
    half = A_DROPE // 4
    inv = ROPE_BASE ** (-jnp.arange(half, dtype=F32) / half)
    pos = jnp.arange(dec_len, dtype=jnp.int32)
    rows = (pos // GRID_W).astype(F32)[:, None] * inv[None, :]
    cols = (pos % GRID_W).astype(F32)[:, None] * inv[None, :]
    cr, sr, cc, sc = jnp.cos(rows), jnp.sin(rows), jnp.cos(cols), jnp.sin(cols)
    cos32 = jnp.concatenate([cr, cr, cc, cc], axis=1)
    sin32 = jnp.concatenate([-sr, sr, -sc, sc], axis=1)
    ones_lo = jnp.ones((dec_len, KR_LO), F32)
    ones_hi = jnp.ones((dec_len, A_HPAD - KR_HI), F32)
    cos = jnp.concatenate([ones_lo, cos32, ones_hi], axis=1)
    sin = jnp.concatenate([0 * ones_lo, sin32, 0 * ones_hi], axis=1)
    kcos = jnp.concatenate([0 * ones_lo, cos32, 0 * ones_hi], axis=1)
    lane = jnp.arange(A_HPAD)
    rope_lane = ((lane >= KR_LO) & (lane < KR_HI)).astype(F32)[None, :]
    ctx_cos = jnp.ones((WIDE_ROWS, A_HPAD), F32)
    ctx_sin = jnp.zeros((WIDE_ROWS, A_HPAD), F32)
    ctx_kcos = jnp.broadcast_to(rope_lane, (WIDE_ROWS, A_HPAD))
    return (jnp.concatenate([ctx_cos, cos]), jnp.concatenate([ctx_sin, sin]), jnp.concatenate([ctx_kcos, kcos]))


def _rope_partner(w):
    e = A_DROPE // 4
    return jnp.concatenate([w[..., e:2 * e], w[..., 0:e], w[..., 3 * e:4 * e], w[..., 2 * e:3 * e]], axis=-1)


def kernel(x_prompt, x_sample, state_mlstm_C, state_mlstm_n, state_mlstm_m, state_gla_S, cache_mla_ckv, cache_mla_krope, c, c_ctx, w_ada, b_ada, w_in, m_gate_b, m_norm_g, g_w2, g_b2, g_norm_g, a_q_norm_g, a_kv_norm_g, a_w_uq, a_w_ukv, w_out, ln1_g, ln1_b, w_ffn_gate, w_ffn_up, w_ffn_down, ln2_g, ln2_b):
    n_ctx, ctx_len, d = x_prompt.shape
    n_dec, dec_len, _ = x_sample.shape
    depth = w_in.shape[0]
    past_len = cache_mla_ckv.shape[2]
    assert d == D_MODEL and ctx_len == ROW_BLOCK and dec_len % ROW_BLOCK == 0 and past_len % 8 == 0
    assert (n_ctx * ctx_len) % WIDE_ROWS == 0 and dec_len % WIDE_ROWS == 0
    assert n_dec + 1 <= 8
    n_ctx_rows = n_ctx * ctx_len
    n_dec_rows = n_dec * dec_len
    nt = n_ctx_rows + n_dec_rows
    nb_ctx = n_ctx_rows // ROW_BLOCK
    nb_seq = dec_len // ROW_BLOCK
    alpha = (2.0 * depth) ** 0.25

    blocks = np.arange(nt // ROW_BLOCK)
    dec_b = np.maximum(blocks - nb_ctx, 0)
    is_ctx = blocks < nb_ctx
    plan = {
        "first": jnp.asarray(np.where(is_ctx, 2, (dec_b % nb_seq) == 0).astype(np.int32)),
        "seq": jnp.asarray(np.where(is_ctx, blocks, n_ctx + dec_b // nb_seq).astype(np.int32)),
        "bwd": jnp.asarray(np.where(is_ctx, blocks,
                                    nb_ctx + (dec_b // nb_seq) * nb_seq + nb_seq - 1 - dec_b % nb_seq).astype(np.int32)),
    }

    def mod_map_for(layer):
        def mod_map(i):
            return layer * 8 + jnp.where(i < nb_ctx, 0, 1 + (i - nb_ctx) // nb_seq)
        return mod_map

    cond8 = jnp.zeros((8, d), F32).at[0].set(c_ctx).at[1:1 + n_dec].set(c)
    mods = _ada_call(cond8, w_ada, b_ada).reshape(depth * 8, 1, 6 * d)

    pts = [int(p) for p in np.cumsum(IN_SPLITS)[:-1]]
    (wmq, wmk, wmv, wmo, wmg, wgq, wgk, wgv, wgg, wga, wacq, wackv, wakr) = jnp.split(w_in.astype(BF16), pts, axis=-1)
    zpad = lambda n: jnp.zeros((depth, d, n), BF16)
    s1 = jnp.concatenate([wmg, wga, zpad(KR_LO - 48), wakr, zpad(LANES - KR_HI)], axis=-1)
    s2 = jnp.concatenate([zpad(KR_LO), _rope_partner(wakr), zpad(LANES - KR_HI)], axis=-1)
    w_in_p = jnp.concatenate([wmq, wmk, wmv, wmo, wgq, wgk, wgv, wgg, wacq, wackv, s1, s2], axis=-1)
    w_gt = jnp.swapaxes(wmg, 1, 2)
    gate_b = jnp.zeros((depth, 1, LANES), F32).at[:, 0, 0:16].set(m_gate_b)
    gate_bt = m_gate_b.reshape(depth, 16, 1)
    w2p = jnp.zeros((depth, 2, LANES, G_HEADS * G_DK), F32)
    w2p = w2p.at[:, 0, 16:16 + G_RANK].set(g_w2[:, 0]).at[:, 1, 16 + G_RANK:16 + 2 * G_RANK].set(g_w2[:, 1]).astype(BF16)
    b2p = g_b2.reshape(depth, 2, 1, G_HEADS * G_DK)
    uq = a_w_uq.reshape(depth, A_DQ, A_HEADS, A_DNOPE + A_DROPE)
    hp = lambda n: jnp.zeros((depth, uq.shape[1], A_HEADS, n), F32)
    wuq = jnp.concatenate([uq, hp(A_HPAD - KR_HI)], axis=-1).reshape(depth, A_DQ, -1).astype(BF16)
    wuqs = jnp.concatenate([hp(KR_LO), _rope_partner(uq[..., A_DNOPE:]), hp(A_HPAD - KR_HI)],
                           axis=-1).reshape(depth, A_DQ, -1).astype(BF16)
    ukv = a_w_ukv.reshape(depth, A_DC, A_HEADS, A_DNOPE + A_DV)
    kp = jnp.zeros((depth, A_DC, A_HEADS, A_HPAD - A_DNOPE), F32)
    wk = jnp.concatenate([ukv[..., :A_DNOPE], kp], axis=-1).reshape(depth, A_DC, -1).astype(BF16)
    wv = jnp.concatenate([ukv[..., A_DNOPE:], kp], axis=-1).reshape(depth, A_DC, -1).astype(BF16)
    mw, gw = M_HEADS * M_DV, G_HEADS * G_DV
    wo_m = w_out[:, 0:mw].astype(BF16)
    wo_g = w_out[:, mw:mw + gw].astype(BF16)
    wo_a = w_out[:, mw + gw:].reshape(depth, A_HEADS, A_DV, d)
    wo_a = jnp.concatenate([wo_a, jnp.zeros((depth, A_HEADS, A_HPAD - A_DV, d), F32)], axis=2)
    wo_a = wo_a.reshape(depth, A_HEADS * A_HPAD, d).astype(BF16)
    wfg, wfu, wfd = w_ffn_gate.astype(BF16), w_ffn_up.astype(BF16), w_ffn_down.astype(BF16)

    sconst = _scan_consts()
    head_lanes = jnp.asarray((np.arange(M_HEADS)[:, None] == (np.arange(S_W) // M_DV)[None, :]).astype(np.float32))
    hi = np.arange(mw) // M_DV
    avg = jnp.asarray((hi[:, None] == hi[None, :]).astype(np.float32) / M_DV, BF16)
    cos, sin, kcos = _rope_tables(dec_len)
    nbw_ctx = n_ctx_rows // WIDE_ROWS
    nbw_seq = dec_len // WIDE_ROWS

    def rope_map(i):
        return jnp.where(i < nbw_ctx, 0, 1 + (i - nbw_ctx) % nbw_seq)

    krp_all = jnp.zeros(cache_mla_krope.shape[:-1] + (LANES,), F32).at[..., KR_LO:KR_HI].set(cache_mla_krope)
    n_bd = jnp.einsum('bldhk,hg->bldhkg', state_mlstm_n, head_lanes)
    cn0 = jnp.concatenate([_block_diag_heads(state_mlstm_C),
                           n_bd.reshape(n_dec, depth, 2, M_HEADS * M_DK, S_W)], axis=-1)
    m0 = jnp.repeat(state_mlstm_m, M_DV, axis=-1)[:, :, :, None, :]
    s0 = _block_diag_heads(state_gla_S)
    vec = lambda p: p[:, None, :]

    x = (x_prompt.reshape(n_ctx_rows, d), x_sample.reshape(n_dec_rows, d))
    new_c, new_n, new_m, new_s, new_ckv, new_kr = [], [], [], [], [], []
    for l in range(depth):
        mod_map = mod_map_for(l)
        zm, zg, zs, zt, q, k, v, ckv, kr = _in_mla_call(
            l, x, mods, mod_map, rope_map, w_in_p, w_gt, cos, sin, kcos,
            vec(a_q_norm_g), vec(a_kv_norm_g), wuq, wuqs, wk, wv)

        (hf, hb, c_fin, n_fin, m_fin), (of, ob, s_fin) = _scan_call(
            plan,
            [_mlstm_part(l, n_ctx, zm, zs, zt, cn0, m0, gate_b, gate_bt, sconst),
             _gla_part(l, n_ctx, zg, zs, s0, w2p, b2p, sconst)],
            "mixer_scan")

        kc, vc = _cache_kv_call(l, cache_mla_ckv, krp_all, wk, wv)
        a_ctx = _attn_ctx_call(q, k, v, n_ctx_rows)
        a_dec = _attn_dec_call(q, k, v, kc, vc, n_ctx_rows, n_dec, dec_len, past_len)

        out_ffn = functools.partial(
            _out_ffn_call, l, hf, hb, of, ob, a_ctx, a_dec, zm, zg, x, mods, mod_map,
            vec(m_norm_g), vec(g_norm_g), avg, wo_m, wo_g, wo_a, vec(ln1_g), vec(ln1_b),
            wfg, wfu, wfd, vec(ln2_g), vec(ln2_b), alpha)
        if l + 1 < depth:
            x = out_ffn()
        else:
            y_prompt = out_ffn(blk0=0, n_blk=nbw_ctx).reshape(n_ctx, ctx_len, d)
            y_sample = out_ffn(blk0=nbw_ctx, n_blk=n_dec * nbw_seq).reshape(n_dec, dec_len, d)

        new_c.append(c_fin[:n_ctx])
        new_n.append(n_fin[:n_ctx, ..., 0])
        new_m.append(m_fin[:n_ctx, :, 0, ::M_DV])
        new_s.append(s_fin[:n_ctx])
        new_ckv.append(ckv[:n_ctx_rows].reshape(n_ctx, ctx_len, A_DC))
        new_kr.append(kr[:n_ctx_rows].reshape(n_ctx, ctx_len, A_DROPE))

    st = lambda xs: jnp.stack(xs, axis=1)
    return (y_prompt, y_sample, st(new_c), st(new_n), st(new_m), st(new_s), st(new_ckv), st(new_kr))
```

```python
import functools

import numpy as np
import jax
import jax.numpy as jnp
from jax import lax
from jax.experimental import pallas as pl
from jax.experimental.pallas import tpu as pltpu

F32 = jnp.float32
BF16 = jnp.bfloat16

D_MODEL = 1024
GRID_W = 64
M_HEADS, M_DK, M_DV = 4, 32, 64
G_HEADS, G_DK, G_DV = 4, 32, 64
G_RANK = 16
G_TAU = 16.0
A_HEADS, A_DNOPE, A_DROPE, A_DV = 8, 64, 32, 64
A_DQ, A_DC = 256, 128
A_SCALE = (A_DNOPE + A_DROPE) ** -0.5
ROPE_BASE = 10000.0
IN_SPLITS = (M_HEADS * M_DK, M_HEADS * M_DK, M_HEADS * M_DV, M_HEADS * M_DV, 4 * M_HEADS,
             G_HEADS * G_DK, G_HEADS * G_DK, G_HEADS * G_DV, G_HEADS * G_DV, 2 * G_RANK,
             A_DQ, A_DC, A_DROPE)

LANES = 128
MXU_TILE = 256
VMEM_LIMIT = 56 * 1024 * 1024
ROW_BLOCK = 256
WIDE_ROWS = 512
S_CHUNK = 64
S_W = M_HEADS * M_DV
G_LEVELS = 6
G_SMALL_LEVELS = 3
A_HPAD = 128
A_KEY_TILE = 512
FFN_CHUNKS = 2

ZM_W = 768
ZG_W = 768
ZQ_W = 384
ZS_W = 256
Z_W = ZM_W + ZG_W + ZQ_W + ZS_W
QKV_W = 2 * M_HEADS * M_DK + M_HEADS * M_DV
GATE_ROWS = 4 * M_HEADS
GA_LO = GATE_ROWS
COND_ROWS = 8
ADA_COLS = 1536
KR_LO, KR_HI = A_DNOPE, A_DNOPE + A_DROPE

assert (M_HEADS, M_DK, M_DV) == (G_HEADS, G_DK, G_DV) and S_CHUNK == M_DV and M_HEADS * M_DK == LANES


def _cparams(sem):
    return pltpu.CompilerParams(dimension_semantics=sem, vmem_limit_bytes=VMEM_LIMIT)


def _f32dot(a, b):
    return jnp.dot(a, b, preferred_element_type=F32)


NT_DIMS = (((1,), (1,)), ((), ()))
TN_DIMS = (((0,), (0,)), ((), ()))


def _split(x):
    hi = x.astype(BF16)
    lo = (x - hi.astype(F32)).astype(BF16)
    return hi, lo


def _dot_c_x(c, x):
    hi, lo = _split(x)
    return _f32dot(c, hi) + _f32dot(c, lo)


def _dot_x_c(x, c):
    hi, lo = _split(x)
    return _f32dot(hi, c) + _f32dot(lo, c)


def _log_sigmoid(x):
    return jnp.minimum(x, 0.0) - jnp.log(1.0 + jnp.exp(-jnp.abs(x)))


def _sigmoid(x):
    return 1.0 / (1.0 + jnp.exp(-x))


def _rms(x, g, eps=1e-6):
    return x * lax.rsqrt(jnp.mean(x * x, axis=-1, keepdims=True) + eps) * g


def _layer_norm(y, g, b, eps=1e-5):
    yc = y - jnp.mean(y, axis=-1, keepdims=True)
    return yc * lax.rsqrt(jnp.mean(yc * yc, axis=-1, keepdims=True) + eps) * g + b


def _with_ones_lane(v):
    lane = lax.broadcasted_iota(jnp.int32, v.shape, 1) % A_HPAD
    return jnp.where(lane == A_DV, 1.0, v)


def _x_specs(x, tm, blk0=0):
    if not isinstance(x, tuple):
        return [pl.BlockSpec((tm, x.shape[1]), lambda i: (i + blk0, 0))], [x]
    nb0 = x[0].shape[0] // tm
    return ([pl.BlockSpec((tm, x[0].shape[1]), lambda i: (jnp.minimum(i + blk0, nb0 - 1), 0)),
             pl.BlockSpec((tm, x[1].shape[1]), lambda i: (jnp.maximum(i + blk0 - nb0, 0), 0))], list(x))


def _read_x(refs, nb0, blk0=0):
    if len(refs) == 1:
        return refs[0][...]
    return jnp.where(pl.program_id(0) + blk0 < nb0, refs[0][...], refs[1][...])


def _ada_kernel(c_ref, w_ref, b_ref, o_ref):
    c = c_ref[...]
    s = c * _sigmoid(c)
    o_ref[0] = _f32dot(s.astype(BF16), w_ref[0].astype(BF16)) + b_ref[0]


def _ada_call(cond8, w_ada, b_ada):
    depth, d, n = w_ada.shape
    tn = ADA_COLS
    return pl.pallas_call(
        _ada_kernel,
        grid=(depth, n // tn),
        in_specs=[pl.BlockSpec((COND_ROWS, d), lambda l, j: (0, 0)),
                  pl.BlockSpec((1, d, tn), lambda l, j: (l, 0, j)),
                  pl.BlockSpec((1, 1, tn), lambda l, j: (l, 0, j))],
        out_specs=pl.BlockSpec((1, COND_ROWS, tn), lambda l, j: (l, 0, j)),
        out_shape=jax.ShapeDtypeStruct((depth, COND_ROWS, n), F32),
        compiler_params=_cparams(("arbitrary", "arbitrary")),
        name="ada_mod",
    )(cond8, w_ada, b_ada.reshape(depth, 1, n))


def _in_mla_kernel(*refs, n_x, nb0):
    x_refs = refs[:n_x]
    (mod_ref, w_ref, wgt_ref, cos_ref, sin_ref, kcos_ref, gq_ref, gkv_ref, wuq_ref, wuqs_ref, wk_ref, wv_ref,
     zm_ref, zg_ref, zs_ref, zt_ref, q_ref, k_ref, v_ref, ckv_ref, kr_ref) = refs[n_x:]
    d = D_MODEL
    mod = mod_ref[0]
    h = (_read_x(x_refs, nb0) * (1.0 + mod[:, d:2 * d]) + mod[:, 0:d]).astype(BF16)
    z = _f32dot(h, w_ref[...])
    zm_ref[...] = z[:, 0:ZM_W]
    zg_ref[...] = z[:, ZM_W:ZM_W + ZG_W]
    zs = z[:, ZM_W + ZG_W + ZQ_W:Z_W]
    zs_ref[...] = zs[:, 0:LANES]
    kr_ref[...] = zs[:, KR_LO:KR_HI]
    zt_ref[...] = lax.dot_general(wgt_ref[...], h, NT_DIMS, preferred_element_type=F32)
    zq = z[:, ZM_W + ZG_W:ZM_W + ZG_W + ZQ_W]
    cq = _rms(zq[:, 0:A_DQ], gq_ref[...]).astype(BF16)
    qn = _f32dot(cq, wuq_ref[...])
    qs = _f32dot(cq, wuqs_ref[...])
    cos = cos_ref[...]
    sin = sin_ref[...]
    for hd in range(A_HEADS):
        sl = slice(hd * A_HPAD, (hd + 1) * A_HPAD)
        q_ref[:, sl] = ((qn[:, sl] * cos + qs[:, sl] * sin) * A_SCALE).astype(BF16)
    ckv = _rms(zq[:, A_DQ:A_DQ + A_DC], gkv_ref[...])
    ckv_ref[...] = ckv
    ckvb = ckv.astype(BF16)
    kn = _f32dot(ckvb, wk_ref[...])
    v_ref[...] = _with_ones_lane(_f32dot(ckvb, wv_ref[...])).astype(BF16)
    kr = zs[:, 0:LANES] * kcos_ref[...] + zs[:, LANES:2 * LANES] * sin
    for hd in range(A_HEADS):
        sl = slice(hd * A_HPAD, (hd + 1) * A_HPAD)
        k_ref[:, sl] = (kn[:, sl] + kr).astype(BF16)


def _layer_spec(arr, layer, **kw):
    zeros = (0,) * (arr.ndim - 1)
    return pl.BlockSpec((None,) + arr.shape[1:], lambda *_: (layer,) + zeros, **kw)


def _in_mla_call(layer, x, mods, mod_map, rope_map, w_in_p, w_gt, cos, sin, kcos, gq, gkv, wuq, wuqs, wk, wv):
    tm = WIDE_ROWS
    x_specs, x_ops = _x_specs(x, tm)
    nt = sum(a.shape[0] for a in x_ops)
    d = D_MODEL
    hw = A_HEADS * A_HPAD
    row = lambda i: (i, 0)
    rope = lambda i: (rope_map(i), 0)
    resident = dict(pipeline_mode=pl.Buffered(1))
    return pl.pallas_call(
        functools.partial(_in_mla_kernel, n_x=len(x_ops), nb0=x_ops[0].shape[0] // tm),
        grid=(nt // tm,),
        in_specs=x_specs
                 + [pl.BlockSpec((1, 1, 6 * d), lambda i: (mod_map(i * (tm // ROW_BLOCK)), 0, 0)),
                    _layer_spec(w_in_p, layer, **resident), _layer_spec(w_gt, layer),
                    pl.BlockSpec((tm, LANES), rope), pl.BlockSpec((tm, LANES), rope), pl.BlockSpec((tm, LANES), rope),
                    _layer_spec(gq, layer), _layer_spec(gkv, layer),
                    _layer_spec(wuq, layer, **resident), _layer_spec(wuqs, layer, **resident),
                    _layer_spec(wk, layer, **resident), _layer_spec(wv, layer, **resident)],
        out_specs=[pl.BlockSpec((tm, ZM_W), row), pl.BlockSpec((tm, ZG_W), row), pl.BlockSpec((tm, LANES), row),
                   pl.BlockSpec((GATE_ROWS, tm), lambda i: (0, i)),
                   pl.BlockSpec((tm, hw), row), pl.BlockSpec((tm, hw), row), pl.BlockSpec((tm, hw), row),
                   pl.BlockSpec((tm, A_DC), row), pl.BlockSpec((tm, A_DROPE), row)],
        out_shape=[jax.ShapeDtypeStruct((nt, ZM_W), F32), jax.ShapeDtypeStruct((nt, ZG_W), F32),
                   jax.ShapeDtypeStruct((nt, LANES), F32), jax.ShapeDtypeStruct((GATE_ROWS, nt), F32),
                   jax.ShapeDtypeStruct((nt, hw), BF16), jax.ShapeDtypeStruct((nt, hw), BF16),
                   jax.ShapeDtypeStruct((nt, hw), BF16), jax.ShapeDtypeStruct((nt, A_DC), F32),
                   jax.ShapeDtypeStruct((nt, A_DROPE), F32)],
        compiler_params=_cparams(("parallel",)),
        name="in_proj_mla",
    )(*x_ops, mods, w_in_p, w_gt, cos, sin, kcos, gq, gkv, wuq, wuqs, wk, wv)


def _scan_consts():
    n = S_CHUNK
    t = np.arange(n)[:, None]
    s = np.arange(n)[None, :]
    tri = [(s <= t), (s >= t)]
    head_of = np.arange(S_W) // M_DV
    khead = np.arange(M_HEADS * M_DK) // M_DK
    kmask = (head_of[:, None] == khead[None, :]).astype(np.float32)
    vmask = (head_of[:, None] == head_of[None, :]).astype(np.float32)
    bd = (khead[:, None] == head_of[None, :]).astype(np.float32)
    sel, tritile, neg = [], [], []
    for d in (0, 1):
        io, fo = 2 * M_HEADS * d, 2 * M_HEADS * d + M_HEADS
        m = np.zeros((LANES, 3 * S_W), np.float32)
        for h in range(M_HEADS):
            m[fo + h, h * M_DV:(h + 1) * M_DV] = 1.0
            m[io + h, S_W + h * M_DV:S_W + (h + 1) * M_DV] = 1.0
            m[fo + h, 2 * S_W + h * M_DK:2 * S_W + (h + 1) * M_DK] = 1.0
            m[io + h, 2 * S_W + LANES + h * M_DK:2 * S_W + LANES + (h + 1) * M_DK] = 1.0
        sel.append(m)
        tritile.append(np.tile(tri[d].T.astype(np.float32), (1, M_HEADS)))
        neg.append(np.tile(np.where(tri[d], 0.0, -np.inf).astype(np.float32), (1, M_HEADS)))
    masks = [np.eye(n)]
    for lev in range(G_LEVELS):
        b = 1 << lev
        right = (t % (2 * b)) >= b
        same_parent = (t // (2 * b)) == (s // (2 * b))
        masks.append((same_parent & right & ((s % (2 * b)) < b)).astype(np.float32))
    flip = lambda m: m[::-1, ::-1]
    mcat = np.stack([np.stack([np.tile(m, (1, G_HEADS)) for m in masks]),
                     np.stack([np.tile(flip(m), (1, G_HEADS)) for m in masks])])
    gath = []
    for d in (0, 1):
        g = np.zeros((G_SMALL_LEVELS * n, n), np.float32)
        for lev in range(G_SMALL_LEVELS):
            for row in range(n):
                g[lev * n + row, _gla_boundary(row, 1 << lev, d)] = 1.0
        gath.append(g)
    f32 = lambda a: jnp.asarray(a, F32)
    b16 = lambda a: jnp.asarray(a, BF16)
    return dict(tri=b16(np.stack(tri).astype(np.float32)), sel=b16(np.stack(sel)), tritile=b16(np.stack(tritile)),
                neg=f32(np.stack(neg)), kmask=b16(kmask), vmask=b16(vmask), bd=f32(bd),
                bd2=f32(np.concatenate([bd, bd], axis=1)), mcat=f32(mcat), gath=b16(np.stack(gath)))


def _gla_boundary(row, b, d):
    pstart = (row // (2 * b)) * (2 * b)
    return pstart + b - 1 if d == 0 else pstart + b


def _scan_units():
    nch = ROW_BLOCK // S_CHUNK
    return [(d, ci if d == 0 else nch - 1 - ci) for ci in range(nch) for d in (0, 1)]


def _chunk_rows(c):
    return slice(c * S_CHUNK, (c + 1) * S_CHUNK)


def _mlstm_body(first_ref, seq_ref, bwd_ref,
                zmf_ref, zsf_ref, ztf_ref, zmb_ref, zsb_ref, ztb_ref,
                cn0_ref, m0_ref, gb_ref, gbt_ref,
                tri_ref, sel_ref, tritile_ref, neg_ref, kmask_ref, vmask_ref, bd2_ref,
                hf_ref, hb_ref, cout_ref, nout_ref, mout_ref,
                cns_ref, ms_ref):
    r = pl.program_id(0)
    n = S_CHUNK
    qk_w = M_HEADS * M_DK
    kmask = kmask_ref[...]
    vmask = vmask_ref[...]
    bd2 = bd2_ref[...]
    units = _scan_units()
    rows = _chunk_rows
    zm_refs, zs_refs, zt_refs, h_refs = (zmf_ref, zmb_ref), (zsf_ref, zsb_ref), (ztf_ref, ztb_ref), (hf_ref, hb_ref)
    fo = lambda d: 2 * M_HEADS * d + M_HEADS
    last = lambda d: n - 1 if d == 0 else 0
    lane = lax.broadcasted_iota(jnp.int32, (n, LANES), 1)
    head_row = lax.broadcasted_iota(jnp.int32, (1, S_W), 1) // M_DV

    def per_unit(f):
        return [f(d, c) for d, c in units]

    def dot_split(c_left, x_split, c_right):
        hi, lo = x_split
        if c_left is not None:
            return _f32dot(c_left, hi) + _f32dot(c_left, lo)
        return _f32dot(hi, c_right) + _f32dot(lo, c_right)

    q = per_unit(lambda d, c: zm_refs[d][rows(c), 0:qk_w].astype(BF16))
    k = per_unit(lambda d, c: zm_refs[d][rows(c), qk_w:2 * qk_w] * (M_DK ** -0.5))
    v = per_unit(lambda d, c: zm_refs[d][rows(c), 2 * qk_w:2 * qk_w + S_W])
    graw = per_unit(lambda d, c: zs_refs[d][rows(c), :] + gb_ref[...])
    graw_t = per_unit(lambda d, c: zt_refs[d][:, rows(c)] + gbt_ref[...])
    lf = [_split(_log_sigmoid(g)) for g in graw]
    lf_t = [_split(_log_sigmoid(g)) for g in graw_t]
    yield
    bcol = [dot_split(tri_ref[d], s, None) for (d, _), s in zip(units, lf)]
    xs = [_split(jnp.where((lane >= fo(d)) & (lane < fo(d) + M_HEADS), b, g))
          for (d, _), b, g in zip(units, bcol, graw)]
    yield
    ex = [dot_split(None, s, sel_ref[d]) for (d, _), s in zip(units, xs)]
    yield
    b256 = [e[:, 0:S_W] for e in ex]
    i256 = [e[:, S_W:2 * S_W] for e in ex]
    b128 = [e[:, 2 * S_W:2 * S_W + LANES] for e in ex]
    i128 = [e[:, 2 * S_W + LANES:] for e in ex]
    rows_t = [dot_split(None, s, tritile_ref[d]) for (d, _), s in zip(units, lf_t)]
    yield
    b_row = []
    for (d, _), rt in zip(units, rows_t):
        br = jnp.zeros((1, S_W), F32)
        for h in range(M_HEADS):
            br = jnp.where(head_row == h, rt[fo(d) + h:fo(d) + h + 1, :], br)
        b_row.append(br)
    decay = [jnp.exp(b - br + neg_ref[d]) for (d, _), b, br in zip(units, b256, b_row)]
    yield
    k_bd = [jnp.concatenate([ki.astype(BF16)] * M_HEADS, axis=0) * kmask for ki in k]
    qk = [lax.dot_general(qi, ki, NT_DIMS, preferred_element_type=F32) for qi, ki in zip(q, k_bd)]
    yield
    p = [(a * b).astype(BF16) for a, b in zip(qk, decay)]
    i_max = [jnp.max(i, axis=0, keepdims=True) for i in i256]
    scale = [jnp.exp(i - im) for i, im in zip(i256, i_max)]
    yield
    ve = [jnp.concatenate([jnp.concatenate([(vi * sc).astype(BF16)] * M_HEADS, axis=0) * vmask,
                           jnp.concatenate([sc.astype(BF16)] * M_HEADS, axis=0) * vmask], axis=1)
          for vi, sc in zip(v, scale)]
    yield
    nd = [_f32dot(pi, vei) for pi, vei in zip(p, ve)]
    yield
    bl256 = [b[last(d):last(d) + 1, :] for (d, _), b in zip(units, b256)]
    g_max = [jnp.max(bl - b + i, axis=0, keepdims=True) for bl, b, i in zip(bl256, b256, i256)]
    g128 = [b[last(d):last(d) + 1, :] - b + i for (d, _), b, i in zip(units, b128, i128)]
    w = [jnp.exp(g - jnp.max(g, axis=0, keepdims=True)) for g in g128]
    kw = [(ki * wi).astype(BF16) for ki, wi in zip(k, w)]
    yield
    ones = jnp.ones((n, S_W), BF16)
    upd = [lax.dot_general(kwi, jnp.concatenate([vi.astype(BF16), ones], axis=1), TN_DIMS,
                           preferred_element_type=F32) for kwi, vi in zip(kw, v)]
    fresh = first_ref[r] != 0
    zero = first_ref[r] == 2
    state = [(jnp.where(fresh, jnp.where(zero, 0.0, cn0_ref[0, d]), cns_ref[d]),
              jnp.where(fresh, jnp.where(zero, 0.0, m0_ref[0, d]), ms_ref[d])) for d in (0, 1)]
    outs = []
    for i, (d, c) in enumerate(units):
        cn, m = state[d]
        qcn = _f32dot(q[i], cn.astype(BF16))
        a = b256[i] + m
        sig = jnp.maximum(i_max[i], a)
        f1 = jnp.exp(i_max[i] - sig)
        f2 = jnp.exp(a - sig)
        num = f1 * nd[i][:, 0:S_W] + f2 * qcn[:, 0:S_W]
        den = f1 * nd[i][:, S_W:] + f2 * qcn[:, S_W:]
        outs.append(num / jnp.maximum(jnp.abs(den), jnp.exp(-sig)))
        m_new = jnp.maximum(bl256[i] + m, g_max[i])
        alpha = jnp.exp(bl256[i] + m - m_new)
        beta = jnp.exp(g_max[i] - m_new)
        state[d] = (jnp.concatenate([alpha, alpha], axis=1) * cn
                    + (jnp.concatenate([beta, beta], axis=1) * bd2) * upd[i], m_new)
        yield
    for (d, c), o in zip(units, outs):
        h_refs[d][rows(c), :] = o
    for d in (0, 1):
        cn, m = state[d]
        cns_ref[d], ms_ref[d] = cn, m
        mout_ref[0, d] = m
        for h in range(M_HEADS):
            blk = cn[h * M_DK:(h + 1) * M_DK, :]
            cout_ref[0, d, h] = blk[:, h * M_DV:(h + 1) * M_DV]
            nout_ref[0, d, h] = blk[:, S_W + h * M_DV:S_W + (h + 1) * M_DV]


def _mlstm_part(layer, n_ctx, zm, zs, zt, cn0, m0, gb, gbt, sc):
    nt = zm.shape[0]
    kd = M_HEADS * M_DK
    n = S_CHUNK
    fwd = lambda r, first, seq, bwd: (r, 0)
    bwd_ = lambda r, first, seq, bwd: (bwd[r], 0)
    fwd_t = lambda r, first, seq, bwd: (0, r)
    bwd_t = lambda r, first, seq, bwd: (0, bwd[r])
    seq5 = lambda r, first, seq, bwd: (jnp.maximum(seq[r] - n_ctx, 0), layer, 0, 0, 0)
    out4 = lambda r, first, seq, bwd: (jnp.minimum(seq[r], n_ctx), 0, 0, 0)
    out5 = lambda r, first, seq, bwd: (jnp.minimum(seq[r], n_ctx), 0, 0, 0, 0)
    const = lambda r, first, seq, bwd: (0, 0)
    const3 = lambda r, first, seq, bwd: (0, 0, 0)
    return dict(
        body=_mlstm_body,
        in_specs=[pl.BlockSpec((ROW_BLOCK, QKV_W), fwd), pl.BlockSpec((ROW_BLOCK, LANES), fwd),
                  pl.BlockSpec((GATE_ROWS, ROW_BLOCK), fwd_t),
                  pl.BlockSpec((ROW_BLOCK, QKV_W), bwd_), pl.BlockSpec((ROW_BLOCK, LANES), bwd_),
                  pl.BlockSpec((GATE_ROWS, ROW_BLOCK), bwd_t),
                  pl.BlockSpec((1, None, 2, kd, 2 * S_W), seq5), pl.BlockSpec((1, None, 2, 1, S_W), seq5),
                  _layer_spec(gb, layer), _layer_spec(gbt, layer),
                  pl.BlockSpec((2, n, n), const3), pl.BlockSpec((2, LANES, 3 * S_W), const3),
                  pl.BlockSpec((2, n, S_W), const3), pl.BlockSpec((2, n, S_W), const3),
                  pl.BlockSpec((S_W, kd), const), pl.BlockSpec((S_W, S_W), const),
                  pl.BlockSpec((kd, 2 * S_W), const)],
        out_specs=[pl.BlockSpec((ROW_BLOCK, S_W), fwd), pl.BlockSpec((ROW_BLOCK, S_W), bwd_),
                   pl.BlockSpec((1, 2, M_HEADS, M_DK, M_DV), out5), pl.BlockSpec((1, 2, M_HEADS, M_DK, M_DV), out5),
                   pl.BlockSpec((1, 2, 1, S_W), out4)],
        scratch_shapes=[pltpu.VMEM((2, kd, 2 * S_W), F32), pltpu.VMEM((2, 1, S_W), F32)],
        out_shape=[jax.ShapeDtypeStruct((nt, S_W), F32), jax.ShapeDtypeStruct((nt, S_W), F32),
                   jax.ShapeDtypeStruct((n_ctx + 1, 2, M_HEADS, M_DK, M_DV), F32),
                   jax.ShapeDtypeStruct((n_ctx + 1, 2, M_HEADS, M_DK, M_DV), F32),
                   jax.ShapeDtypeStruct((n_ctx + 1, 2, 1, S_W), F32)],
        operands=[zm, zs, zt, zm, zs, zt, cn0, m0, gb, gbt,
                  sc["tri"], sc["sel"], sc["tritile"], sc["neg"], sc["kmask"], sc["vmask"], sc["bd2"]])


def _gla_body(first_ref, seq_ref, bwd_ref,
              zgf_ref, zsf_ref, zgb_ref, zsb_ref, s0_ref, w2_ref, b2_ref,
              tri_ref, gath_ref, mcat_ref, kmask_ref, vmask_ref, bd_ref,
              of_ref, ob_ref, sout_ref, ss_ref):
    r = pl.program_id(0)
    n = S_CHUNK
    kd = G_HEADS * G_DK
    vd = G_HEADS * G_DV
    ones_v = jnp.ones((n, vd), BF16)
    kmask = kmask_ref[...]
    vmask = vmask_ref[...]
    bd = bd_ref[...]
    units = _scan_units()
    rows = _chunk_rows
    zg_refs, zs_refs, o_refs = (zgf_ref, zgb_ref), (zsf_ref, zsb_ref), (of_ref, ob_ref)

    def per_unit(f):
        return [f(d, c) for d, c in units]

    def bd_k(kt):
        return jnp.concatenate([kt] * G_HEADS, axis=0) * kmask

    q = per_unit(lambda d, c: zg_refs[d][rows(c), 0:kd] * (G_DK ** -0.5))
    k = per_unit(lambda d, c: zg_refs[d][rows(c), kd:2 * kd])
    v = per_unit(lambda d, c: zg_refs[d][rows(c), 2 * kd:2 * kd + vd].astype(BF16))
    x = per_unit(lambda d, c: _f32dot(zs_refs[d][rows(c), :].astype(BF16), w2_ref[d]) + b2_ref[d])
    lg = [_log_sigmoid(xi) * (1.0 / G_TAU) for xi in x]
    lg_split = [_split(l) for l in lg]
    yield
    bc = [_f32dot(tri_ref[d], hi) + _f32dot(tri_ref[d], lo)
          for (d, _), (hi, lo) in zip(units, lg_split)]
    bc_split = [_split(b) for b in bc]
    yield
    r_small = [_f32dot(gath_ref[d], hi) + _f32dot(gath_ref[d], lo)
               for (d, _), (hi, lo) in zip(units, bc_split)]
    qb = [qi.astype(BF16) for qi in q]
    kbd = [bd_k(ki.astype(BF16)) for ki in k]
    yield
    att = [mcat_ref[d, 0] * lax.dot_general(qi, ki, NT_DIMS, preferred_element_type=F32)
           for (d, _), qi, ki in zip(units, qb, kbd)]
    for lev in range(G_LEVELS):
        b = 1 << lev
        if lev < G_SMALL_LEVELS:
            r_lev = [rs[lev * n:(lev + 1) * n] for rs in r_small]
        else:
            r_lev = [jnp.concatenate(
                [jnp.broadcast_to(bci[_gla_boundary(p0, b, d):_gla_boundary(p0, b, d) + 1, :], (2 * b, kd))
                 for p0 in range(0, n, 2 * b)], axis=0) for (d, _), bci in zip(units, bc)]
        e_lev = [jnp.exp(-jnp.abs(bci - ri)) for bci, ri in zip(bc, r_lev)]
        qt = [(qi * ei).astype(BF16) for qi, ei in zip(q, e_lev)]
        kt = [bd_k((ki * ei).astype(BF16)) for ki, ei in zip(k, e_lev)]
        yield
        p = [lax.dot_general(qi, ki, NT_DIMS, preferred_element_type=F32) for qi, ki in zip(qt, kt)]
        att = [ai + mcat_ref[d, lev + 1] * pi for (d, _), ai, pi in zip(units, att, p)]
        yield
    v_bd = [jnp.concatenate([vi] * G_HEADS, axis=0) * vmask for vi in v]
    intra = [_f32dot(ai.astype(BF16), vi) for ai, vi in zip(att, v_bd)]
    yield
    q_inc = [(qi * jnp.exp(bci)).astype(BF16) for qi, bci in zip(q, bc)]
    k_suf = [(ki * jnp.exp(bci[(n - 1 if d == 0 else 0):(n if d == 0 else 1), :] - bci)).astype(BF16)
             for (d, _), ki, bci in zip(units, k, bc)]
    decay = [jnp.exp(lax.dot_general(hi, ones_v, TN_DIMS, preferred_element_type=F32)
                     + lax.dot_general(lo, ones_v, TN_DIMS, preferred_element_type=F32))
             for hi, lo in lg_split]
    upd = [bd * lax.dot_general(ki, vi, TN_DIMS, preferred_element_type=F32) for ki, vi in zip(k_suf, v)]
    yield
    fresh = first_ref[r] != 0
    zero = first_ref[r] == 2
    states = [jnp.where(fresh, jnp.where(zero, 0.0, s0_ref[0, d]), ss_ref[d]) for d in (0, 1)]
    outs = []
    for i, (d, c) in enumerate(units):
        inter = _f32dot(q_inc[i], states[d].astype(BF16))
        outs.append(inter + intra[i])
        states[d] = decay[i] * states[d] + upd[i]
        yield
    for (d, c), o in zip(units, outs):
        o_refs[d][rows(c), :] = o
    for d in (0, 1):
        ss_ref[d] = states[d]
        for h in range(G_HEADS):
            sout_ref[0, d, h] = states[d][h * G_DK:(h + 1) * G_DK, h * G_DV:(h + 1) * G_DV]


def _gla_part(layer, n_ctx, zg, zs, s0, w2p, b2p, sc):
    nt = zg.shape[0]
    kd = G_HEADS * G_DK
    vd = G_HEADS * G_DV
    n = S_CHUNK
    fwd = lambda r, first, seq, bwd: (r, 0)
    bwd_ = lambda r, first, seq, bwd: (bwd[r], 0)
    seq5 = lambda r, first, seq, bwd: (jnp.maximum(seq[r] - n_ctx, 0), layer, 0, 0, 0)
    out5 = lambda r, first, seq, bwd: (jnp.minimum(seq[r], n_ctx), 0, 0, 0, 0)
    const = lambda r, first, seq, bwd: (0, 0)
    const3 = lambda r, first, seq, bwd: (0, 0, 0)
    const4 = lambda r, first, seq, bwd: (0, 0, 0, 0)
    return dict(
        body=_gla_body,
        in_specs=[pl.BlockSpec((ROW_BLOCK, QKV_W), fwd), pl.BlockSpec((ROW_BLOCK, LANES), fwd),
                  pl.BlockSpec((ROW_BLOCK, QKV_W), bwd_), pl.BlockSpec((ROW_BLOCK, LANES), bwd_),
                  pl.BlockSpec((1, None, 2, kd, vd), seq5),
                  _layer_spec(w2p, layer), _layer_spec(b2p, layer),
                  pl.BlockSpec((2, n, n), const3),
                  pl.BlockSpec((2, G_SMALL_LEVELS * n, n), const3),
                  pl.BlockSpec((2, G_LEVELS + 1, n, G_HEADS * n), const4),
                  pl.BlockSpec((G_HEADS * n, kd), const), pl.BlockSpec((G_HEADS * n, vd), const),
                  pl.BlockSpec((kd, vd), const)],
        out_specs=[pl.BlockSpec((ROW_BLOCK, vd), fwd), pl.BlockSpec((ROW_BLOCK, vd), bwd_),
                   pl.BlockSpec((1, 2, G_HEADS, G_DK, G_DV), out5)],
        scratch_shapes=[pltpu.VMEM((2, kd, vd), F32)],
        out_shape=[jax.ShapeDtypeStruct((nt, vd), F32), jax.ShapeDtypeStruct((nt, vd), F32),
                   jax.ShapeDtypeStruct((n_ctx + 1, 2, G_HEADS, G_DK, G_DV), F32)],
        operands=[zg, zs, zg, zs, s0, w2p, b2p, sc["tri"], sc["gath"], sc["mcat"],
                  sc["kmask"], sc["vmask"], sc["bd"]])


_DONE = object()


def _scan_call(plan, parts, name):
    n_in = [len(p["operands"]) for p in parts]
    n_out = [len(p["out_specs"]) for p in parts]
    n_scr = [len(p["scratch_shapes"]) for p in parts]

    def kern(first_ref, seq_ref, bwd_ref, *refs):
        ins, outs, scrs = refs[:sum(n_in)], refs[sum(n_in):sum(n_in) + sum(n_out)], refs[sum(n_in) + sum(n_out):]
        @pl.when(pl.program_id(0) == 0)
        def _():
            for s in scrs:
                s[...] = jnp.zeros(s.shape, s.dtype)

        oi = oo = os_ = 0
        bodies = []
        for p, a, b, c in zip(parts, n_in, n_out, n_scr):
            bodies.append(p["body"](first_ref, seq_ref, bwd_ref, *ins[oi:oi + a], *outs[oo:oo + b],
                                    *scrs[os_:os_ + c]))
            oi, oo, os_ = oi + a, oo + b, os_ + c
        while bodies:
            bodies = [g for g in bodies if next(g, _DONE) is not _DONE]

    nb = plan["first"].shape[0]
    grid_spec = pltpu.PrefetchScalarGridSpec(
        num_scalar_prefetch=3,
        grid=(nb,),
        in_specs=[s for p in parts for s in p["in_specs"]],
        out_specs=[s for p in parts for s in p["out_specs"]],
        scratch_shapes=[s for p in parts for s in p["scratch_shapes"]],
    )
    outs = pl.pallas_call(
        kern,
        grid_spec=grid_spec,
        out_shape=[s for p in parts for s in p["out_shape"]],
        compiler_params=_cparams(("arbitrary",)),
        name=name,
    )(plan["first"], plan["seq"], plan["bwd"], *[o for p in parts for o in p["operands"]])
    res, oo = [], 0
    for b in n_out:
        res.append(outs[oo:oo + b])
        oo += b
    return res


def _cache_kv_kernel(ckv_ref, kr_ref, wk_ref, wv_ref, k_ref, v_ref):
    ckvb = ckv_ref[...].astype(BF16)
    kn = _f32dot(ckvb, wk_ref[...])
    v_ref[...] = _with_ones_lane(_f32dot(ckvb, wv_ref[...])).astype(BF16)
    kr = kr_ref[...]
    for h in range(A_HEADS):
        sl = slice(h * A_HPAD, (h + 1) * A_HPAD)
        k_ref[:, sl] = (kn[:, sl] + kr).astype(BF16)


def _cache_kv_call(layer, ckv, krp, wk, wv):
    n_dec, _, past_len, _ = ckv.shape
    hw = A_HEADS * A_HPAD
    row = lambda b: (b, 0)
    cache = lambda b: (b, layer, 0, 0)
    return pl.pallas_call(
        _cache_kv_kernel,
        grid=(n_dec,),
        in_specs=[pl.BlockSpec((None, None, past_len, A_DC), cache),
                  pl.BlockSpec((None, None, past_len, LANES), cache),
                  _layer_spec(wk, layer), _layer_spec(wv, layer)],
        out_specs=[pl.BlockSpec((past_len, hw), row), pl.BlockSpec((past_len, hw), row)],
        out_shape=[jax.ShapeDtypeStruct((n_dec * past_len, hw), BF16),
                   jax.ShapeDtypeStruct((n_dec * past_len, hw), BF16)],
        compiler_params=_cparams(("parallel",)),
        name="mla_cache_kv",
    )(ckv, krp, wk, wv)


def _attn_kernel(*refs, n_seg, heads_per_group):
    q_ref = refs[0]
    kv_refs = refs[1:1 + 2 * n_seg]
    o_ref = refs[1 + 2 * n_seg]
    tiles = []
    for j in range(n_seg):
        nk = kv_refs[2 * j].shape[0]
        tk = min(nk, A_KEY_TILE)
        tiles += [(j, t0, tk) for t0 in range(0, nk, tk)]
    head = lambda h: slice(h * A_HPAD, (h + 1) * A_HPAD)

    def scores(h, tile):
        j, t0, tk = tile
        return lax.dot_general(q_ref[:, head(h)], kv_refs[2 * j][t0:t0 + tk, head(h)], NT_DIMS,
                               preferred_element_type=F32)

    def lane_max(acc, s):
        for c0 in range(0, s.shape[1], LANES):
            part = s[:, c0:c0 + LANES]
            acc = part if acc is None else jnp.maximum(acc, part)
        return acc

    groups = [list(range(g0, min(g0 + heads_per_group, A_HEADS))) for g0 in range(0, A_HEADS, heads_per_group)]
    s_cur = {h: [scores(h, t) for t in tiles] for h in groups[0]}
    for gi, grp in enumerate(groups):
        nxt = groups[gi + 1] if gi + 1 < len(groups) else []
        m = {}
        for h in grp:
            mx = None
            for s in s_cur[h]:
                mx = lane_max(mx, s)
            m[h] = jnp.max(mx, axis=1, keepdims=True)
        s_next = {h: [] for h in nxt}
        acc = {h: None for h in grp}
        for ti, tile in enumerate(tiles):
            j, t0, tk = tile
            for h in nxt:
                s_next[h].append(scores(h, tile))
            p = {h: jnp.exp((s_cur[h][ti] - m[h]).astype(BF16)) for h in grp}
            for h in grp:
                pv = _f32dot(p[h], kv_refs[2 * j + 1][t0:t0 + tk, head(h)])
                acc[h] = pv if acc[h] is None else acc[h] + pv
        for h in grp:
            o_ref[:, head(h)] = (acc[h] / acc[h][:, A_DV:A_DV + 1]).astype(BF16)
        s_cur = s_next


def _attn_ctx_call(q, k, v, n_rows):
    hw = A_HEADS * A_HPAD
    tm = ROW_BLOCK
    row = lambda i: (i, 0)
    return pl.pallas_call(
        functools.partial(_attn_kernel, n_seg=1, heads_per_group=A_HEADS),
        grid=(n_rows // tm,),
        in_specs=[pl.BlockSpec((tm, hw), row)] * 3,
        out_specs=pl.BlockSpec((tm, hw), row),
        out_shape=jax.ShapeDtypeStruct((n_rows, hw), BF16),
        compiler_params=_cparams(("parallel",)),
        name="mla_attn_ctx",
    )(q, k, v)


def _attn_dec_call(q, k, v, kc, vc, n_ctx_rows, n_dec, dec_len, past_len):
    hw = A_HEADS * A_HPAD
    tm = ROW_BLOCK
    nq = dec_len // tm
    assert n_ctx_rows % dec_len == 0
    qmap = lambda b, i: (n_ctx_rows // tm + b * nq + i, 0)
    own = lambda b, i: (n_ctx_rows // dec_len + b, 0)
    return pl.pallas_call(
        functools.partial(_attn_kernel, n_seg=2, heads_per_group=4),
        grid=(n_dec, nq),
        in_specs=[pl.BlockSpec((tm, hw), qmap),
                  pl.BlockSpec((past_len, hw), lambda b, i: (b, 0)),
                  pl.BlockSpec((past_len, hw), lambda b, i: (b, 0)),
                  pl.BlockSpec((dec_len, hw), own),
                  pl.BlockSpec((dec_len, hw), own)],
        out_specs=pl.BlockSpec((tm, hw), lambda b, i: (b * nq + i, 0)),
        out_shape=jax.ShapeDtypeStruct((n_dec * dec_len, hw), BF16),
        compiler_params=_cparams(("parallel", "arbitrary")),
        name="mla_attn_dec",
    )(q, kc, vc, k, v)


def _out_ffn_kernel(*refs, n_x, nb0, nb_ctx, alpha, blk0):
    (hf_ref, hb_ref, of_ref, ob_ref, actx_ref, adec_ref, mo_ref, gg_ref) = refs[:8]
    x_refs = refs[8:8 + n_x]
    (mod_ref, mng_ref, gng_ref, avg_ref, wm_ref, wg_ref, wa_ref, l1g_ref, l1b_ref,
     wfg_ref, wfu_ref, wfd_ref, l2g_ref, l2b_ref, o_ref) = refs[8 + n_x:]
    d = D_MODEL
    avg = avg_ref[...]
    mod = mod_ref[0]
    is_ctx = pl.program_id(0) + blk0 < nb_ctx
    x_in = _read_x(x_refs, nb0, blk0)
    tm = o_ref.shape[0]
    parts = [slice(p0, p0 + ROW_BLOCK) for p0 in range(0, tm, ROW_BLOCK)]
    f = wfg_ref.shape[1]
    tiles = -(-f // MXU_TILE)
    edges = [min(f, MXU_TILE * (tiles * c // FFN_CHUNKS)) for c in range(FFN_CHUNKS)] + [f]

    def ffn_chunk(hh, c):
        sl = slice(edges[c], edges[c + 1])
        g = _f32dot(hh, wfg_ref[:, sl])
        u = _f32dot(hh, wfu_ref[:, sl])
        return _f32dot((g * _sigmoid(g) * u).astype(BF16), wfd_ref[sl, :])

    each = lambda fn: [fn(s) for s in parts]
    a = each(lambda s: jnp.where(is_ctx, actx_ref[s, :], adec_ref[s, :]))
    hm = each(lambda s: hf_ref[s, :] + hb_ref[s, :])
    og = each(lambda s: of_ref[s, :] + ob_ref[s, :])
    hc = [x - _dot_x_c(x, avg) for x in hm]
    hvar = [_dot_x_c(x * x, avg) for x in hc]
    gms = [_dot_x_c(x * x, avg) for x in og]
    m_out = [x * lax.rsqrt(v + 1e-6) * mng_ref[...] * _sigmoid(mo_ref[s, :]) for x, v, s in zip(hc, hvar, parts)]
    gg = each(lambda s: gg_ref[s, :])
    g_out = [x * lax.rsqrt(v + 1e-6) * gng_ref[...] * (g * _sigmoid(g)) for x, v, g in zip(og, gms, gg)]
    mix = [_f32dot(m.astype(BF16), wm_ref[...]) + _f32dot(g.astype(BF16), wg_ref[...]) + _f32dot(ai, wa_ref[...])
           for m, g, ai in zip(m_out, g_out, a)]
    y = [alpha * x_in[s, :] + mod[:, 2 * d:3 * d] * mi for s, mi in zip(parts, mix)]
    x1 = jnp.concatenate([_layer_norm(yi, l1g_ref[...], l1b_ref[...]) for yi in y], axis=0)
    hh = (x1 * (1.0 + mod[:, 4 * d:5 * d]) + mod[:, 3 * d:4 * d]).astype(BF16)
    acc = ffn_chunk(hh, 0)
    for c in range(1, FFN_CHUNKS):
        acc = acc + ffn_chunk(hh, c)
    o_ref[...] = _layer_norm(alpha * x1 + mod[:, 5 * d:6 * d] * acc, l2g_ref[...], l2b_ref[...])


def _out_ffn_call(layer, hf, hb, of, ob, a_ctx, a_dec, zm, zg, x, mods, mod_map, mng, gng, avg, wm, wg, wa, l1g, l1b,
                  wfg, wfu, wfd, l2g, l2b, alpha, blk0=0, n_blk=None):
    tm = WIDE_ROWS
    x_specs, x_ops = _x_specs(x, tm, blk0)
    nt = sum(a.shape[0] for a in x_ops)
    n_blk = nt // tm if n_blk is None else n_blk
    d = D_MODEL
    hw = M_HEADS * M_DV
    ahw = A_HEADS * A_HPAD
    nb_ctx = a_ctx.shape[0] // tm
    row = lambda i: (i + blk0, 0)
    gate_col = lambda i: (i + blk0, QKV_W // hw)
    resident = dict(pipeline_mode=pl.Buffered(1))
    return pl.pallas_call(
        functools.partial(_out_ffn_kernel, n_x=len(x_ops), nb0=x_ops[0].shape[0] // tm, nb_ctx=nb_ctx,
                          alpha=alpha, blk0=blk0),
        grid=(n_blk,),
        in_specs=[pl.BlockSpec((tm, hw), row)] * 4
                 + [pl.BlockSpec((tm, ahw), lambda i: (jnp.minimum(i + blk0, nb_ctx - 1), 0)),
                    pl.BlockSpec((tm, ahw), lambda i: (jnp.maximum(i + blk0 - nb_ctx, 0), 0)),
                    pl.BlockSpec((tm, hw), gate_col), pl.BlockSpec((tm, hw), gate_col)]
                 + x_specs
                 + [pl.BlockSpec((1, 1, 6 * d), lambda i: (mod_map((i + blk0) * (tm // ROW_BLOCK)), 0, 0)),
                    _layer_spec(mng, layer), _layer_spec(gng, layer),
                    pl.BlockSpec((hw, hw), lambda i: (0, 0)),
                    _layer_spec(wm, layer, **resident), _layer_spec(wg, layer, **resident),
                    _layer_spec(wa, layer, **resident),
                    _layer_spec(l1g, layer), _layer_spec(l1b, layer),
                    _layer_spec(wfg, layer, **resident), _layer_spec(wfu, layer, **resident),
                    _layer_spec(wfd, layer, **resident),
                    _layer_spec(l2g, layer), _layer_spec(l2b, layer)],
        out_specs=pl.BlockSpec((tm, d), lambda i: (i, 0)),
        out_shape=jax.ShapeDtypeStruct((n_blk * tm, d), F32),
        compiler_params=_cparams(("parallel",)),
        name="out_proj_ffn",
    )(hf, hb, of, ob, a_ctx, a_dec, zm, zg, *x_ops, mods, mng, gng, avg, wm, wg, wa, l1g, l1b,
      wfg, wfu, wfd, l2g, l2b)


def _block_diag_heads(t):
    h = t.shape[-3]
    eye = jnp.eye(h, dtype=t.dtype)
    out = jnp.einsum('...hkv,hg->...hkgv', t, eye)
    return out.reshape(t.shape[:-3] + (h * t.shape[-2], h * t.shape[-1]))


def _rope_tables(dec_len):
    half = A_DROPE // 4
    inv = ROPE_BASE ** (-jnp.arange(half, dtype=F32) / half)
    pos = jnp.arange(dec_len, dtype=jnp.int32)
    rows = (pos // GRID_W).astype(F32)[:, None] * inv[None, :]
    cols = (pos % GRID_W).astype(F32)[:, None] * inv[None, :]
    cr, sr, cc, sc = jnp.cos(rows), jnp.sin(rows), jnp.cos(cols), jnp.sin(cols)
    cos32 = jnp.concatenate([cr, cr, cc, cc], axis=1)
    sin32 = jnp.concatenate([-sr, sr, -sc, sc], axis=1)
    ones_lo = jnp.ones((dec_len, KR_LO), F32)
    ones_hi = jnp.ones((dec_len, A_HPAD - KR_HI), F32)
    cos = jnp.concatenate([ones_lo, cos32, ones_hi], axis=1)
    sin = jnp.concatenate([0 * ones_lo, sin32, 0 * ones_hi], axis=1)
    kcos = jnp.concatenate([0 * ones_lo, cos32, 0 * ones_hi], axis=1)
    lane = jnp.arange(A_HPAD)
    rope_lane = ((lane >= KR_LO) & (lane < KR_HI)).astype(F32)[None, :]
    ctx_cos = jnp.ones((WIDE_ROWS, A_HPAD), F32)
    ctx_sin = jnp.zeros((WIDE_ROWS, A_HPAD), F32)
    ctx_kcos = jnp.broadcast_to(rope_lane, (WIDE_ROWS, A_HPAD))
    return (jnp.concatenate([ctx_cos, cos]), jnp.concatenate([ctx_sin, sin]), jnp.concatenate([ctx_kcos, kcos]))


def _rope_partner(w):
    e = A_DROPE // 4
    return jnp.concatenate([w[..., e:2 * e], w[..., 0:e], w[..., 3 * e:4 * e], w[..., 2 * e:3 * e]], axis=-1)


def kernel(x_prompt, x_sample, state_mlstm_C, state_mlstm_n, state_mlstm_m, state_gla_S, cache_mla_ckv, cache_mla_krope, c, c_ctx, w_ada, b_ada, w_in, m_gate_b, m_norm_g, g_w2, g_b2, g_norm_g, a_q_norm_g, a_kv_norm_g, a_w_uq, a_w_ukv, w_out, ln1_g, ln1_b, w_ffn_gate, w_ffn_up, w_ffn_down, ln2_g, ln2_b):
    n_ctx, ctx_len, d = x_prompt.shape
    n_dec, dec_len, _ = x_sample.shape
    depth = w_in.shape[0]
    past_len = cache_mla_ckv.shape[2]
    assert d == D_MODEL and ctx_len == ROW_BLOCK and dec_len % ROW_BLOCK == 0 and past_len % 8 == 0
    assert (n_ctx * ctx_len) % WIDE_ROWS == 0 and dec_len % WIDE_ROWS == 0
    assert n_dec + 1 <= COND_ROWS
    n_ctx_rows = n_ctx * ctx_len
    n_dec_rows = n_dec * dec_len
    nt = n_ctx_rows + n_dec_rows
    nb_ctx = n_ctx_rows // ROW_BLOCK
    nb_seq = dec_len // ROW_BLOCK
    alpha = (2.0 * depth) ** 0.25

    blocks = np.arange(nt // ROW_BLOCK)
    dec_b = np.maximum(blocks - nb_ctx, 0)
    is_ctx = blocks < nb_ctx
    plan = {
        "first": jnp.asarray(np.where(is_ctx, 2, (dec_b % nb_seq) == 0).astype(np.int32)),
        "seq": jnp.asarray(np.where(is_ctx, blocks, n_ctx + dec_b // nb_seq).astype(np.int32)),
        "bwd": jnp.asarray(np.where(is_ctx, blocks,
                                    nb_ctx + (dec_b // nb_seq) * nb_seq + nb_seq - 1 - dec_b % nb_seq).astype(np.int32)),
    }

    def mod_map_for(layer):
        def mod_map(i):
            return layer * COND_ROWS + jnp.where(i < nb_ctx, 0, 1 + (i - nb_ctx) // nb_seq)
        return mod_map

    cond8 = jnp.zeros((COND_ROWS, d), F32).at[0].set(c_ctx).at[1:1 + n_dec].set(c)
    mods = _ada_call(cond8, w_ada, b_ada).reshape(depth * COND_ROWS, 1, 6 * d)

    pts = [int(p) for p in np.cumsum(IN_SPLITS)[:-1]]
    (wmq, wmk, wmv, wmo, wmg, wgq, wgk, wgv, wgg, wga, wacq, wackv, wakr) = jnp.split(w_in.astype(BF16), pts, axis=-1)
    zpad = lambda n: jnp.zeros((depth, d, n), BF16)
    s1 = jnp.concatenate([wmg, wga, zpad(KR_LO - GA_LO - 2 * G_RANK), wakr, zpad(LANES - KR_HI)], axis=-1)
    s2 = jnp.concatenate([zpad(KR_LO), _rope_partner(wakr), zpad(LANES - KR_HI)], axis=-1)
    w_in_p = jnp.concatenate([wmq, wmk, wmv, wmo, wgq, wgk, wgv, wgg, wacq, wackv, s1, s2], axis=-1)
    w_gt = jnp.swapaxes(wmg, 1, 2)
    gate_b = jnp.zeros((depth, 1, LANES), F32).at[:, 0, 0:GATE_ROWS].set(m_gate_b)
    gate_bt = m_gate_b.reshape(depth, GATE_ROWS, 1)
    w2p = jnp.zeros((depth, 2, LANES, G_HEADS * G_DK), F32)
    w2p = (w2p.at[:, 0, GA_LO:GA_LO + G_RANK].set(g_w2[:, 0])
           .at[:, 1, GA_LO + G_RANK:GA_LO + 2 * G_RANK].set(g_w2[:, 1]).astype(BF16))
    b2p = g_b2.reshape(depth, 2, 1, G_HEADS * G_DK)
    uq = a_w_uq.reshape(depth, A_DQ, A_HEADS, A_DNOPE + A_DROPE)
    hp = lambda n: jnp.zeros((depth, uq.shape[1], A_HEADS, n), F32)
    wuq = jnp.concatenate([uq, hp(A_HPAD - KR_HI)], axis=-1).reshape(depth, A_DQ, -1).astype(BF16)
    wuqs = jnp.concatenate([hp(KR_LO), _rope_partner(uq[..., A_DNOPE:]), hp(A_HPAD - KR_HI)],
                           axis=-1).reshape(depth, A_DQ, -1).astype(BF16)
    ukv = a_w_ukv.reshape(depth, A_DC, A_HEADS, A_DNOPE + A_DV)
    kp = jnp.zeros((depth, A_DC, A_HEADS, A_HPAD - A_DNOPE), F32)
    wk = jnp.concatenate([ukv[..., :A_DNOPE], kp], axis=-1).reshape(depth, A_DC, -1).astype(BF16)
    wv = jnp.concatenate([ukv[..., A_DNOPE:], kp], axis=-1).reshape(depth, A_DC, -1).astype(BF16)
    mw, gw = M_HEADS * M_DV, G_HEADS * G_DV
    wo_m = w_out[:, 0:mw].astype(BF16)
    wo_g = w_out[:, mw:mw + gw].astype(BF16)
    wo_a = w_out[:, mw + gw:].reshape(depth, A_HEADS, A_DV, d)
    wo_a = jnp.concatenate([wo_a, jnp.zeros((depth, A_HEADS, A_HPAD - A_DV, d), F32)], axis=2)
    wo_a = wo_a.reshape(depth, A_HEADS * A_HPAD, d).astype(BF16)
    wfg, wfu, wfd = w_ffn_gate.astype(BF16), w_ffn_up.astype(BF16), w_ffn_down.astype(BF16)

    sconst = _scan_consts()
    head_lanes = jnp.asarray((np.arange(M_HEADS)[:, None] == (np.arange(S_W) // M_DV)[None, :]).astype(np.float32))
    hi = np.arange(mw) // M_DV
    avg = jnp.asarray((hi[:, None] == hi[None, :]).astype(np.float32) / M_DV, BF16)
    cos, sin, kcos = _rope_tables(dec_len)
    nbw_ctx = n_ctx_rows // WIDE_ROWS
    nbw_seq = dec_len // WIDE_ROWS

    def rope_map(i):
        return jnp.where(i < nbw_ctx, 0, 1 + (i - nbw_ctx) % nbw_seq)

    krp_all = jnp.zeros(cache_mla_krope.shape[:-1] + (LANES,), F32).at[..., KR_LO:KR_HI].set(cache_mla_krope)
    n_bd = jnp.einsum('bldhk,hg->bldhkg', state_mlstm_n, head_lanes)
    cn0 = jnp.concatenate([_block_diag_heads(state_mlstm_C),
                           n_bd.reshape(n_dec, depth, 2, M_HEADS * M_DK, S_W)], axis=-1)
    m0 = jnp.repeat(state_mlstm_m, M_DV, axis=-1)[:, :, :, None, :]
    s0 = _block_diag_heads(state_gla_S)
    vec = lambda p: p[:, None, :]

    x = (x_prompt.reshape(n_ctx_rows, d), x_sample.reshape(n_dec_rows, d))
    new_c, new_n, new_m, new_s, new_ckv, new_kr = [], [], [], [], [], []
    for l in range(depth):
        mod_map = mod_map_for(l)
        zm, zg, zs, zt, q, k, v, ckv, kr = _in_mla_call(
            l, x, mods, mod_map, rope_map, w_in_p, w_gt, cos, sin, kcos,
            vec(a_q_norm_g), vec(a_kv_norm_g), wuq, wuqs, wk, wv)

        (hf, hb, c_fin, n_fin, m_fin), (of, ob, s_fin) = _scan_call(
            plan,
            [_mlstm_part(l, n_ctx, zm, zs, zt, cn0, m0, gate_b, gate_bt, sconst),
             _gla_part(l, n_ctx, zg, zs, s0, w2p, b2p, sconst)],
            "mixer_scan")

        kc, vc = _cache_kv_call(l, cache_mla_ckv, krp_all, wk, wv)
        a_ctx = _attn_ctx_call(q, k, v, n_ctx_rows)
        a_dec = _attn_dec_call(q, k, v, kc, vc, n_ctx_rows, n_dec, dec_len, past_len)

        out_ffn = functools.partial(
            _out_ffn_call, l, hf, hb, of, ob, a_ctx, a_dec, zm, zg, x, mods, mod_map,
            vec(m_norm_g), vec(g_norm_g), avg, wo_m, wo_g, wo_a, vec(ln1_g), vec(ln1_b),
            wfg, wfu, wfd, vec(ln2_g), vec(ln2_b), alpha)
        if l + 1 < depth:
            x = out_ffn()
        else:
            y_prompt = out_ffn(blk0=0, n_blk=nbw_ctx).reshape(n_ctx, ctx_len, d)
            y_sample = out_ffn(blk0=nbw_ctx, n_blk=n_dec * nbw_seq).reshape(n_dec, dec_len, d)

        new_c.append(c_fin[:n_ctx])
        new_n.append(n_fin[:n_ctx, ..., 0])
        new_m.append(m_fin[:n_ctx, :, 0, ::M_DV])
        new_s.append(s_fin[:n_ctx])
        new_ckv.append(ckv[:n_ctx_rows].reshape(n_ctx, ctx_len, A_DC))
        new_kr.append(kr[:n_ctx_rows].reshape(n_ctx, ctx_len, A_DROPE))

    st = lambda xs: jnp.stack(xs, axis=1)
    return (y_prompt, y_sample, st(new_c), st(new_n), st(new_m), st(new_s), st(new_ckv), st(new_kr))
```

```python
import functools

import numpy as np
import jax
import jax.numpy as jnp
from jax import lax
from jax.experimental import pallas as pl
from jax.experimental.pallas import tpu as pltpu

F32 = jnp.float32
BF16 = jnp.bfloat16

D_MODEL = 1024
GRID_W = 64
M_HEADS, M_DK, M_DV = 4, 32, 64
G_HEADS, G_DK, G_DV = 4, 32, 64
G_RANK = 16
G_TAU = 16.0
A_HEADS, A_DNOPE, A_DROPE, A_DV = 8, 64, 32, 64
A_DQ, A_DC = 256, 128
A_SCALE = (A_DNOPE + A_DROPE) ** -0.5
ROPE_BASE = 10000.0
IN_SPLITS = (M_HEADS * M_DK, M_HEADS * M_DK, M_HEADS * M_DV, M_HEADS * M_DV, 4 * M_HEADS,
             G_HEADS * G_DK, G_HEADS * G_DK, G_HEADS * G_DV, G_HEADS * G_DV, 2 * G_RANK,
             A_DQ, A_DC, A_DROPE)

LANES = 128
MXU_TILE = 256
VMEM_LIMIT = 56 * 1024 * 1024
ROW_BLOCK = 256
WIDE_ROWS = 512
S_CHUNK = 64
S_W = M_HEADS * M_DV
G_LEVELS = 6
G_SMALL_LEVELS = 3
A_HPAD = 128
A_KEY_TILE = 512
FFN_CHUNKS = 2

ZM_W = 768
ZG_W = 768
ZQ_W = 384
ZS_W = 256
Z_W = ZM_W + ZG_W + ZQ_W + ZS_W
QKV_W = 2 * M_HEADS * M_DK + M_HEADS * M_DV
GATE_ROWS = 4 * M_HEADS
GA_LO = GATE_ROWS
COND_ROWS = 8
ADA_COLS = 1536
KR_LO, KR_HI = A_DNOPE, A_DNOPE + A_DROPE

assert (M_HEADS, M_DK, M_DV) == (G_HEADS, G_DK, G_DV) and S_CHUNK == M_DV and M_HEADS * M_DK == LANES


def _cparams(sem):
    return pltpu.CompilerParams(dimension_semantics=sem, vmem_limit_bytes=VMEM_LIMIT)


def _f32dot(a, b):
    return jnp.dot(a, b, preferred_element_type=F32)


NT_DIMS = (((1,), (1,)), ((), ()))
TN_DIMS = (((0,), (0,)), ((), ()))


def _split(x):
    hi = x.astype(BF16)
    lo = (x - hi.astype(F32)).astype(BF16)
    return hi, lo


def _dot_c_x(c, x):
    hi, lo = _split(x)
    return _f32dot(c, hi) + _f32dot(c, lo)


def _dot_x_c(x, c):
    hi, lo = _split(x)
    return _f32dot(hi, c) + _f32dot(lo, c)


def _log_sigmoid(x):
    return jnp.minimum(x, 0.0) - jnp.log(1.0 + jnp.exp(-jnp.abs(x)))


def _sigmoid(x):
    return 1.0 / (1.0 + jnp.exp(-x))


def _rms(x, g, eps=1e-6):
    return x * lax.rsqrt(jnp.mean(x * x, axis=-1, keepdims=True) + eps) * g


def _layer_norm(y, g, b, eps=1e-5):
    yc = y - jnp.mean(y, axis=-1, keepdims=True)
    return yc * lax.rsqrt(jnp.mean(yc * yc, axis=-1, keepdims=True) + eps) * g + b


def _with_ones_lane(v):
    lane = lax.broadcasted_iota(jnp.int32, v.shape, 1) % A_HPAD
    return jnp.where(lane == A_DV, 1.0, v)


def _x_specs(x, tm, blk0=0):
    if not isinstance(x, tuple):
        return [pl.BlockSpec((tm, x.shape[1]), lambda i: (i + blk0, 0))], [x]
    nb0 = x[0].shape[0] // tm
    return ([pl.BlockSpec((tm, x[0].shape[1]), lambda i: (jnp.minimum(i + blk0, nb0 - 1), 0)),
             pl.BlockSpec((tm, x[1].shape[1]), lambda i: (jnp.maximum(i + blk0 - nb0, 0), 0))], list(x))


def _read_x(refs, nb0, blk0=0):
    if len(refs) == 1:
        return refs[0][...]
    return jnp.where(pl.program_id(0) + blk0 < nb0, refs[0][...], refs[1][...])


def _ada_kernel(c_ref, w_ref, b_ref, o_ref):
    c = c_ref[...]
    s = c * _sigmoid(c)
    o_ref[0] = _f32dot(s.astype(BF16), w_ref[0].astype(BF16)) + b_ref[0]


def _ada_call(cond8, w_ada, b_ada):
    depth, d, n = w_ada.shape
    tn = ADA_COLS
    return pl.pallas_call(
        _ada_kernel,
        grid=(depth, n // tn),
        in_specs=[pl.BlockSpec((COND_ROWS, d), lambda l, j: (0, 0)),
                  pl.BlockSpec((1, d, tn), lambda l, j: (l, 0, j)),
                  pl.BlockSpec((1, 1, tn), lambda l, j: (l, 0, j))],
        out_specs=pl.BlockSpec((1, COND_ROWS, tn), lambda l, j: (l, 0, j)),
        out_shape=jax.ShapeDtypeStruct((depth, COND_ROWS, n), F32),
        compiler_params=_cparams(("arbitrary", "arbitrary")),
        name="ada_mod",
    )(cond8, w_ada, b_ada.reshape(depth, 1, n))


def _in_mla_kernel(*refs, n_x, nb0):
    x_refs = refs[:n_x]
    (mod_ref, w_ref, wgt_ref, cos_ref, sin_ref, kcos_ref, gq_ref, gkv_ref, wuq_ref, wuqs_ref, wk_ref, wv_ref,
     zm_ref, zg_ref, zs_ref, zt_ref, q_ref, k_ref, v_ref, ckv_ref, kr_ref) = refs[n_x:]
    d = D_MODEL
    mod = mod_ref[0]
    h = (_read_x(x_refs, nb0) * (1.0 + mod[:, d:2 * d]) + mod[:, 0:d]).astype(BF16)
    z = _f32dot(h, w_ref[...])
    zm_ref[...] = z[:, 0:ZM_W]
    zg_ref[...] = z[:, ZM_W:ZM_W + ZG_W]
    zs = z[:, ZM_W + ZG_W + ZQ_W:Z_W]
    zs_ref[...] = zs[:, 0:LANES]
    kr_ref[...] = zs[:, KR_LO:KR_HI]
    zt_ref[...] = lax.dot_general(wgt_ref[...], h, NT_DIMS, preferred_element_type=F32)
    zq = z[:, ZM_W + ZG_W:ZM_W + ZG_W + ZQ_W]
    cq = _rms(zq[:, 0:A_DQ], gq_ref[...]).astype(BF16)
    qn = _f32dot(cq, wuq_ref[...])
    qs = _f32dot(cq, wuqs_ref[...])
    cos = cos_ref[...]
    sin = sin_ref[...]
    for hd in range(A_HEADS):
        sl = slice(hd * A_HPAD, (hd + 1) * A_HPAD)
        q_ref[:, sl] = ((qn[:, sl] * cos + qs[:, sl] * sin) * A_SCALE).astype(BF16)
    ckv = _rms(zq[:, A_DQ:A_DQ + A_DC], gkv_ref[...])
    ckv_ref[...] = ckv
    ckvb = ckv.astype(BF16)
    kn = _f32dot(ckvb, wk_ref[...])
    v_ref[...] = _with_ones_lane(_f32dot(ckvb, wv_ref[...])).astype(BF16)
    kr = zs[:, 0:LANES] * kcos_ref[...] + zs[:, LANES:2 * LANES] * sin
    for hd in range(A_HEADS):
        sl = slice(hd * A_HPAD, (hd + 1) * A_HPAD)
        k_ref[:, sl] = (kn[:, sl] + kr).astype(BF16)


def _layer_spec(arr, layer, **kw):
    zeros = (0,) * (arr.ndim - 1)
    return pl.BlockSpec((None,) + arr.shape[1:], lambda *_: (layer,) + zeros, **kw)


def _in_mla_call(layer, x, mods, mod_map, rope_map, w_in_p, w_gt, cos, sin, kcos, gq, gkv, wuq, wuqs, wk, wv):
    tm = WIDE_ROWS
    x_specs, x_ops = _x_specs(x, tm)
    nt = sum(a.shape[0] for a in x_ops)
    d = D_MODEL
    hw = A_HEADS * A_HPAD
    row = lambda i: (i, 0)
    rope = lambda i: (rope_map(i), 0)
    resident = dict(pipeline_mode=pl.Buffered(1))
    return pl.pallas_call(
        functools.partial(_in_mla_kernel, n_x=len(x_ops), nb0=x_ops[0].shape[0] // tm),
        grid=(nt // tm,),
        in_specs=x_specs
                 + [pl.BlockSpec((1, 1, 6 * d), lambda i: (mod_map(i * (tm // ROW_BLOCK)), 0, 0)),
                    _layer_spec(w_in_p, layer, **resident), _layer_spec(w_gt, layer),
                    pl.BlockSpec((tm, LANES), rope), pl.BlockSpec((tm, LANES), rope), pl.BlockSpec((tm, LANES), rope),
                    _layer_spec(gq, layer), _layer_spec(gkv, layer),
                    _layer_spec(wuq, layer, **resident), _layer_spec(wuqs, layer, **resident),
                    _layer_spec(wk, layer, **resident), _layer_spec(wv, layer, **resident)],
        out_specs=[pl.BlockSpec((tm, ZM_W), row), pl.BlockSpec((tm, ZG_W), row), pl.BlockSpec((tm, LANES), row),
                   pl.BlockSpec((GATE_ROWS, tm), lambda i: (0, i)),
                   pl.BlockSpec((tm, hw), row), pl.BlockSpec((tm, hw), row), pl.BlockSpec((tm, hw), row),
                   pl.BlockSpec((tm, A_DC), row), pl.BlockSpec((tm, A_DROPE), row)],
        out_shape=[jax.ShapeDtypeStruct((nt, ZM_W), F32), jax.ShapeDtypeStruct((nt, ZG_W), F32),
                   jax.ShapeDtypeStruct((nt, LANES), F32), jax.ShapeDtypeStruct((GATE_ROWS, nt), F32),
                   jax.ShapeDtypeStruct((nt, hw), BF16), jax.ShapeDtypeStruct((nt, hw), BF16),
                   jax.ShapeDtypeStruct((nt, hw), BF16), jax.ShapeDtypeStruct((nt, A_DC), F32),
                   jax.ShapeDtypeStruct((nt, A_DROPE), F32)],
        compiler_params=_cparams(("parallel",)),
        name="in_proj_mla",
    )(*x_ops, mods, w_in_p, w_gt, cos, sin, kcos, gq, gkv, wuq, wuqs, wk, wv)


def _scan_consts():
    n = S_CHUNK
    t = np.arange(n)[:, None]
    s = np.arange(n)[None, :]
    tri = [(s <= t), (s >= t)]
    head_of = np.arange(S_W) // M_DV
    khead = np.arange(M_HEADS * M_DK) // M_DK
    kmask = (head_of[:, None] == khead[None, :]).astype(np.float32)
    vmask = (head_of[:, None] == head_of[None, :]).astype(np.float32)
    bd = (khead[:, None] == head_of[None, :]).astype(np.float32)
    sel, tritile, neg = [], [], []
    for d in (0, 1):
        io, fo = 2 * M_HEADS * d, 2 * M_HEADS * d + M_HEADS
        m = np.zeros((LANES, 3 * S_W), np.float32)
        for h in range(M_HEADS):
            m[fo + h, h * M_DV:(h + 1) * M_DV] = 1.0
            m[io + h, S_W + h * M_DV:S_W + (h + 1) * M_DV] = 1.0
            m[fo + h, 2 * S_W + h * M_DK:2 * S_W + (h + 1) * M_DK] = 1.0
            m[io + h, 2 * S_W + LANES + h * M_DK:2 * S_W + LANES + (h + 1) * M_DK] = 1.0
        sel.append(m)
        tritile.append(np.tile(tri[d].T.astype(np.float32), (1, M_HEADS)))
        neg.append(np.tile(np.where(tri[d], 0.0, -np.inf).astype(np.float32), (1, M_HEADS)))
    masks = [np.eye(n)]
    for lev in range(G_LEVELS):
        b = 1 << lev
        right = (t % (2 * b)) >= b
        same_parent = (t // (2 * b)) == (s // (2 * b))
        masks.append((same_parent & right & ((s % (2 * b)) < b)).astype(np.float32))
    flip = lambda m: m[::-1, ::-1]
    mcat = np.stack([np.stack([np.tile(m, (1, G_HEADS)) for m in masks]),
                     np.stack([np.tile(flip(m), (1, G_HEADS)) for m in masks])])
    gath = []
    for d in (0, 1):
        g = np.zeros((G_SMALL_LEVELS * n, n), np.float32)
        for lev in range(G_SMALL_LEVELS):
            for row in range(n):
                g[lev * n + row, _gla_boundary(row, 1 << lev, d)] = 1.0
        gath.append(g)
    f32 = lambda a: jnp.asarray(a, F32)
    b16 = lambda a: jnp.asarray(a, BF16)
    return dict(tri=b16(np.stack(tri).astype(np.float32)), sel=b16(np.stack(sel)), tritile=b16(np.stack(tritile)),
                neg=f32(np.stack(neg)), kmask=b16(kmask), vmask=b16(vmask), bd=f32(bd),
                bd2=f32(np.concatenate([bd, bd], axis=1)), mcat=f32(mcat), gath=b16(np.stack(gath)))


def _gla_boundary(row, b, d):
    pstart = (row // (2 * b)) * (2 * b)
    return pstart + b - 1 if d == 0 else pstart + b


def _scan_units():
    nch = ROW_BLOCK // S_CHUNK
    return [(d, ci if d == 0 else nch - 1 - ci) for ci in range(nch) for d in (0, 1)]


def _chunk_rows(c):
    return slice(c * S_CHUNK, (c + 1) * S_CHUNK)


def _mlstm_body(first_ref, seq_ref, bwd_ref,
                zmf_ref, zsf_ref, ztf_ref, zmb_ref, zsb_ref, ztb_ref,
                cn0_ref, m0_ref, gb_ref, gbt_ref,
                tri_ref, sel_ref, tritile_ref, neg_ref, kmask_ref, vmask_ref, bd2_ref,
                hf_ref, hb_ref, cout_ref, nout_ref, mout_ref,
                cns_ref, ms_ref):
    r = pl.program_id(0)
    n = S_CHUNK
    qk_w = M_HEADS * M_DK
    kmask = kmask_ref[...]
    vmask = vmask_ref[...]
    bd2 = bd2_ref[...]
    units = _scan_units()
    rows = _chunk_rows
    zm_refs, zs_refs, zt_refs, h_refs = (zmf_ref, zmb_ref), (zsf_ref, zsb_ref), (ztf_ref, ztb_ref), (hf_ref, hb_ref)
    fo = lambda d: 2 * M_HEADS * d + M_HEADS
    last = lambda d: n - 1 if d == 0 else 0
    lane = lax.broadcasted_iota(jnp.int32, (n, LANES), 1)
    head_row = lax.broadcasted_iota(jnp.int32, (1, S_W), 1) // M_DV

    def per_unit(f):
        return [f(d, c) for d, c in units]

    def dot_split(c_left, x_split, c_right):
        hi, lo = x_split
        if c_left is not None:
            return _f32dot(c_left, hi) + _f32dot(c_left, lo)
        return _f32dot(hi, c_right) + _f32dot(lo, c_right)

    q = per_unit(lambda d, c: zm_refs[d][rows(c), 0:qk_w].astype(BF16))
    k = per_unit(lambda d, c: zm_refs[d][rows(c), qk_w:2 * qk_w] * (M_DK ** -0.5))
    v = per_unit(lambda d, c: zm_refs[d][rows(c), 2 * qk_w:2 * qk_w + S_W])
    graw = per_unit(lambda d, c: zs_refs[d][rows(c), :] + gb_ref[...])
    graw_t = per_unit(lambda d, c: zt_refs[d][:, rows(c)] + gbt_ref[...])
    lf = [_split(_log_sigmoid(g)) for g in graw]
    lf_t = [_split(_log_sigmoid(g)) for g in graw_t]
    yield
    bcol = [dot_split(tri_ref[d], s, None) for (d, _), s in zip(units, lf)]
    xs = [_split(jnp.where((lane >= fo(d)) & (lane < fo(d) + M_HEADS), b, g))
          for (d, _), b, g in zip(units, bcol, graw)]
    yield
    ex = [dot_split(None, s, sel_ref[d]) for (d, _), s in zip(units, xs)]
    yield
    b256 = [e[:, 0:S_W] for e in ex]
    i256 = [e[:, S_W:2 * S_W] for e in ex]
    b128 = [e[:, 2 * S_W:2 * S_W + LANES] for e in ex]
    i128 = [e[:, 2 * S_W + LANES:] for e in ex]
    rows_t = [dot_split(None, s, tritile_ref[d]) for (d, _), s in zip(units, lf_t)]
    yield
    b_row = []
    for (d, _), rt in zip(units, rows_t):
        br = jnp.zeros((1, S_W), F32)
        for h in range(M_HEADS):
            br = jnp.where(head_row == h, rt[fo(d) + h:fo(d) + h + 1, :], br)
        b_row.append(br)
    decay = [jnp.exp(b - br + neg_ref[d]) for (d, _), b, br in zip(units, b256, b_row)]
    yield
    k_bd = [jnp.concatenate([ki.astype(BF16)] * M_HEADS, axis=0) * kmask for ki in k]
    qk = [lax.dot_general(qi, ki, NT_DIMS, preferred_element_type=F32) for qi, ki in zip(q, k_bd)]
    yield
    p = [(a * b).astype(BF16) for a, b in zip(qk, decay)]
    i_max = [jnp.max(i, axis=0, keepdims=True) for i in i256]
    scale = [jnp.exp(i - im) for i, im in zip(i256, i_max)]
    yield
    ve = [jnp.concatenate([jnp.concatenate([(vi * sc).astype(BF16)] * M_HEADS, axis=0) * vmask,
                           jnp.concatenate([sc.astype(BF16)] * M_HEADS, axis=0) * vmask], axis=1)
          for vi, sc in zip(v, scale)]
    yield
    nd = [_f32dot(pi, vei) for pi, vei in zip(p, ve)]
    yield
    bl256 = [b[last(d):last(d) + 1, :] for (d, _), b in zip(units, b256)]
    g_max = [jnp.max(bl - b + i, axis=0, keepdims=True) for bl, b, i in zip(bl256, b256, i256)]
    g128 = [b[last(d):last(d) + 1, :] - b + i for (d, _), b, i in zip(units, b128, i128)]
    w = [jnp.exp(g - jnp.max(g, axis=0, keepdims=True)) for g in g128]
    kw = [(ki * wi).astype(BF16) for ki, wi in zip(k, w)]
    yield
    ones = jnp.ones((n, S_W), BF16)
    upd = [lax.dot_general(kwi, jnp.concatenate([vi.astype(BF16), ones], axis=1), TN_DIMS,
                           preferred_element_type=F32) for kwi, vi in zip(kw, v)]
    fresh = first_ref[r] != 0
    zero = first_ref[r] == 2
    state = [(jnp.where(fresh, jnp.where(zero, 0.0, cn0_ref[0, d]), cns_ref[d]),
              jnp.where(fresh, jnp.where(zero, 0.0, m0_ref[0, d]), ms_ref[d])) for d in (0, 1)]
    outs = []
    for i, (d, c) in enumerate(units):
        cn, m = state[d]
        qcn = _f32dot(q[i], cn.astype(BF16))
        a = b256[i] + m
        sig = jnp.maximum(i_max[i], a)
        f1 = jnp.exp(i_max[i] - sig)
        f2 = jnp.exp(a - sig)
        num = f1 * nd[i][:, 0:S_W] + f2 * qcn[:, 0:S_W]
        den = f1 * nd[i][:, S_W:] + f2 * qcn[:, S_W:]
        outs.append(num / jnp.maximum(jnp.abs(den), jnp.exp(-sig)))
        m_new = jnp.maximum(bl256[i] + m, g_max[i])
        alpha = jnp.exp(bl256[i] + m - m_new)
        beta = jnp.exp(g_max[i] - m_new)
        state[d] = (jnp.concatenate([alpha, alpha], axis=1) * cn
                    + (jnp.concatenate([beta, beta], axis=1) * bd2) * upd[i], m_new)
        yield
    for (d, c), o in zip(units, outs):
        h_refs[d][rows(c), :] = o
    for d in (0, 1):
        cn, m = state[d]
        cns_ref[d], ms_ref[d] = cn, m
        mout_ref[0, d] = m
        for h in range(M_HEADS):
            blk = cn[h * M_DK:(h + 1) * M_DK, :]
            cout_ref[0, d, h] = blk[:, h * M_DV:(h + 1) * M_DV]
            nout_ref[0, d, h] = blk[:, S_W + h * M_DV:S_W + (h + 1) * M_DV]


def _mlstm_part(layer, n_ctx, zm, zs, zt, cn0, m0, gb, gbt, sc):
    nt = zm.shape[0]
    kd = M_HEADS * M_DK
    n = S_CHUNK
    fwd = lambda r, first, seq, bwd: (r, 0)
    bwd_ = lambda r, first, seq, bwd: (bwd[r], 0)
    fwd_t = lambda r, first, seq, bwd: (0, r)
    bwd_t = lambda r, first, seq, bwd: (0, bwd[r])
    seq5 = lambda r, first, seq, bwd: (jnp.maximum(seq[r] - n_ctx, 0), layer, 0, 0, 0)
    out4 = lambda r, first, seq, bwd: (jnp.minimum(seq[r], n_ctx), 0, 0, 0)
    out5 = lambda r, first, seq, bwd: (jnp.minimum(seq[r], n_ctx), 0, 0, 0, 0)
    const = lambda r, first, seq, bwd: (0, 0)
    const3 = lambda r, first, seq, bwd: (0, 0, 0)
    return dict(
        body=_mlstm_body,
        in_specs=[pl.BlockSpec((ROW_BLOCK, QKV_W), fwd), pl.BlockSpec((ROW_BLOCK, LANES), fwd),
                  pl.BlockSpec((GATE_ROWS, ROW_BLOCK), fwd_t),
                  pl.BlockSpec((ROW_BLOCK, QKV_W), bwd_), pl.BlockSpec((ROW_BLOCK, LANES), bwd_),
                  pl.BlockSpec((GATE_ROWS, ROW_BLOCK), bwd_t),
                  pl.BlockSpec((1, None, 2, kd, 2 * S_W), seq5), pl.BlockSpec((1, None, 2, 1, S_W), seq5),
                  _layer_spec(gb, layer), _layer_spec(gbt, layer),
                  pl.BlockSpec((2, n, n), const3), pl.BlockSpec((2, LANES, 3 * S_W), const3),
                  pl.BlockSpec((2, n, S_W), const3), pl.BlockSpec((2, n, S_W), const3),
                  pl.BlockSpec((S_W, kd), const), pl.BlockSpec((S_W, S_W), const),
                  pl.BlockSpec((kd, 2 * S_W), const)],
        out_specs=[pl.BlockSpec((ROW_BLOCK, S_W), fwd), pl.BlockSpec((ROW_BLOCK, S_W), bwd_),
                   pl.BlockSpec((1, 2, M_HEADS, M_DK, M_DV), out5), pl.BlockSpec((1, 2, M_HEADS, M_DK, M_DV), out5),
                   pl.BlockSpec((1, 2, 1, S_W), out4)],
        scratch_shapes=[pltpu.VMEM((2, kd, 2 * S_W), F32), pltpu.VMEM((2, 1, S_W), F32)],
        out_shape=[jax.ShapeDtypeStruct((nt, S_W), F32), jax.ShapeDtypeStruct((nt, S_W), F32),
                   jax.ShapeDtypeStruct((n_ctx + 1, 2, M_HEADS, M_DK, M_DV), F32),
                   jax.ShapeDtypeStruct((n_ctx + 1, 2, M_HEADS, M_DK, M_DV), F32),
                   jax.ShapeDtypeStruct((n_ctx + 1, 2, 1, S_W), F32)],
        operands=[zm, zs, zt, zm, zs, zt, cn0, m0, gb, gbt,
                  sc["tri"], sc["sel"], sc["tritile"], sc["neg"], sc["kmask"], sc["vmask"], sc["bd2"]])


def _gla_body(first_ref, seq_ref, bwd_ref,
              zgf_ref, zsf_ref, zgb_ref, zsb_ref, s0_ref, w2_ref, b2_ref,
              tri_ref, gath_ref, mcat_ref, kmask_ref, vmask_ref, bd_ref,
              of_ref, ob_ref, sout_ref, ss_ref):
    r = pl.program_id(0)
    n = S_CHUNK
    kd = G_HEADS * G_DK
    vd = G_HEADS * G_DV
    ones_v = jnp.ones((n, vd), BF16)
    kmask = kmask_ref[...]
    vmask = vmask_ref[...]
    bd = bd_ref[...]
    units = _scan_units()
    rows = _chunk_rows
    zg_refs, zs_refs, o_refs = (zgf_ref, zgb_ref), (zsf_ref, zsb_ref), (of_ref, ob_ref)

    def per_unit(f):
        return [f(d, c) for d, c in units]

    def bd_k(kt):
        return jnp.concatenate([kt] * G_HEADS, axis=0) * kmask

    q = per_unit(lambda d, c: zg_refs[d][rows(c), 0:kd] * (G_DK ** -0.5))
    k = per_unit(lambda d, c: zg_refs[d][rows(c), kd:2 * kd])
    v = per_unit(lambda d, c: zg_refs[d][rows(c), 2 * kd:2 * kd + vd].astype(BF16))
    x = per_unit(lambda d, c: _f32dot(zs_refs[d][rows(c), :].astype(BF16), w2_ref[d]) + b2_ref[d])
    lg = [_log_sigmoid(xi) * (1.0 / G_TAU) for xi in x]
    lg_split = [_split(l) for l in lg]
    yield
    bc = [_f32dot(tri_ref[d], hi) + _f32dot(tri_ref[d], lo)
          for (d, _), (hi, lo) in zip(units, lg_split)]
    bc_split = [_split(b) for b in bc]
    yield
    r_small = [_f32dot(gath_ref[d], hi) + _f32dot(gath_ref[d], lo)
               for (d, _), (hi, lo) in zip(units, bc_split)]
    qb = [qi.astype(BF16) for qi in q]
    kbd = [bd_k(ki.astype(BF16)) for ki in k]
    yield
    att = [mcat_ref[d, 0] * lax.dot_general(qi, ki, NT_DIMS, preferred_element_type=F32)
           for (d, _), qi, ki in zip(units, qb, kbd)]
    for lev in range(G_LEVELS):
        b = 1 << lev
        if lev < G_SMALL_LEVELS:
            r_lev = [rs[lev * n:(lev + 1) * n] for rs in r_small]
        else:
            r_lev = [jnp.concatenate(
                [jnp.broadcast_to(bci[_gla_boundary(p0, b, d):_gla_boundary(p0, b, d) + 1, :], (2 * b, kd))
                 for p0 in range(0, n, 2 * b)], axis=0) for (d, _), bci in zip(units, bc)]
        e_lev = [jnp.exp(-jnp.abs(bci - ri)) for bci, ri in zip(bc, r_lev)]
        qt = [(qi * ei).astype(BF16) for qi, ei in zip(q, e_lev)]
        kt = [bd_k((ki * ei).astype(BF16)) for ki, ei in zip(k, e_lev)]
        yield
        p = [lax.dot_general(qi, ki, NT_DIMS, preferred_element_type=F32) for qi, ki in zip(qt, kt)]
        att = [ai + mcat_ref[d, lev + 1] * pi for (d, _), ai, pi in zip(units, att, p)]
        yield
    v_bd = [jnp.concatenate([vi] * G_HEADS, axis=0) * vmask for vi in v]
    intra = [_f32dot(ai.astype(BF16), vi) for ai, vi in zip(att, v_bd)]
    yield
    q_inc = [(qi * jnp.exp(bci)).astype(BF16) for qi, bci in zip(q, bc)]
    k_suf = [(ki * jnp.exp(bci[(n - 1 if d == 0 else 0):(n if d == 0 else 1), :] - bci)).astype(BF16)
             for (d, _), ki, bci in zip(units, k, bc)]
    decay = [jnp.exp(lax.dot_general(hi, ones_v, TN_DIMS, preferred_element_type=F32)
                     + lax.dot_general(lo, ones_v, TN_DIMS, preferred_element_type=F32))
             for hi, lo in lg_split]
    upd = [bd * lax.dot_general(ki, vi, TN_DIMS, preferred_element_type=F32) for ki, vi in zip(k_suf, v)]
    yield
    fresh = first_ref[r] != 0
    zero = first_ref[r] == 2
    states = [jnp.where(fresh, jnp.where(zero, 0.0, s0_ref[0, d]), ss_ref[d]) for d in (0, 1)]
    outs = []
    for i, (d, c) in enumerate(units):
        inter = _f32dot(q_inc[i], states[d].astype(BF16))
        outs.append(inter + intra[i])
        states[d] = decay[i] * states[d] + upd[i]
        yield
    for (d, c), o in zip(units, outs):
        o_refs[d][rows(c), :] = o
    for d in (0, 1):
        ss_ref[d] = states[d]
        for h in range(G_HEADS):
            sout_ref[0, d, h] = states[d][h * G_DK:(h + 1) * G_DK, h * G_DV:(h + 1) * G_DV]


def _gla_part(layer, n_ctx, zg, zs, s0, w2p, b2p, sc):
    nt = zg.shape[0]
    kd = G_HEADS * G_DK
    vd = G_HEADS * G_DV
    n = S_CHUNK
    fwd = lambda r, first, seq, bwd: (r, 0)
    bwd_ = lambda r, first, seq, bwd: (bwd[r], 0)
    seq5 = lambda r, first, seq, bwd: (jnp.maximum(seq[r] - n_ctx, 0), layer, 0, 0, 0)
    out5 = lambda r, first, seq, bwd: (jnp.minimum(seq[r], n_ctx), 0, 0, 0, 0)
    const = lambda r, first, seq, bwd: (0, 0)
    const3 = lambda r, first, seq, bwd: (0, 0, 0)
    const4 = lambda r, first, seq, bwd: (0, 0, 0, 0)
    return dict(
        body=_gla_body,
        in_specs=[pl.BlockSpec((ROW_BLOCK, QKV_W), fwd), pl.BlockSpec((ROW_BLOCK, LANES), fwd),
                  pl.BlockSpec((ROW_BLOCK, QKV_W), bwd_), pl.BlockSpec((ROW_BLOCK, LANES), bwd_),
                  pl.BlockSpec((1, None, 2, kd, vd), seq5),
                  _layer_spec(w2p, layer), _layer_spec(b2p, layer),
                  pl.BlockSpec((2, n, n), const3),
                  pl.BlockSpec((2, G_SMALL_LEVELS * n, n), const3),
                  pl.BlockSpec((2, G_LEVELS + 1, n, G_HEADS * n), const4),
                  pl.BlockSpec((G_HEADS * n, kd), const), pl.BlockSpec((G_HEADS * n, vd), const),
                  pl.BlockSpec((kd, vd), const)],
        out_specs=[pl.BlockSpec((ROW_BLOCK, vd), fwd), pl.BlockSpec((ROW_BLOCK, vd), bwd_),
                   pl.BlockSpec((1, 2, G_HEADS, G_DK, G_DV), out5)],
        scratch_shapes=[pltpu.VMEM((2, kd, vd), F32)],
        out_shape=[jax.ShapeDtypeStruct((nt, vd), F32), jax.ShapeDtypeStruct((nt, vd), F32),
                   jax.ShapeDtypeStruct((n_ctx + 1, 2, G_HEADS, G_DK, G_DV), F32)],
        operands=[zg, zs, zg, zs, s0, w2p, b2p, sc["tri"], sc["gath"], sc["mcat"],
                  sc["kmask"], sc["vmask"], sc["bd"]])


_DONE = object()


def _scan_call(plan, parts, name):
    n_in = [len(p["operands"]) for p in parts]
    n_out = [len(p["out_specs"]) for p in parts]
    n_scr = [len(p["scratch_shapes"]) for p in parts]

    def kern(first_ref, seq_ref, bwd_ref, *refs):
        ins, outs, scrs = refs[:sum(n_in)], refs[sum(n_in):sum(n_in) + sum(n_out)], refs[sum(n_in) + sum(n_out):]
        @pl.when(pl.program_id(0) == 0)
        def _():
            for s in scrs:
                s[...] = jnp.zeros(s.shape, s.dtype)

        oi = oo = os_ = 0
        bodies = []
        for p, a, b, c in zip(parts, n_in, n_out, n_scr):
            bodies.append(p["body"](first_ref, seq_ref, bwd_ref, *ins[oi:oi + a], *outs[oo:oo + b],
                                    *scrs[os_:os_ + c]))
            oi, oo, os_ = oi + a, oo + b, os_ + c
        while bodies:
            bodies = [g for g in bodies if next(g, _DONE) is not _DONE]

    nb = plan["first"].shape[0]
    grid_spec = pltpu.PrefetchScalarGridSpec(
        num_scalar_prefetch=3,
        grid=(nb,),
        in_specs=[s for p in parts for s in p["in_specs"]],
        out_specs=[s for p in parts for s in p["out_specs"]],
        scratch_shapes=[s for p in parts for s in p["scratch_shapes"]],
    )
    outs = pl.pallas_call(
        kern,
        grid_spec=grid_spec,
        out_shape=[s for p in parts for s in p["out_shape"]],
        compiler_params=_cparams(("arbitrary",)),
        name=name,
    )(plan["first"], plan["seq"], plan["bwd"], *[o for p in parts for o in p["operands"]])
    res, oo = [], 0
    for b in n_out:
        res.append(outs[oo:oo + b])
        oo += b
    return res


def _cache_kv_kernel(ckv_ref, kr_ref, wk_ref, wv_ref, k_ref, v_ref):
    ckvb = ckv_ref[...].astype(BF16)
    kn = _f32dot(ckvb, wk_ref[...])
    v_ref[...] = _with_ones_lane(_f32dot(ckvb, wv_ref[...])).astype(BF16)
    kr = kr_ref[...]
    for h in range(A_HEADS):
        sl = slice(h * A_HPAD, (h + 1) * A_HPAD)
        k_ref[:, sl] = (kn[:, sl] + kr).astype(BF16)


def _cache_kv_call(layer, ckv, krp, wk, wv):
    n_dec, _, past_len, _ = ckv.shape
    hw = A_HEADS * A_HPAD
    row = lambda b: (b, 0)
    cache = lambda b: (b, layer, 0, 0)
    return pl.pallas_call(
        _cache_kv_kernel,
        grid=(n_dec,),
        in_specs=[pl.BlockSpec((None, None, past_len, A_DC), cache),
                  pl.BlockSpec((None, None, past_len, LANES), cache),
                  _layer_spec(wk, layer), _layer_spec(wv, layer)],
        out_specs=[pl.BlockSpec((past_len, hw), row), pl.BlockSpec((past_len, hw), row)],
        out_shape=[jax.ShapeDtypeStruct((n_dec * past_len, hw), BF16),
                   jax.ShapeDtypeStruct((n_dec * past_len, hw), BF16)],
        compiler_params=_cparams(("parallel",)),
        name="mla_cache_kv",
    )(ckv, krp, wk, wv)


def _attn_kernel(*refs, n_seg, heads_per_group):
    q_ref = refs[0]
    kv_refs = refs[1:1 + 2 * n_seg]
    o_ref = refs[1 + 2 * n_seg]
    tiles = []
    for j in range(n_seg):
        nk = kv_refs[2 * j].shape[0]
        tk = min(nk, A_KEY_TILE)
        tiles += [(j, t0, tk) for t0 in range(0, nk, tk)]
    head = lambda h: slice(h * A_HPAD, (h + 1) * A_HPAD)

    def scores(h, tile):
        j, t0, tk = tile
        return lax.dot_general(q_ref[:, head(h)], kv_refs[2 * j][t0:t0 + tk, head(h)], NT_DIMS,
                               preferred_element_type=F32)

    def lane_max(acc, s):
        for c0 in range(0, s.shape[1], LANES):
            part = s[:, c0:c0 + LANES]
            acc = part if acc is None else jnp.maximum(acc, part)
        return acc

    groups = [list(range(g0, min(g0 + heads_per_group, A_HEADS))) for g0 in range(0, A_HEADS, heads_per_group)]
    s_cur = {h: [scores(h, t) for t in tiles] for h in groups[0]}
    for gi, grp in enumerate(groups):
        nxt = groups[gi + 1] if gi + 1 < len(groups) else []
        m = {}
        for h in grp:
            mx = None
            for s in s_cur[h]:
                mx = lane_max(mx, s)
            m[h] = jnp.max(mx, axis=1, keepdims=True)
        s_next = {h: [] for h in nxt}
        acc = {h: None for h in grp}
        for ti, tile in enumerate(tiles):
            j, t0, tk = tile
            for h in nxt:
                s_next[h].append(scores(h, tile))
            p = {h: jnp.exp((s_cur[h][ti] - m[h]).astype(BF16)) for h in grp}
            for h in grp:
                pv = _f32dot(p[h], kv_refs[2 * j + 1][t0:t0 + tk, head(h)])
                acc[h] = pv if acc[h] is None else acc[h] + pv
        for h in grp:
            o_ref[:, head(h)] = (acc[h] / acc[h][:, A_DV:A_DV + 1]).astype(BF16)
        s_cur = s_next


def _attn_ctx_call(q, k, v, n_rows):
    hw = A_HEADS * A_HPAD
    tm = ROW_BLOCK
    row = lambda i: (i, 0)
    return pl.pallas_call(
        functools.partial(_attn_kernel, n_seg=1, heads_per_group=A_HEADS),
        grid=(n_rows // tm,),
        in_specs=[pl.BlockSpec((tm, hw), row)] * 3,
        out_specs=pl.BlockSpec((tm, hw), row),
        out_shape=jax.ShapeDtypeStruct((n_rows, hw), BF16),
        compiler_params=_cparams(("parallel",)),
        name="mla_attn_ctx",
    )(q, k, v)


def _attn_dec_call(q, k, v, kc, vc, n_ctx_rows, n_dec, dec_len, past_len):
    hw = A_HEADS * A_HPAD
    tm = WIDE_ROWS
    nq = dec_len // tm
    assert n_ctx_rows % dec_len == 0
    qmap = lambda b, i: (n_ctx_rows // tm + b * nq + i, 0)
    own = lambda b, i: (n_ctx_rows // dec_len + b, 0)
    return pl.pallas_call(
        functools.partial(_attn_kernel, n_seg=2, heads_per_group=2),
        grid=(n_dec, nq),
        in_specs=[pl.BlockSpec((tm, hw), qmap),
                  pl.BlockSpec((past_len, hw), lambda b, i: (b, 0)),
                  pl.BlockSpec((past_len, hw), lambda b, i: (b, 0)),
                  pl.BlockSpec((dec_len, hw), own),
                  pl.BlockSpec((dec_len, hw), own)],
        out_specs=pl.BlockSpec((tm, hw), lambda b, i: (b * nq + i, 0)),
        out_shape=jax.ShapeDtypeStruct((n_dec * dec_len, hw), BF16),
        compiler_params=_cparams(("parallel", "arbitrary")),
        name="mla_attn_dec",
    )(q, kc, vc, k, v)


def _out_ffn_kernel(*refs, n_x, nb0, nb_ctx, alpha, blk0):
    (hf_ref, hb_ref, of_ref, ob_ref, actx_ref, adec_ref, mo_ref, gg_ref) = refs[:8]
    x_refs = refs[8:8 + n_x]
    (mod_ref, mng_ref, gng_ref, avg_ref, wm_ref, wg_ref, wa_ref, l1g_ref, l1b_ref,
     wfg_ref, wfu_ref, wfd_ref, l2g_ref, l2b_ref, o_ref) = refs[8 + n_x:]
    d = D_MODEL
    avg = avg_ref[...]
    mod = mod_ref[0]
    is_ctx = pl.program_id(0) + blk0 < nb_ctx
    x_in = _read_x(x_refs, nb0, blk0)
    tm = o_ref.shape[0]
    parts = [slice(p0, p0 + ROW_BLOCK) for p0 in range(0, tm, ROW_BLOCK)]
    f = wfg_ref.shape[1]
    tiles = -(-f // MXU_TILE)
    edges = [min(f, MXU_TILE * (tiles * c // FFN_CHUNKS)) for c in range(FFN_CHUNKS)] + [f]

    def ffn_chunk(hh, c):
        sl = slice(edges[c], edges[c + 1])
        g = _f32dot(hh, wfg_ref[:, sl])
        u = _f32dot(hh, wfu_ref[:, sl])
        return _f32dot((g * _sigmoid(g) * u).astype(BF16), wfd_ref[sl, :])

    each = lambda fn: [fn(s) for s in parts]
    a = each(lambda s: jnp.where(is_ctx, actx_ref[s, :], adec_ref[s, :]))
    hm = each(lambda s: hf_ref[s, :] + hb_ref[s, :])
    og = each(lambda s: of_ref[s, :] + ob_ref[s, :])
    hc = [x - _dot_x_c(x, avg) for x in hm]
    hvar = [_dot_x_c(x * x, avg) for x in hc]
    gms = [_dot_x_c(x * x, avg) for x in og]
    m_out = [x * lax.rsqrt(v + 1e-6) * mng_ref[...] * _sigmoid(mo_ref[s, :]) for x, v, s in zip(hc, hvar, parts)]
    gg = each(lambda s: gg_ref[s, :])
    g_out = [x * lax.rsqrt(v + 1e-6) * gng_ref[...] * (g * _sigmoid(g)) for x, v, g in zip(og, gms, gg)]
    mix = [_f32dot(m.astype(BF16), wm_ref[...]) + _f32dot(g.astype(BF16), wg_ref[...]) + _f32dot(ai, wa_ref[...])
           for m, g, ai in zip(m_out, g_out, a)]
    y = [alpha * x_in[s, :] + mod[:, 2 * d:3 * d] * mi for s, mi in zip(parts, mix)]
    x1 = jnp.concatenate([_layer_norm(yi, l1g_ref[...], l1b_ref[...]) for yi in y], axis=0)
    hh = (x1 * (1.0 + mod[:, 4 * d:5 * d]) + mod[:, 3 * d:4 * d]).astype(BF16)
    acc = ffn_chunk(hh, 0)
    for c in range(1, FFN_CHUNKS):
        acc = acc + ffn_chunk(hh, c)
    o_ref[...] = _layer_norm(alpha * x1 + mod[:, 5 * d:6 * d] * acc, l2g_ref[...], l2b_ref[...])


def _out_ffn_call(layer, hf, hb, of, ob, a_ctx, a_dec, zm, zg, x, mods, mod_map, mng, gng, avg, wm, wg, wa, l1g, l1b,
                  wfg, wfu, wfd, l2g, l2b, alpha, blk0=0, n_blk=None):
    tm = WIDE_ROWS
    x_specs, x_ops = _x_specs(x, tm, blk0)
    nt = sum(a.shape[0] for a in x_ops)
    n_blk = nt // tm if n_blk is None else n_blk
    d = D_MODEL
    hw = M_HEADS * M_DV
    ahw = A_HEADS * A_HPAD
    nb_ctx = a_ctx.shape[0] // tm
    row = lambda i: (i + blk0, 0)
    gate_col = lambda i: (i + blk0, QKV_W // hw)
    resident = dict(pipeline_mode=pl.Buffered(1))
    return pl.pallas_call(
        functools.partial(_out_ffn_kernel, n_x=len(x_ops), nb0=x_ops[0].shape[0] // tm, nb_ctx=nb_ctx,
                          alpha=alpha, blk0=blk0),
        grid=(n_blk,),
        in_specs=[pl.BlockSpec((tm, hw), row)] * 4
                 + [pl.BlockSpec((tm, ahw), lambda i: (jnp.minimum(i + blk0, nb_ctx - 1), 0)),
                    pl.BlockSpec((tm, ahw), lambda i: (jnp.maximum(i + blk0 - nb_ctx, 0), 0)),
                    pl.BlockSpec((tm, hw), gate_col), pl.BlockSpec((tm, hw), gate_col)]
                 + x_specs
                 + [pl.BlockSpec((1, 1, 6 * d), lambda i: (mod_map((i + blk0) * (tm // ROW_BLOCK)), 0, 0)),
                    _layer_spec(mng, layer), _layer_spec(gng, layer),
                    pl.BlockSpec((hw, hw), lambda i: (0, 0)),
                    _layer_spec(wm, layer, **resident), _layer_spec(wg, layer, **resident),
                    _layer_spec(wa, layer, **resident),
                    _layer_spec(l1g, layer), _layer_spec(l1b, layer),
                    _layer_spec(wfg, layer, **resident), _layer_spec(wfu, layer, **resident),
                    _layer_spec(wfd, layer, **resident),
                    _layer_spec(l2g, layer), _layer_spec(l2b, layer)],
        out_specs=pl.BlockSpec((tm, d), lambda i: (i, 0)),
        out_shape=jax.ShapeDtypeStruct((n_blk * tm, d), F32),
        compiler_params=_cparams(("parallel",)),
        name="out_proj_ffn",
    )(hf, hb, of, ob, a_ctx, a_dec, zm, zg, *x_ops, mods, mng, gng, avg, wm, wg, wa, l1g, l1b,
      wfg, wfu, wfd, l2g, l2b)


def _block_diag_heads(t):
    h = t.shape[-3]
    eye = jnp.eye(h, dtype=t.dtype)
    out = jnp.einsum('...hkv,hg->...hkgv', t, eye)
    return out.reshape(t.shape[:-3] + (h * t.shape[-2], h * t.shape[-1]))


def _rope_tables(dec_len):
    half = A_DROPE // 4
    inv = ROPE_BASE ** (-jnp.arange(half, dtype=F32) / half)
    pos = jnp.arange(dec_len, dtype=jnp.int32)
    rows = (pos // GRID_W).astype(F32)[:, None] * inv[None, :]
    cols = (pos % GRID_W).astype(F32)[:, None] * inv[None, :]
    cr, sr, cc, sc = jnp.cos(rows), jnp.sin(rows), jnp.cos(cols), jnp.sin(cols)
    cos32 = jnp.concatenate([cr, cr, cc, cc], axis=1)
    sin32 = jnp.concatenate([-sr, sr, -sc, sc], axis=1)
    ones_lo = jnp.ones((dec_len, KR_LO), F32)
    ones_hi = jnp.ones((dec_len, A_HPAD - KR_HI), F32)
    cos = jnp.concatenate([ones_lo, cos32, ones_hi], axis=1)
    sin = jnp.concatenate([0 * ones_lo, sin32, 0 * ones_hi], axis=1)
    kcos = jnp.concatenate([0 * ones_lo, cos32, 0 * ones_hi], axis=1)
    lane = jnp.arange(A_HPAD)
    rope_lane = ((lane >= KR_LO) & (lane < KR_HI)).astype(F32)[None, :]
    ctx_cos = jnp.ones((WIDE_ROWS, A_HPAD), F32)
    ctx_sin = jnp.zeros((WIDE_ROWS, A_HPAD), F32)
    ctx_kcos = jnp.broadcast_to(rope_lane, (WIDE_ROWS, A_HPAD))
    return (jnp.concatenate([ctx_cos, cos]), jnp.concatenate([ctx_sin, sin]), jnp.concatenate([ctx_kcos, kcos]))


def _rope_partner(w):
    e = A_DROPE // 4
    return jnp.concatenate([w[..., e:2 * e], w[..., 0:e], w[..., 3 * e:4 * e], w[..., 2 * e:3 * e]], axis=-1)


def kernel(x_prompt, x_sample, state_mlstm_C, state_mlstm_n, state_mlstm_m, state_gla_S, cache_mla_ckv, cache_mla_krope, c, c_ctx, w_ada, b_ada, w_in, m_gate_b, m_norm_g, g_w2, g_b2, g_norm_g, a_q_norm_g, a_kv_norm_g, a_w_uq, a_w_ukv, w_out, ln1_g, ln1_b, w_ffn_gate, w_ffn_up, w_ffn_down, ln2_g, ln2_b):
    n_ctx, ctx_len, d = x_prompt.shape
    n_dec, dec_len, _ = x_sample.shape
    depth = w_in.shape[0]
    past_len = cache_mla_ckv.shape[2]
    assert d == D_MODEL and ctx_len == ROW_BLOCK and dec_len % ROW_BLOCK == 0 and past_len % 8 == 0
    assert (n_ctx * ctx_len) % WIDE_ROWS == 0 and dec_len % WIDE_ROWS == 0
    assert n_dec + 1 <= COND_ROWS
    n_ctx_rows = n_ctx * ctx_len
    n_dec_rows = n_dec * dec_len
    nt = n_ctx_rows + n_dec_rows
    nb_ctx = n_ctx_rows // ROW_BLOCK
    nb_seq = dec_len // ROW_BLOCK
    alpha = (2.0 * depth) ** 0.25

    blocks = np.arange(nt // ROW_BLOCK)
    dec_b = np.maximum(blocks - nb_ctx, 0)
    is_ctx = blocks < nb_ctx
    plan = {
        "first": jnp.asarray(np.where(is_ctx, 2, (dec_b % nb_seq) == 0).astype(np.int32)),
        "seq": jnp.asarray(np.where(is_ctx, blocks, n_ctx + dec_b // nb_seq).astype(np.int32)),
        "bwd": jnp.asarray(np.where(is_ctx, blocks,
                                    nb_ctx + (dec_b // nb_seq) * nb_seq + nb_seq - 1 - dec_b % nb_seq).astype(np.int32)),
    }

    def mod_map_for(layer):
        def mod_map(i):
            return layer * COND_ROWS + jnp.where(i < nb_ctx, 0, 1 + (i - nb_ctx) // nb_seq)
        return mod_map

    cond8 = jnp.zeros((COND_ROWS, d), F32).at[0].set(c_ctx).at[1:1 + n_dec].set(c)
    mods = _ada_call(cond8, w_ada, b_ada).reshape(depth * COND_ROWS, 1, 6 * d)

    pts = [int(p) for p in np.cumsum(IN_SPLITS)[:-1]]
    (wmq, wmk, wmv, wmo, wmg, wgq, wgk, wgv, wgg, wga, wacq, wackv, wakr) = jnp.split(w_in.astype(BF16), pts, axis=-1)
    zpad = lambda n: jnp.zeros((depth, d, n), BF16)
    s1 = jnp.concatenate([wmg, wga, zpad(KR_LO - GA_LO - 2 * G_RANK), wakr, zpad(LANES - KR_HI)], axis=-1)
    s2 = jnp.concatenate([zpad(KR_LO), _rope_partner(wakr), zpad(LANES - KR_HI)], axis=-1)
    w_in_p = jnp.concatenate([wmq, wmk, wmv, wmo, wgq, wgk, wgv, wgg, wacq, wackv, s1, s2], axis=-1)
    w_gt = jnp.swapaxes(wmg, 1, 2)
    gate_b = jnp.zeros((depth, 1, LANES), F32).at[:, 0, 0:GATE_ROWS].set(m_gate_b)
    gate_bt = m_gate_b.reshape(depth, GATE_ROWS, 1)
    w2p = jnp.zeros((depth, 2, LANES, G_HEADS * G_DK), F32)
    w2p = (w2p.at[:, 0, GA_LO:GA_LO + G_RANK].set(g_w2[:, 0])
           .at[:, 1, GA_LO + G_RANK:GA_LO + 2 * G_RANK].set(g_w2[:, 1]).astype(BF16))
    b2p = g_b2.reshape(depth, 2, 1, G_HEADS * G_DK)
    uq = a_w_uq.reshape(depth, A_DQ, A_HEADS, A_DNOPE + A_DROPE)
    hp = lambda n: jnp.zeros((depth, uq.shape[1], A_HEADS, n), F32)
    wuq = jnp.concatenate([uq, hp(A_HPAD - KR_HI)], axis=-1).reshape(depth, A_DQ, -1).astype(BF16)
    wuqs = jnp.concatenate([hp(KR_LO), _rope_partner(uq[..., A_DNOPE:]), hp(A_HPAD - KR_HI)],
                           axis=-1).reshape(depth, A_DQ, -1).astype(BF16)
    ukv = a_w_ukv.reshape(depth, A_DC, A_HEADS, A_DNOPE + A_DV)
    kp = jnp.zeros((depth, A_DC, A_HEADS, A_HPAD - A_DNOPE), F32)
    wk = jnp.concatenate([ukv[..., :A_DNOPE], kp], axis=-1).reshape(depth, A_DC, -1).astype(BF16)
    wv = jnp.concatenate([ukv[..., A_DNOPE:], kp], axis=-1).reshape(depth, A_DC, -1).astype(BF16)
    mw, gw = M_HEADS * M_DV, G_HEADS * G_DV
    wo_m = w_out[:, 0:mw].astype(BF16)
    wo_g = w_out[:, mw:mw + gw].astype(BF16)
    wo_a = w_out[:, mw + gw:].reshape(depth, A_HEADS, A_DV, d)
    wo_a = jnp.concatenate([wo_a, jnp.zeros((depth, A_HEADS, A_HPAD - A_DV, d), F32)], axis=2)
    wo_a = wo_a.reshape(depth, A_HEADS * A_HPAD, d).astype(BF16)
    wfg, wfu, wfd = w_ffn_gate.astype(BF16), w_ffn_up.astype(BF16), w_ffn_down.astype(BF16)

    sconst = _scan_consts()
    head_lanes = jnp.asarray((np.arange(M_HEADS)[:, None] == (np.arange(S_W) // M_DV)[None, :]).astype(np.float32))
    hi = np.arange(mw) // M_DV
    avg = jnp.asarray((hi[:, None] == hi[None, :]).astype(np.float32) / M_DV, BF16)
    cos, sin, kcos = _rope_tables(dec_len)
    nbw_ctx = n_ctx_rows // WIDE_ROWS
    nbw_seq = dec_len // WIDE_ROWS

    def rope_map(i):
        return jnp.where(i < nbw_ctx, 0, 1 + (i - nbw_ctx) % nbw_seq)

    krp_all = jnp.zeros(cache_mla_krope.shape[:-1] + (LANES,), F32).at[..., KR_LO:KR_HI].set(cache_mla_krope)
    n_bd = jnp.einsum('bldhk,hg->bldhkg', state_mlstm_n, head_lanes)
    cn0 = jnp.concatenate([_block_diag_heads(state_mlstm_C),
                           n_bd.reshape(n_dec, depth, 2, M_HEADS * M_DK, S_W)], axis=-1)
    m0 = jnp.repeat(state_mlstm_m, M_DV, axis=-1)[:, :, :, None, :]
    s0 = _block_diag_heads(state_gla_S)
    vec = lambda p: p[:, None, :]

    x = (x_prompt.reshape(n_ctx_rows, d), x_sample.reshape(n_dec_rows, d))
    new_c, new_n, new_m, new_s, new_ckv, new_kr = [], [], [], [], [], []
    for l in range(depth):
        mod_map = mod_map_for(l)
        zm, zg, zs, zt, q, k, v, ckv, kr = _in_mla_call(
            l, x, mods, mod_map, rope_map, w_in_p, w_gt, cos, sin, kcos,
            vec(a_q_norm_g), vec(a_kv_norm_g), wuq, wuqs, wk, wv)

        (hf, hb, c_fin, n_fin, m_fin), (of, ob, s_fin) = _scan_call(
            plan,
            [_mlstm_part(l, n_ctx, zm, zs, zt, cn0, m0, gate_b, gate_bt, sconst),
             _gla_part(l, n_ctx, zg, zs, s0, w2p, b2p, sconst)],
            "mixer_scan")

        kc, vc = _cache_kv_call(l, cache_mla_ckv, krp_all, wk, wv)
        a_ctx = _attn_ctx_call(q, k, v, n_ctx_rows)
        a_dec = _attn_dec_call(q, k, v, kc, vc, n_ctx_rows, n_dec, dec_len, past_len)

        out_ffn = functools.partial(
            _out_ffn_call, l, hf, hb, of, ob, a_ctx, a_dec, zm, zg, x, mods, mod_map,
            vec(m_norm_g), vec(g_norm_g), avg, wo_m, wo_g, wo_a, vec(ln1_g), vec(ln1_b),
            wfg, wfu, wfd, vec(ln2_g), vec(ln2_b), alpha)
        if l + 1 < depth:
            x = out_ffn()
        else:
            y_prompt = out_ffn(blk0=0, n_blk=nbw_ctx).reshape(n_ctx, ctx_len, d)
            y_sample = out_ffn(blk0=nbw_ctx, n_blk=n_dec * nbw_seq).reshape(n_dec, dec_len, d)

        new_c.append(c_fin[:n_ctx])
        new_n.append(n_fin[:n_ctx, ..., 0])
        new_m.append(m_fin[:n_ctx, :, 0, ::M_DV])
        new_s.append(s_fin[:n_ctx])
        new_ckv.append(ckv[:n_ctx_rows].reshape(n_ctx, ctx_len, A_DC))
        new_kr.append(kr[:n_ctx_rows].reshape(n_ctx, ctx_len, A_DROPE))

    st = lambda xs: jnp.stack(xs, axis=1)
    return (y_prompt, y_sample, st(new_c), st(new_n), st(new_m), st(new_s), st(new_ckv), st(new_kr))
```
